```python
import math
import jax, jax.numpy as jnp
from jax import lax
import numpy as np

D_MODEL = 2048
BATCH = 8
SEQ = 2048
DEPTH = 2

N_HEADS = 16
N_KV_HEADS = 4
HEAD_DIM = D_MODEL // N_HEADS
WINDOW = 128
BLOCK = 128
GMLP_GROUPS = 16
GMLP_GROUP_DIM = D_MODEL // GMLP_GROUPS
GMLP_WIDTH = GMLP_GROUPS * GMLP_GROUP_DIM
CHUNK = 128
D_FF_DENSE = 5632
N_EXPERTS = 8
TOP_K = 2
D_FF_EXPERT = 7168
EPS = 1e-6
NEG_INF = -1e30

kernel_name = 'hybrid_swa_gmlp_moe_encoder'


def rmsnorm(x, g):
    xf = x.astype(jnp.float32)
    y = xf * lax.rsqrt(jnp.mean(xf * xf, axis=-1, keepdims=True) + EPS)
    return (y * g.astype(jnp.float32)).astype(x.dtype)


def alibi_slopes(n):
    return jnp.exp2(-8.0 * jnp.arange(1, n + 1, dtype=jnp.float32) / n)


def windowed_gqa_attention(h, w_qkv, q_norm, k_norm, sink, w_o):
    B, S, _ = h.shape
    nb = S // BLOCK
    G = N_HEADS // N_KV_HEADS
    qkv = h @ w_qkv
    q, k, v = jnp.split(qkv, [N_HEADS * HEAD_DIM, (N_HEADS + N_KV_HEADS) * HEAD_DIM], axis=-1)
    q = rmsnorm(q.reshape(B, S, N_HEADS, HEAD_DIM), q_norm)
    k = rmsnorm(k.reshape(B, S, N_KV_HEADS, HEAD_DIM), k_norm)
    v = v.reshape(B, S, N_KV_HEADS, HEAD_DIM)
    q = q.reshape(B, nb, BLOCK, N_KV_HEADS, G, HEAD_DIM)

    def band(t):
        tp = jnp.pad(t, ((0, 0), (BLOCK, BLOCK), (0, 0), (0, 0)))
        tb = tp.reshape(B, nb + 2, BLOCK, N_KV_HEADS, HEAD_DIM)
        return jnp.concatenate([tb[:, :-2], tb[:, 1:-1], tb[:, 2:]], axis=2)

    kb, vb = band(k), band(v)
    scores = jnp.einsum('bnqkgd,bnskd->bkgnqs', q, kb,
                        preferred_element_type=jnp.float32) * (HEAD_DIM ** -0.5)
    qi = jnp.arange(BLOCK)[:, None]
    sj = jnp.arange(3 * BLOCK)[None, :]
    dist = jnp.abs(qi - sj + BLOCK)
    key_pos = jnp.arange(nb)[:, None, None] * BLOCK - BLOCK + sj[None]
    valid = (dist[None] <= WINDOW) & (key_pos >= 0) & (key_pos < S)
    slopes = alibi_slopes(N_HEADS).reshape(N_KV_HEADS, G)
    bias = -slopes[:, :, None, None] * dist.astype(jnp.float32)
    scores = jnp.where(valid, scores + bias[:, :, None], NEG_INF)
    s_logit = sink.astype(jnp.float32).reshape(N_KV_HEADS, G)[None, :, :, None, None, None]
    m = jnp.maximum(jnp.max(scores, axis=-1, keepdims=True), s_logit)
    p = jnp.exp(scores - m)
    probs = p / (jnp.sum(p, axis=-1, keepdims=True) + jnp.exp(s_logit - m))
    out = jnp.einsum('bkgnqs,bnskd->bnqkgd', probs.astype(vb.dtype), vb)
    return out.reshape(B, S, N_HEADS * HEAD_DIM) @ w_o


def chunked_spatial_gating(h, w_in, v_norm, w_s, b_s, w_out):
    B, S, _ = h.shape
    nc = S // CHUNK
    z = jax.nn.gelu(h @ w_in)
    u, v = jnp.split(z, 2, axis=-1)
    v = rmsnorm(v, v_norm).reshape(B, nc, CHUNK, GMLP_GROUPS, GMLP_GROUP_DIM)
    v = jnp.einsum('gts,bcsgd->bctgd', w_s, v) + b_s.T[:, :, None]
    y = u * v.reshape(B, S, GMLP_WIDTH)
    return y @ w_out


def dense_swiglu(h, w_gate_up, w_down):
    g, u = jnp.split(h @ w_gate_up, 2, axis=-1)
    return (jax.nn.silu(g) * u) @ w_down


def moe_swiglu(h, w_router, we_gate, we_up, we_down):
    B, S, D = h.shape
    t = h.reshape(B * S, D)
    logits = (t @ w_router).astype(jnp.float32)
    top_vals, top_idx = lax.top_k(logits, TOP_K)
    top_w = jax.nn.softmax(top_vals, axis=-1)
    gates = jnp.sum(jax.nn.one_hot(top_idx, N_EXPERTS, dtype=jnp.float32) * top_w[..., None], axis=1)
    gates = gates.astype(t.dtype)
    y = jnp.zeros_like(t)
    for e in range(N_EXPERTS):
        he = jax.nn.silu(t @ we_gate[e]) * (t @ we_up[e])
        y = y + gates[:, e:e + 1] * (he @ we_down[e])
    return y.reshape(B, S, D)


def setup_inputs(seed: int = 0) -> dict:
    key = jax.random.key(seed)
    ks = jax.random.split(key, 24)
    f32 = jnp.float32

    def nrm(k, shape, scale):
        return jax.random.normal(k, shape, f32) * scale

    def gain(k, n):
        return 1.0 + 0.02 * jax.random.normal(k, (n,), f32)

    qkv_out = (N_HEADS + 2 * N_KV_HEADS) * HEAD_DIM
    return {
        'x': jax.random.normal(ks[0], (BATCH, SEQ, D_MODEL), f32),
        'l0_mix_norm': gain(ks[1], D_MODEL),
        'l0_w_qkv': nrm(ks[2], (D_MODEL, qkv_out), D_MODEL ** -0.5),
        'l0_q_norm': gain(ks[3], HEAD_DIM),
        'l0_k_norm': gain(ks[4], HEAD_DIM),
        'l0_sink': nrm(ks[5], (N_HEADS,), 1.0),
        'l0_w_o': nrm(ks[6], (N_HEADS * HEAD_DIM, D_MODEL), (N_HEADS * HEAD_DIM) ** -0.5),
        'l0_ffn_norm': gain(ks[7], D_MODEL),
        'l0_w_gate_up': nrm(ks[8], (D_MODEL, 2 * D_FF_DENSE), D_MODEL ** -0.5),
        'l0_w_down': nrm(ks[9], (D_FF_DENSE, D_MODEL), D_FF_DENSE ** -0.5),
        'l1_mix_norm': gain(ks[10], D_MODEL),
        'l1_w_in': nrm(ks[11], (D_MODEL, 2 * GMLP_WIDTH), D_MODEL ** -0.5),
        'l1_v_norm': gain(ks[12], GMLP_WIDTH),
        'l1_w_s': nrm(ks[13], (GMLP_GROUPS, CHUNK, CHUNK), CHUNK ** -0.5),
        'l1_b_s': 1.0 + 0.02 * jax.random.normal(ks[14], (GMLP_GROUPS, CHUNK), f32),
        'l1_w_out': nrm(ks[15], (GMLP_WIDTH, D_MODEL), GMLP_WIDTH ** -0.5),
        'l1_ffn_norm': gain(ks[16], D_MODEL),
        'l1_w_router': nrm(ks[17], (D_MODEL, N_EXPERTS), D_MODEL ** -0.5),
        'l1_we_gate': nrm(ks[18], (N_EXPERTS, D_MODEL, D_FF_EXPERT), D_MODEL ** -0.5),
        'l1_we_up': nrm(ks[19], (N_EXPERTS, D_MODEL, D_FF_EXPERT), D_MODEL ** -0.5),
        'l1_we_down': nrm(ks[20], (N_EXPERTS, D_FF_EXPERT, D_MODEL), D_FF_EXPERT ** -0.5),
    }


def reference(x,
              l0_mix_norm, l0_w_qkv, l0_q_norm, l0_k_norm, l0_sink, l0_w_o,
              l0_ffn_norm, l0_w_gate_up, l0_w_down,
              l1_mix_norm, l1_w_in, l1_v_norm, l1_w_s, l1_b_s, l1_w_out,
              l1_ffn_norm, l1_w_router, l1_we_gate, l1_we_up, l1_we_down):
    layers = [
        dict(mix_norm=l0_mix_norm, mix=(l0_w_qkv, l0_q_norm, l0_k_norm, l0_sink, l0_w_o),
             ffn_norm=l0_ffn_norm, ffn=(l0_w_gate_up, l0_w_down)),
        dict(mix_norm=l1_mix_norm, mix=(l1_w_in, l1_v_norm, l1_w_s, l1_b_s, l1_w_out),
             ffn_norm=l1_ffn_norm, ffn=(l1_w_router, l1_we_gate, l1_we_up, l1_we_down)),
    ]
    for i in range(DEPTH):
        p = layers[i]
        h = rmsnorm(x, p['mix_norm'])
        if i % 2 == 0:
            x = x + windowed_gqa_attention(h, *p['mix'])
        else:
            x = x + chunked_spatial_gating(h, *p['mix'])
        h = rmsnorm(x, p['ffn_norm'])
        if i % 2 == 0:
            x = x + dense_swiglu(h, *p['ffn'])
        else:
            x = x + moe_swiglu(h, *p['ffn'])
    return x
```

```python
import functools

import jax
import jax.numpy as jnp
from jax import lax
from jax.experimental import pallas as pl
from jax.experimental.pallas import tpu as pltpu

F32 = jnp.float32
BF16 = jnp.bfloat16

EPS = 1e-6
NEG_INF = -1e30
LANES = 128
HEAD_DIM = 128
N_HEADS = 16
N_KV_HEADS = 4
WINDOW = 128
BLOCK = 128
CHUNK = 128
N_EXPERTS = 8
TOP_K = 2

MIB = 1024 * 1024
VMEM_LIMIT = 56 * MIB


def _params(*sem):
    return pltpu.CompilerParams(dimension_semantics=sem, vmem_limit_bytes=VMEM_LIMIT)


def _rms(x, gain):
    ms = jnp.mean(x * x, axis=-1, keepdims=True)
    return x * lax.rsqrt(ms + EPS) * gain


def _qkv_kernel(x_ref, g_ref, w_ref, hg_ref, o_ref, *, n_norm_cols, col_chunk):
    h = _rms(x_ref[...], g_ref[...]).astype(BF16)
    n_out = o_ref.shape[1]
    for c0 in range(0, n_out, col_chunk):
        y = jnp.dot(h, w_ref[:, c0:c0 + col_chunk], preferred_element_type=F32)
        for h0 in range(0, col_chunk, HEAD_DIM):
            col = c0 + h0
            yh = y[:, h0:h0 + HEAD_DIM]
            if col < n_norm_cols:
                yh = _rms(yh, hg_ref[:, col:col + HEAD_DIM])
            o_ref[:, col:col + HEAD_DIM] = yh.astype(BF16)


def _qkv_proj(x, gain, w, head_gain, n_norm_cols, tm=512):
    n, d = x.shape
    nq = w.shape[1]
    kern = functools.partial(_qkv_kernel, n_norm_cols=n_norm_cols, col_chunk=512)
    return pl.pallas_call(
        kern,
        grid=(n // tm,),
        in_specs=[
            pl.BlockSpec((tm, d), lambda i: (i, 0)),
            pl.BlockSpec((1, d), lambda i: (0, 0)),
            pl.BlockSpec((d, nq), lambda i: (0, 0)),
            pl.BlockSpec((1, nq), lambda i: (0, 0)),
        ],
        out_specs=pl.BlockSpec((tm, nq), lambda i: (i, 0)),
        out_shape=jax.ShapeDtypeStruct((n, nq), BF16),
        compiler_params=_params("parallel"),
        name="qkv_proj",
    )(x, gain, w, head_gain)


def _attn_kernel(sink_ref, q_ref, kp_ref, kc_ref, kn_ref, vp_ref, vc_ref, vn_ref, o_ref,
                 *, seq, slopes):
    n = pl.program_id(1)
    kcat = jnp.concatenate([kp_ref[...], kc_ref[...], kn_ref[...]], axis=0)
    vcat = jnp.concatenate([vp_ref[...], vc_ref[...], vn_ref[...]], axis=0)
    qi = lax.broadcasted_iota(jnp.int32, (BLOCK, 3 * BLOCK), 0)
    sj = lax.broadcasted_iota(jnp.int32, (BLOCK, 3 * BLOCK), 1)
    dist = jnp.abs(qi - sj + BLOCK)
    key_pos = n * BLOCK - BLOCK + sj
    valid = (dist <= WINDOW) & (key_pos >= 0) & (key_pos < seq)
    distf = dist.astype(F32)
    group = N_HEADS // N_KV_HEADS
    for kh in range(N_KV_HEADS):
        k_h = kcat[:, kh * HEAD_DIM:(kh + 1) * HEAD_DIM]
        v_h = vcat[:, kh * HEAD_DIM:(kh + 1) * HEAD_DIM]
        for g in range(group):
            hd = kh * group + g
            q_h = q_ref[:, hd * HEAD_DIM:(hd + 1) * HEAD_DIM]
            s = lax.dot_general(q_h, k_h, (((1,), (1,)), ((), ())),
                                preferred_element_type=F32)
            s = jnp.where(valid, s - slopes[hd] * distf, NEG_INF)
            sink = sink_ref[hd]
            m = jnp.maximum(jnp.max(s, axis=-1, keepdims=True), sink)
            p = jnp.exp(s - m)
            denom = jnp.sum(p, axis=-1, keepdims=True) + jnp.exp(sink - m)
            probs = (p / denom).astype(BF16)
            o = jnp.dot(probs, v_h, preferred_element_type=F32)
            o_ref[:, hd * HEAD_DIM:(hd + 1) * HEAD_DIM] = o.astype(BF16)


def _attention(qkv, sink, batch, seq):
    n = qkv.shape[0]
    nb = seq // BLOCK
    dq = N_HEADS * HEAD_DIM
    dkv = N_KV_HEADS * HEAD_DIM
    k_col = dq // dkv
    v_col = k_col + 1
    slopes = tuple(float(2.0 ** (-8.0 * (h + 1) / N_HEADS)) for h in range(N_HEADS))

    def prev_blk(b, j):
        return b * nb + jnp.maximum(j - 1, 0)

    def next_blk(b, j):
        return b * nb + jnp.minimum(j + 1, nb - 1)

    kern = functools.partial(_attn_kernel, seq=seq, slopes=slopes)
    return pl.pallas_call(
        kern,
        grid=(batch, nb),
        in_specs=[
            pl.BlockSpec(memory_space=pltpu.SMEM),
            pl.BlockSpec((BLOCK, dq), lambda b, j: (b * nb + j, 0)),
            pl.BlockSpec((BLOCK, dkv), lambda b, j: (prev_blk(b, j), k_col)),
            pl.BlockSpec((BLOCK, dkv), lambda b, j: (b * nb + j, k_col)),
            pl.BlockSpec((BLOCK, dkv), lambda b, j: (next_blk(b, j), k_col)),
            pl.BlockSpec((BLOCK, dkv), lambda b, j: (prev_blk(b, j), v_col)),
            pl.BlockSpec((BLOCK, dkv), lambda b, j: (b * nb + j, v_col)),
            pl.BlockSpec((BLOCK, dkv), lambda b, j: (next_blk(b, j), v_col)),
        ],
        out_specs=pl.BlockSpec((BLOCK, dq), lambda b, j: (b * nb + j, 0)),
        out_shape=jax.ShapeDtypeStruct((n, dq), BF16),
        compiler_params=_params("parallel", "parallel"),
        name="window_attention",
    )(sink, qkv, qkv, qkv, qkv, qkv, qkv, qkv)


def _proj_res_kernel(a_ref, w_ref, r_ref, o_ref):
    o_ref[...] = r_ref[...] + jnp.dot(a_ref[...], w_ref[...], preferred_element_type=F32)


def _proj_residual(a, w, res, tm=512, name="proj_residual"):
    n, k = a.shape
    d = w.shape[1]
    return pl.pallas_call(
        _proj_res_kernel,
        grid=(n // tm,),
        in_specs=[
            pl.BlockSpec((tm, k), lambda i: (i, 0)),
            pl.BlockSpec((k, d), lambda i: (0, 0)),
            pl.BlockSpec((tm, d), lambda i: (i, 0)),
        ],
        out_specs=pl.BlockSpec((tm, d), lambda i: (i, 0)),
        out_shape=jax.ShapeDtypeStruct((n, d), F32),
        compiler_params=_params("parallel"),
        name=name,
    )(a, w, res)


def _dense_ffn_kernel(x_ref, g_ref, wg_ref, wu_ref, wd_ref, o_ref, h_ref, acc_ref):
    j = pl.program_id(1)

    @pl.when(j == 0)
    def _():
        h_ref[...] = _rms(x_ref[...], g_ref[...]).astype(BF16)

    h = h_ref[...]
    gt = jnp.dot(h, wg_ref[...], preferred_element_type=F32)
    up = jnp.dot(h, wu_ref[...], preferred_element_type=F32)
    act = (gt * jax.nn.sigmoid(gt) * up).astype(BF16)
    contrib = jnp.dot(act, wd_ref[...], preferred_element_type=F32)

    @pl.when(j == 0)
    def _():
        acc_ref[...] = contrib

    @pl.when(j > 0)
    def _():
        acc_ref[...] += contrib

    @pl.when(j == pl.num_programs(1) - 1)
    def _():
        o_ref[...] = x_ref[...] + acc_ref[...]


def _dense_ffn(x, gain, w_gate_up, w_down, tm=512, tf=512):
    n, d = x.shape
    f = w_down.shape[0]
    nf = f // tf
    return pl.pallas_call(
        _dense_ffn_kernel,
        grid=(n // tm, nf),
        in_specs=[
            pl.BlockSpec((tm, d), lambda i, j: (i, 0)),
            pl.BlockSpec((1, d), lambda i, j: (0, 0)),
            pl.BlockSpec((d, tf), lambda i, j: (0, j)),
            pl.BlockSpec((d, tf), lambda i, j: (0, nf + j)),
            pl.BlockSpec((tf, d), lambda i, j: (j, 0)),
        ],
        out_specs=pl.BlockSpec((tm, d), lambda i, j: (i, 0)),
        out_shape=jax.ShapeDtypeStruct((n, d), F32),
        scratch_shapes=[pltpu.VMEM((tm, d), BF16), pltpu.VMEM((tm, d), F32)],
        compiler_params=_params("parallel", "arbitrary"),
        name="dense_ffn",
    )(x, gain, w_gate_up, w_gate_up, w_down)


def _gmlp_in_kernel(x_ref, g_ref, w_ref, vg_ref, u_ref, v_ref, vf_ref, *, col_chunk):
    h = _rms(x_ref[...], g_ref[...]).astype(BF16)
    width = u_ref.shape[1]
    for c0 in range(0, width, col_chunk):
        z = jnp.dot(h, w_ref[:, c0:c0 + col_chunk], preferred_element_type=F32)
        u_ref[:, c0:c0 + col_chunk] = jax.nn.gelu(z).astype(BF16)
    ssq = jnp.zeros((x_ref.shape[0], 1), F32)
    for c0 in range(0, width, col_chunk):
        z = jnp.dot(h, w_ref[:, width + c0:width + c0 + col_chunk], preferred_element_type=F32)
        v = jax.nn.gelu(z)
        ssq = ssq + jnp.sum(v * v, axis=-1, keepdims=True)
        vf_ref[:, c0:c0 + col_chunk] = v
    inv = lax.rsqrt(ssq / width + EPS)
    v_ref[...] = (vf_ref[...] * inv * vg_ref[...]).astype(BF16)


def _gmlp_in(x, gain, w_in, v_gain, tm=256):
    n, d = x.shape
    width = w_in.shape[1] // 2
    kern = functools.partial(_gmlp_in_kernel, col_chunk=512)
    return pl.pallas_call(
        kern,
        grid=(n // tm,),
        in_specs=[
            pl.BlockSpec((tm, d), lambda i: (i, 0)),
            pl.BlockSpec((1, d), lambda i: (0, 0)),
            pl.BlockSpec((d, 2 * width), lambda i: (0, 0)),
            pl.BlockSpec((1, width), lambda i: (0, 0)),
        ],
        out_specs=[
            pl.BlockSpec((tm, width), lambda i: (i, 0)),
            pl.BlockSpec((tm, width), lambda i: (i, 0)),
        ],
        out_shape=[
            jax.ShapeDtypeStruct((n, width), BF16),
            jax.ShapeDtypeStruct((n, width), BF16),
        ],
        scratch_shapes=[pltpu.VMEM((tm, width), F32)],
        compiler_params=_params("parallel"),
        name="gmlp_in",
    )(x, gain, w_in, v_gain)


def _gmlp_out_kernel(u_ref, v_ref, ws_ref, bs_ref, wo_ref, r_ref, o_ref, y_ref):
    tm, width = u_ref.shape
    for c0 in range(0, tm, CHUNK):
        for g in range(width // LANES):
            cols = slice(g * LANES, (g + 1) * LANES)
            vv = v_ref[c0:c0 + CHUNK, cols]
            mixed = jnp.dot(ws_ref[g], vv, preferred_element_type=F32) + bs_ref[:, cols]
            y_ref[c0:c0 + CHUNK, cols] = (u_ref[c0:c0 + CHUNK, cols].astype(F32) * mixed).astype(BF16)
    o_ref[...] = r_ref[...] + jnp.dot(y_ref[...], wo_ref[...], preferred_element_type=F32)


def _gmlp_out(u, v, w_s, b_full, w_out, res, tm=512):
    n, width = u.shape
    d = w_out.shape[1]
    groups = w_s.shape[0]
    return pl.pallas_call(
        _gmlp_out_kernel,
        grid=(n // tm,),
        in_specs=[
            pl.BlockSpec((tm, width), lambda i: (i, 0)),
            pl.BlockSpec((tm, width), lambda i: (i, 0)),
            pl.BlockSpec((groups, CHUNK, CHUNK), lambda i: (0, 0, 0)),
            pl.BlockSpec((CHUNK, width), lambda i: (0, 0)),
            pl.BlockSpec((width, d), lambda i: (0, 0)),
            pl.BlockSpec((tm, d), lambda i: (i, 0)),
        ],
        out_specs=pl.BlockSpec((tm, d), lambda i: (i, 0)),
        out_shape=jax.ShapeDtypeStruct((n, d), F32),
        scratch_shapes=[pltpu.VMEM((tm, width), BF16)],
        compiler_params=_params("parallel"),
        name="gmlp_out",
    )(u, v, w_s, b_full, w_out, res)


def _router_kernel(x_ref, g_ref, wr_ref, h_ref, idx_ref, gate_ref):
    h = _rms(x_ref[...], g_ref[...])
    h_ref[...] = h
    logits = jnp.dot(h, wr_ref[...], preferred_element_type=F32, precision=lax.Precision.HIGHEST)
    lane = lax.broadcasted_iota(jnp.int32, logits.shape, 1)
    lg = jnp.where(lane < N_EXPERTS, logits, -jnp.inf)
    m1 = jnp.max(lg, axis=-1, keepdims=True)
    i1 = jnp.min(jnp.where(lg == m1, lane, LANES), axis=-1, keepdims=True)
    lg2 = jnp.where(lane == i1, -jnp.inf, lg)
    m2 = jnp.max(lg2, axis=-1, keepdims=True)
    i2 = jnp.min(jnp.where(lg2 == m2, lane, LANES), axis=-1, keepdims=True)
    e2 = jnp.exp(m2 - m1)
    w1 = 1.0 / (1.0 + e2)
    w2 = e2 / (1.0 + e2)
    idx_ref[...] = jnp.where(lane == 0, i1, jnp.where(lane == 1, i2, 0))
    gate_ref[...] = jnp.where(lane == 0, w1, jnp.where(lane == 1, w2, 0.0))


def _router(x, gain, w_router_padded, tm=512):
    n, d = x.shape
    return pl.pallas_call(
        _router_kernel,
        grid=(n // tm,),
        in_specs=[
            pl.BlockSpec((tm, d), lambda i: (i, 0)),
            pl.BlockSpec((1, d), lambda i: (0, 0)),
            pl.BlockSpec((d, LANES), lambda i: (0, 0)),
        ],
        out_specs=[
            pl.BlockSpec((tm, d), lambda i: (i, 0)),
            pl.BlockSpec((tm, LANES), lambda i: (i, 0)),
            pl.BlockSpec((tm, LANES), lambda i: (i, 0)),
        ],
        out_shape=[
            jax.ShapeDtypeStruct((n, d), F32),
            jax.ShapeDtypeStruct((n, LANES), jnp.int32),
            jax.ShapeDtypeStruct((n, LANES), F32),
        ],
        compiler_params=_params("parallel"),
        name="moe_router",
    )(x, gain, w_router_padded)


def _moe_ffn_kernel(te_ref, nt_ref, tok_ref, h_hbm, gate_ref, wg_ref, wu_ref, wd_ref, o_ref,
                    xbuf, hb_ref, acc_ref, sem):
    i = pl.program_id(0)
    j = pl.program_id(1)
    tm = xbuf.shape[0]
    active = i < nt_ref[0]

    @pl.when(active & (j == 0))
    def _():
        def issue(r, carry):
            tok = tok_ref[i * tm + r]
            pltpu.make_async_copy(h_hbm.at[pl.ds(tok, 1), :], xbuf.at[pl.ds(r, 1), :], sem).start()
            return carry
        lax.fori_loop(0, tm, issue, 0)
        pltpu.make_async_copy(h_hbm.at[pl.ds(0, tm), :], xbuf, sem).wait()
        hb_ref[...] = xbuf[...].astype(BF16)

    @pl.when(active)
    def _():
        h = hb_ref[...]
        gt = jnp.dot(h, wg_ref[...], preferred_element_type=F32)
        up = jnp.dot(h, wu_ref[...], preferred_element_type=F32)
        act = (gt * jax.nn.sigmoid(gt) * up).astype(BF16)
        contrib = jnp.dot(act, wd_ref[...], preferred_element_type=F32)

        @pl.when(j == 0)
        def _():
            acc_ref[...] = contrib

        @pl.when(j > 0)
        def _():
            acc_ref[...] += contrib

    last = j == pl.num_programs(1) - 1

    @pl.when(active & last)
    def _():
        o_ref[...] = acc_ref[...] * gate_ref[...]

    @pl.when(jnp.logical_not(active) & last)
    def _():
        o_ref[...] = jnp.zeros_like(o_ref)


def _moe_ffn(h, tile_expert, num_tiles, row_token, gate_sorted, we_gate, we_up, we_down, tm, tf=512):
    n, d = h.shape
    f = we_gate.shape[2]
    nf = f // tf
    n_tiles = tile_expert.shape[0]
    p = n_tiles * tm

    def w_in_map(i, j, te, nt, tok):
        return (te[i], 0, jnp.where(i < nt[0], j, nf - 1))

    def w_out_map(i, j, te, nt, tok):
        return (te[i], jnp.where(i < nt[0], j, nf - 1), 0)

    grid_spec = pltpu.PrefetchScalarGridSpec(
        num_scalar_prefetch=3,
        grid=(n_tiles, nf),
        in_specs=[
            pl.BlockSpec(memory_space=pl.ANY),
            pl.BlockSpec((tm, 1), lambda i, j, te, nt, tok: (i, 0)),
            pl.BlockSpec((None, d, tf), w_in_map),
            pl.BlockSpec((None, d, tf), w_in_map),
            pl.BlockSpec((None, tf, d), w_out_map),
        ],
        out_specs=pl.BlockSpec((tm, d), lambda i, j, te, nt, tok: (i, 0)),
        scratch_shapes=[
            pltpu.VMEM((tm, d), F32),
            pltpu.VMEM((tm, d), BF16),
            pltpu.VMEM((tm, d), F32),
            pltpu.SemaphoreType.DMA(()),
        ],
    )
    return pl.pallas_call(
        _moe_ffn_kernel,
        grid_spec=grid_spec,
        out_shape=jax.ShapeDtypeStruct((p, d), F32),
        compiler_params=_params("arbitrary", "arbitrary"),
        name="moe_ffn",
    )(tile_expert, num_tiles, row_token, h, gate_sorted, we_gate, we_up, we_down)


def _moe_combine_kernel(pos_ref, x_ref, y_hbm, o_ref, buf, sem):
    i = pl.program_id(0)
    tm = x_ref.shape[0]

    def issue(r, carry):
        for k in range(TOP_K):
            p = pos_ref[(i * tm + r) * TOP_K + k]
            pltpu.make_async_copy(y_hbm.at[pl.ds(p, 1), :], buf.at[k, pl.ds(r, 1), :], sem).start()
        return carry
    lax.fori_loop(0, tm, issue, 0)
    for k in range(TOP_K):
        pltpu.make_async_copy(y_hbm.at[pl.ds(0, tm), :], buf.at[k], sem).wait()
    o_ref[...] = x_ref[...] + (buf[0] + buf[1])


def _moe_combine(x, y_sorted, pos, tm=256):
    n, d = x.shape
    grid_spec = pltpu.PrefetchScalarGridSpec(
        num_scalar_prefetch=1,
        grid=(n // tm,),
        in_specs=[
            pl.BlockSpec((tm, d), lambda i, pos: (i, 0)),
            pl.BlockSpec(memory_space=pl.ANY),
        ],
        out_specs=pl.BlockSpec((tm, d), lambda i, pos: (i, 0)),
        scratch_shapes=[
            pltpu.VMEM((TOP_K, tm, d), F32),
            pltpu.SemaphoreType.DMA(()),
        ],
    )
    return pl.pallas_call(
        _moe_combine_kernel,
        grid_spec=grid_spec,
        out_shape=jax.ShapeDtypeStruct((n, d), F32),
        compiler_params=_params("arbitrary"),
        name="moe_combine",
    )(pos, x, y_sorted)


def _route_plan(top_idx, top_w, tm):
    n = top_idx.shape[0]
    n_pairs = n * TOP_K
    n_tiles = (n_pairs + N_EXPERTS * (tm - 1)) // tm
    e_flat = top_idx.reshape(n_pairs)
    onehot = (e_flat[:, None] == jnp.arange(N_EXPERTS, dtype=jnp.int32)[None, :]).astype(jnp.int32)
    csum = jnp.cumsum(onehot, axis=0)
    rank = jnp.sum(onehot * (csum - 1), axis=1)
    counts = csum[-1]
    tiles_per = (counts + tm - 1) // tm
    tile_end = jnp.cumsum(tiles_per)
    row_start = (tile_end - tiles_per) * tm
    pos = (row_start[e_flat] + rank).astype(jnp.int32)
    num_tiles = tile_end[-1:].astype(jnp.int32)
    tile_expert = jnp.searchsorted(tile_end, jnp.arange(n_tiles, dtype=jnp.int32), side="right")
    tile_expert = jnp.minimum(tile_expert, N_EXPERTS - 1).astype(jnp.int32)
    row_token = jnp.zeros((n_tiles * tm,), jnp.int32).at[pos].set(
        jnp.arange(n_pairs, dtype=jnp.int32) // TOP_K)
    gate_sorted = jnp.zeros((n_tiles * tm,), F32).at[pos].set(top_w.reshape(n_pairs))
    return tile_expert, num_tiles, row_token, gate_sorted[:, None], pos


def kernel(x, l0_mix_norm, l0_w_qkv, l0_q_norm, l0_k_norm, l0_sink, l0_w_o, l0_ffn_norm, l0_w_gate_up, l0_w_down, l1_mix_norm, l1_w_in, l1_v_norm, l1_w_s, l1_b_s, l1_w_out, l1_ffn_norm, l1_w_router, l1_we_gate, l1_we_up, l1_we_down):
    batch, seq, d = x.shape
    n = batch * seq
    x0 = x.reshape(n, d)
    dq = N_HEADS * HEAD_DIM
    dkv = N_KV_HEADS * HEAD_DIM

    head_gain = jnp.concatenate([
        jnp.tile(l0_q_norm * (HEAD_DIM ** -0.5), N_HEADS),
        jnp.tile(l0_k_norm, N_KV_HEADS),
        jnp.ones((dkv,), F32),
    ])[None, :]
    qkv = _qkv_proj(x0, l0_mix_norm[None, :], l0_w_qkv.astype(BF16), head_gain, dq + dkv)
    attn = _attention(qkv, l0_sink, batch, seq)
    x1 = _proj_residual(attn, l0_w_o.astype(BF16), x0, name="attn_out_proj")

    x2 = _dense_ffn(x1, l0_ffn_norm[None, :], l0_w_gate_up.astype(BF16), l0_w_down.astype(BF16))

    u, v = _gmlp_in(x2, l1_mix_norm[None, :], l1_w_in.astype(BF16), l1_v_norm[None, :])
    b_full = jnp.repeat(l1_b_s.T, LANES, axis=1)
    x3 = _gmlp_out(u, v, l1_w_s.astype(BF16), b_full, l1_w_out.astype(BF16), x2)

    w_router = jnp.pad(l1_w_router, ((0, 0), (0, LANES - N_EXPERTS)))
    h, ridx, rgate = _router(x3, l1_ffn_norm[None, :], w_router)
    tm = 512
    tile_expert, num_tiles, row_token, gate_sorted, pos = _route_plan(
        ridx[:, :TOP_K], rgate[:, :TOP_K], tm)
    y_sorted = _moe_ffn(h, tile_expert, num_tiles, row_token, gate_sorted,
                        l1_we_gate.astype(BF16), l1_we_up.astype(BF16), l1_we_down.astype(BF16), tm)
    out = _moe_combine(x3, y_sorted, pos)
    return out.reshape(batch, seq, d)
```

```python
import functools

import jax
import jax.numpy as jnp
from jax import lax
from jax.experimental import pallas as pl
from jax.experimental.pallas import tpu as pltpu

F32 = jnp.float32
BF16 = jnp.bfloat16

EPS = 1e-6
NEG_INF = -1e30
LANES = 128
HEAD_DIM = 128
N_HEADS = 16
N_KV_HEADS = 4
WINDOW = 128
BLOCK = 128
CHUNK = 128
N_EXPERTS = 8
TOP_K = 2

MIB = 1024 * 1024
VMEM_LIMIT = 56 * MIB


def _params(*sem):
    return pltpu.CompilerParams(dimension_semantics=sem, vmem_limit_bytes=VMEM_LIMIT)


def _rms(x, gain):
    ms = jnp.mean(x * x, axis=-1, keepdims=True)
    return x * lax.rsqrt(ms + EPS) * gain


def _qkv_kernel(x_ref, g_ref, w_ref, hg_ref, o_ref, *, n_norm_cols, col_chunk):
    h = _rms(x_ref[...], g_ref[...]).astype(BF16)
    n_out = o_ref.shape[1]
    for c0 in range(0, n_out, col_chunk):
        y = jnp.dot(h, w_ref[:, c0:c0 + col_chunk], preferred_element_type=F32)
        for h0 in range(0, col_chunk, HEAD_DIM):
            col = c0 + h0
            yh = y[:, h0:h0 + HEAD_DIM]
            if col < n_norm_cols:
                yh = _rms(yh, hg_ref[:, col:col + HEAD_DIM])
            o_ref[:, col:col + HEAD_DIM] = yh.astype(BF16)


def _qkv_proj(x, gain, w, head_gain, n_norm_cols, tm=512):
    n, d = x.shape
    nq = w.shape[1]
    kern = functools.partial(_qkv_kernel, n_norm_cols=n_norm_cols, col_chunk=512)
    return pl.pallas_call(
        kern,
        grid=(n // tm,),
        in_specs=[
            pl.BlockSpec((tm, d), lambda i: (i, 0)),
            pl.BlockSpec((1, d), lambda i: (0, 0)),
            pl.BlockSpec((d, nq), lambda i: (0, 0)),
            pl.BlockSpec((1, nq), lambda i: (0, 0)),
        ],
        out_specs=pl.BlockSpec((tm, nq), lambda i: (i, 0)),
        out_shape=jax.ShapeDtypeStruct((n, nq), BF16),
        compiler_params=_params("parallel"),
        name="qkv_proj",
    )(x, gain, w, head_gain)


def _attn_kernel(sink_ref, q_ref, kp_ref, kc_ref, kn_ref, vp_ref, vc_ref, vn_ref, o_ref,
                 *, seq, slopes):
    n = pl.program_id(1)
    kcat = jnp.concatenate([kp_ref[...], kc_ref[...], kn_ref[...]], axis=0)
    vcat = jnp.concatenate([vp_ref[...], vc_ref[...], vn_ref[...]], axis=0)
    qi = lax.broadcasted_iota(jnp.int32, (BLOCK, 3 * BLOCK), 0)
    sj = lax.broadcasted_iota(jnp.int32, (BLOCK, 3 * BLOCK), 1)
    dist = jnp.abs(qi - sj + BLOCK)
    key_pos = n * BLOCK - BLOCK + sj
    valid = (dist <= WINDOW) & (key_pos >= 0) & (key_pos < seq)
    distf = dist.astype(F32)
    group = N_HEADS // N_KV_HEADS
    for kh in range(N_KV_HEADS):
        k_h = kcat[:, kh * HEAD_DIM:(kh + 1) * HEAD_DIM]
        v_h = vcat[:, kh * HEAD_DIM:(kh + 1) * HEAD_DIM]
        for g in range(group):
            hd = kh * group + g
            q_h = q_ref[:, hd * HEAD_DIM:(hd + 1) * HEAD_DIM]
            s = lax.dot_general(q_h, k_h, (((1,), (1,)), ((), ())),
                                preferred_element_type=F32)
            s = jnp.where(valid, s - slopes[hd] * distf, NEG_INF)
            sink = sink_ref[hd]
            m = jnp.maximum(jnp.max(s, axis=-1, keepdims=True), sink)
            p = jnp.exp(s - m)
            denom = jnp.sum(p, axis=-1, keepdims=True) + jnp.exp(sink - m)
            probs = (p / denom).astype(BF16)
            o = jnp.dot(probs, v_h, preferred_element_type=F32)
            o_ref[:, hd * HEAD_DIM:(hd + 1) * HEAD_DIM] = o.astype(BF16)


def _attention(qkv, sink, batch, seq):
    n = qkv.shape[0]
    nb = seq // BLOCK
    dq = N_HEADS * HEAD_DIM
    dkv = N_KV_HEADS * HEAD_DIM
    k_col = dq // dkv
    v_col = k_col + 1
    slopes = tuple(float(2.0 ** (-8.0 * (h + 1) / N_HEADS)) for h in range(N_HEADS))

    def prev_blk(b, j):
        return b * nb + jnp.maximum(j - 1, 0)

    def next_blk(b, j):
        return b * nb + jnp.minimum(j + 1, nb - 1)

    kern = functools.partial(_attn_kernel, seq=seq, slopes=slopes)
    return pl.pallas_call(
        kern,
        grid=(batch, nb),
        in_specs=[
            pl.BlockSpec(memory_space=pltpu.SMEM),
            pl.BlockSpec((BLOCK, dq), lambda b, j: (b * nb + j, 0)),
            pl.BlockSpec((BLOCK, dkv), lambda b, j: (prev_blk(b, j), k_col)),
            pl.BlockSpec((BLOCK, dkv), lambda b, j: (b * nb + j, k_col)),
            pl.BlockSpec((BLOCK, dkv), lambda b, j: (next_blk(b, j), k_col)),
            pl.BlockSpec((BLOCK, dkv), lambda b, j: (prev_blk(b, j), v_col)),
            pl.BlockSpec((BLOCK, dkv), lambda b, j: (b * nb + j, v_col)),
            pl.BlockSpec((BLOCK, dkv), lambda b, j: (next_blk(b, j), v_col)),
        ],
        out_specs=pl.BlockSpec((BLOCK, dq), lambda b, j: (b * nb + j, 0)),
        out_shape=jax.ShapeDtypeStruct((n, dq), BF16),
        compiler_params=_params("parallel", "parallel"),
        name="window_attention",
    )(sink, qkv, qkv, qkv, qkv, qkv, qkv, qkv)


def _proj_res_kernel(a_ref, w_ref, r_ref, o_ref):
    o_ref[...] = r_ref[...] + jnp.dot(a_ref[...], w_ref[...], preferred_element_type=F32)


def _proj_residual(a, w, res, tm=512, name="proj_residual"):
    n, k = a.shape
    d = w.shape[1]
    return pl.pallas_call(
        _proj_res_kernel,
        grid=(n // tm,),
        in_specs=[
            pl.BlockSpec((tm, k), lambda i: (i, 0)),
            pl.BlockSpec((k, d), lambda i: (0, 0)),
            pl.BlockSpec((tm, d), lambda i: (i, 0)),
        ],
        out_specs=pl.BlockSpec((tm, d), lambda i: (i, 0)),
        out_shape=jax.ShapeDtypeStruct((n, d), F32),
        compiler_params=_params("parallel"),
        name=name,
    )(a, w, res)


def _dense_ffn_kernel(x_ref, g_ref, wg_ref, wu_ref, wd_ref, o_ref, h_ref):
    @pl.when(pl.program_id(1) == 0)
    def _():
        x = x_ref[...]
        h_ref[...] = _rms(x, g_ref[...]).astype(BF16)
        o_ref[...] = x

    h = h_ref[...]
    gt = jnp.dot(h, wg_ref[...], preferred_element_type=F32)
    up = jnp.dot(h, wu_ref[...], preferred_element_type=F32)
    act = (gt * jax.nn.sigmoid(gt) * up).astype(BF16)
    o_ref[...] += jnp.dot(act, wd_ref[...], preferred_element_type=F32)


def _dense_ffn(x, gain, w_gate_up, w_down, tm=512, tf=512):
    n, d = x.shape
    f = w_down.shape[0]
    nf = f // tf
    return pl.pallas_call(
        _dense_ffn_kernel,
        grid=(n // tm, nf),
        in_specs=[
            pl.BlockSpec((tm, d), lambda i, j: (i, 0)),
            pl.BlockSpec((1, d), lambda i, j: (0, 0)),
            pl.BlockSpec((d, tf), lambda i, j: (0, j)),
            pl.BlockSpec((d, tf), lambda i, j: (0, nf + j)),
            pl.BlockSpec((tf, d), lambda i, j: (j, 0)),
        ],
        out_specs=pl.BlockSpec((tm, d), lambda i, j: (i, 0)),
        out_shape=jax.ShapeDtypeStruct((n, d), F32),
        scratch_shapes=[pltpu.VMEM((tm, d), BF16)],
        compiler_params=_params("parallel", "arbitrary"),
        name="dense_ffn",
    )(x, gain, w_gate_up, w_gate_up, w_down)


def _gmlp_in_kernel(x_ref, g_ref, w_ref, vg_ref, u_ref, v_ref, vf_ref, *, col_chunk):
    h = _rms(x_ref[...], g_ref[...]).astype(BF16)
    width = u_ref.shape[1]
    for c0 in range(0, width, col_chunk):
        z = jnp.dot(h, w_ref[:, c0:c0 + col_chunk], preferred_element_type=F32)
        u_ref[:, c0:c0 + col_chunk] = jax.nn.gelu(z).astype(BF16)
    ssq = jnp.zeros((x_ref.shape[0], 1), F32)
    for c0 in range(0, width, col_chunk):
        z = jnp.dot(h, w_ref[:, width + c0:width + c0 + col_chunk], preferred_element_type=F32)
        v = jax.nn.gelu(z)
        ssq = ssq + jnp.sum(v * v, axis=-1, keepdims=True)
        vf_ref[:, c0:c0 + col_chunk] = v
    inv = lax.rsqrt(ssq / width + EPS)
    v_ref[...] = (vf_ref[...] * inv * vg_ref[...]).astype(BF16)


def _gmlp_in(x, gain, w_in, v_gain, tm=256):
    n, d = x.shape
    width = w_in.shape[1] // 2
    kern = functools.partial(_gmlp_in_kernel, col_chunk=512)
    return pl.pallas_call(
        kern,
        grid=(n // tm,),
        in_specs=[
            pl.BlockSpec((tm, d), lambda i: (i, 0)),
            pl.BlockSpec((1, d), lambda i: (0, 0)),
            pl.BlockSpec((d, 2 * width), lambda i: (0, 0)),
            pl.BlockSpec((1, width), lambda i: (0, 0)),
        ],
        out_specs=[
            pl.BlockSpec((tm, width), lambda i: (i, 0)),
            pl.BlockSpec((tm, width), lambda i: (i, 0)),
        ],
        out_shape=[
            jax.ShapeDtypeStruct((n, width), BF16),
            jax.ShapeDtypeStruct((n, width), BF16),
        ],
        scratch_shapes=[pltpu.VMEM((tm, width), F32)],
        compiler_params=_params("parallel"),
        name="gmlp_in",
    )(x, gain, w_in, v_gain)


def _gmlp_out_kernel(u_ref, v_ref, ws_ref, bs_ref, wo_ref, r_ref, o_ref, y_ref):
    tm, width = u_ref.shape
    for c0 in range(0, tm, CHUNK):
        for g in range(width // LANES):
            cols = slice(g * LANES, (g + 1) * LANES)
            vv = v_ref[c0:c0 + CHUNK, cols]
            mixed = jnp.dot(ws_ref[g], vv, preferred_element_type=F32) + bs_ref[:, cols]
            y_ref[c0:c0 + CHUNK, cols] = (u_ref[c0:c0 + CHUNK, cols].astype(F32) * mixed).astype(BF16)
    o_ref[...] = r_ref[...] + jnp.dot(y_ref[...], wo_ref[...], preferred_element_type=F32)


def _gmlp_out(u, v, w_s, b_full, w_out, res, tm=512):
    n, width = u.shape
    d = w_out.shape[1]
    groups = w_s.shape[0]
    return pl.pallas_call(
        _gmlp_out_kernel,
        grid=(n // tm,),
        in_specs=[
            pl.BlockSpec((tm, width), lambda i: (i, 0)),
            pl.BlockSpec((tm, width), lambda i: (i, 0)),
            pl.BlockSpec((groups, CHUNK, CHUNK), lambda i: (0, 0, 0)),
            pl.BlockSpec((CHUNK, width), lambda i: (0, 0)),
            pl.BlockSpec((width, d), lambda i: (0, 0)),
            pl.BlockSpec((tm, d), lambda i: (i, 0)),
        ],
        out_specs=pl.BlockSpec((tm, d), lambda i: (i, 0)),
        out_shape=jax.ShapeDtypeStruct((n, d), F32),
        scratch_shapes=[pltpu.VMEM((tm, width), BF16)],
        compiler_params=_params("parallel"),
        name="gmlp_out",
    )(u, v, w_s, b_full, w_out, res)


def _router_kernel(x_ref, g_ref, wr_ref, h_ref, idx_ref, gate_ref):
    h = _rms(x_ref[...], g_ref[...])
    h_ref[...] = h
    logits = jnp.dot(h, wr_ref[...], preferred_element_type=F32, precision=lax.Precision.HIGHEST)
    lane = lax.broadcasted_iota(jnp.int32, logits.shape, 1)
    lg = jnp.where(lane < N_EXPERTS, logits, -jnp.inf)
    m1 = jnp.max(lg, axis=-1, keepdims=True)
    i1 = jnp.min(jnp.where(lg == m1, lane, LANES), axis=-1, keepdims=True)
    lg2 = jnp.where(lane == i1, -jnp.inf, lg)
    m2 = jnp.max(lg2, axis=-1, keepdims=True)
    i2 = jnp.min(jnp.where(lg2 == m2, lane, LANES), axis=-1, keepdims=True)
    e2 = jnp.exp(m2 - m1)
    w1 = 1.0 / (1.0 + e2)
    w2 = e2 / (1.0 + e2)
    idx_ref[...] = jnp.where(lane == 0, i1, jnp.where(lane == 1, i2, 0))
    gate_ref[...] = jnp.where(lane == 0, w1, jnp.where(lane == 1, w2, 0.0))


def _router(x, gain, w_router_padded, tm=512):
    n, d = x.shape
    return pl.pallas_call(
        _router_kernel,
        grid=(n // tm,),
        in_specs=[
            pl.BlockSpec((tm, d), lambda i: (i, 0)),
            pl.BlockSpec((1, d), lambda i: (0, 0)),
            pl.BlockSpec((d, LANES), lambda i: (0, 0)),
        ],
        out_specs=[
            pl.BlockSpec((tm, d), lambda i: (i, 0)),
            pl.BlockSpec((tm, LANES), lambda i: (i, 0)),
            pl.BlockSpec((tm, LANES), lambda i: (i, 0)),
        ],
        out_shape=[
            jax.ShapeDtypeStruct((n, d), F32),
            jax.ShapeDtypeStruct((n, LANES), jnp.int32),
            jax.ShapeDtypeStruct((n, LANES), F32),
        ],
        compiler_params=_params("parallel"),
        name="moe_router",
    )(x, gain, w_router_padded)


def _moe_ffn_kernel(te_ref, nt_ref, tok_ref, h_hbm, gate_ref, wg_ref, wu_ref, wd_ref, o_ref,
                    xbuf, hb_ref, sem):
    i = pl.program_id(0)
    j = pl.program_id(1)
    tm = xbuf.shape[0]
    active = i < nt_ref[0]

    @pl.when(j == 0)
    def _():
        o_ref[...] = jnp.zeros_like(o_ref)

    @pl.when(active & (j == 0))
    def _():
        def issue(r, carry):
            tok = tok_ref[i * tm + r]
            pltpu.make_async_copy(h_hbm.at[pl.ds(tok, 1), :], xbuf.at[pl.ds(r, 1), :], sem).start()
            return carry
        lax.fori_loop(0, tm, issue, 0)
        pltpu.make_async_copy(h_hbm.at[pl.ds(0, tm), :], xbuf, sem).wait()
        hb_ref[...] = xbuf[...].astype(BF16)

    @pl.when(active)
    def _():
        h = hb_ref[...]
        gt = jnp.dot(h, wg_ref[...], preferred_element_type=F32)
        up = jnp.dot(h, wu_ref[...], preferred_element_type=F32)
        act = (gt * jax.nn.sigmoid(gt) * up * gate_ref[...]).astype(BF16)
        o_ref[...] += jnp.dot(act, wd_ref[...], preferred_element_type=F32)


def _moe_ffn(h, tile_expert, num_tiles, row_token, gate_sorted, we_gate, we_up, we_down, tm, tf=1024):
    n, d = h.shape
    f = we_gate.shape[2]
    nf = f // tf
    n_tiles = tile_expert.shape[0]
    p = n_tiles * tm

    def w_in_map(i, j, te, nt, tok):
        return (te[i], 0, jnp.where(i < nt[0], j, nf - 1))

    def w_out_map(i, j, te, nt, tok):
        return (te[i], jnp.where(i < nt[0], j, nf - 1), 0)

    grid_spec = pltpu.PrefetchScalarGridSpec(
        num_scalar_prefetch=3,
        grid=(n_tiles, nf),
        in_specs=[
            pl.BlockSpec(memory_space=pl.ANY),
            pl.BlockSpec((tm, 1), lambda i, j, te, nt, tok: (i, 0)),
            pl.BlockSpec((None, d, tf), w_in_map),
            pl.BlockSpec((None, d, tf), w_in_map),
            pl.BlockSpec((None, tf, d), w_out_map),
        ],
        out_specs=pl.BlockSpec((tm, d), lambda i, j, te, nt, tok: (i, 0)),
        scratch_shapes=[
            pltpu.VMEM((tm, d), F32),
            pltpu.VMEM((tm, d), BF16),
            pltpu.SemaphoreType.DMA(()),
        ],
    )
    return pl.pallas_call(
        _moe_ffn_kernel,
        grid_spec=grid_spec,
        out_shape=jax.ShapeDtypeStruct((p, d), F32),
        compiler_params=_params("arbitrary", "arbitrary"),
        name="moe_ffn",
    )(tile_expert, num_tiles, row_token, h, gate_sorted, we_gate, we_up, we_down)


def _moe_combine_kernel(pos_ref, x_ref, y_hbm, o_ref, buf, sem):
    i = pl.program_id(0)
    tm = x_ref.shape[0]

    def issue(r, carry):
        for k in range(TOP_K):
            p = pos_ref[(i * tm + r) * TOP_K + k]
            pltpu.make_async_copy(y_hbm.at[pl.ds(p, 1), :], buf.at[k, pl.ds(r, 1), :], sem).start()
        return carry
    lax.fori_loop(0, tm, issue, 0)
    for k in range(TOP_K):
        pltpu.make_async_copy(y_hbm.at[pl.ds(0, tm), :], buf.at[k], sem).wait()
    o_ref[...] = x_ref[...] + (buf[0] + buf[1])


def _moe_combine(x, y_sorted, pos, tm=256):
    n, d = x.shape
    grid_spec = pltpu.PrefetchScalarGridSpec(
        num_scalar_prefetch=1,
        grid=(n // tm,),
        in_specs=[
            pl.BlockSpec((tm, d), lambda i, pos: (i, 0)),
            pl.BlockSpec(memory_space=pl.ANY),
        ],
        out_specs=pl.BlockSpec((tm, d), lambda i, pos: (i, 0)),
        scratch_shapes=[
            pltpu.VMEM((TOP_K, tm, d), F32),
            pltpu.SemaphoreType.DMA(()),
        ],
    )
    return pl.pallas_call(
        _moe_combine_kernel,
        grid_spec=grid_spec,
        out_shape=jax.ShapeDtypeStruct((n, d), F32),
        compiler_params=_params("arbitrary"),
        name="moe_combine",
    )(pos, x, y_sorted)


def _route_plan(top_idx, top_w, tm):
    n = top_idx.shape[0]
    n_pairs = n * TOP_K
    n_tiles = (n_pairs + N_EXPERTS * (tm - 1)) // tm
    e_flat = top_idx.reshape(n_pairs)
    onehot = (e_flat[:, None] == jnp.arange(N_EXPERTS, dtype=jnp.int32)[None, :]).astype(jnp.int32)
    csum = jnp.cumsum(onehot, axis=0)
    rank = jnp.sum(onehot * (csum - 1), axis=1)
    counts = csum[-1]
    tiles_per = (counts + tm - 1) // tm
    tile_end = jnp.cumsum(tiles_per)
    row_start = (tile_end - tiles_per) * tm
    pos = (row_start[e_flat] + rank).astype(jnp.int32)
    num_tiles = tile_end[-1:].astype(jnp.int32)
    tile_ids = jnp.arange(n_tiles, dtype=jnp.int32)
    tile_expert = jnp.sum((tile_end[None, :] <= tile_ids[:, None]).astype(jnp.int32), axis=1)
    tile_expert = jnp.minimum(tile_expert, N_EXPERTS - 1).astype(jnp.int32)
    row_token = jnp.zeros((n_tiles * tm,), jnp.int32).at[pos].set(
        jnp.arange(n_pairs, dtype=jnp.int32) // TOP_K)
    gate_sorted = jnp.zeros((n_tiles * tm,), F32).at[pos].set(top_w.reshape(n_pairs))
    return tile_expert, num_tiles, row_token, gate_sorted[:, None], pos


def kernel(x, l0_mix_norm, l0_w_qkv, l0_q_norm, l0_k_norm, l0_sink, l0_w_o, l0_ffn_norm, l0_w_gate_up, l0_w_down, l1_mix_norm, l1_w_in, l1_v_norm, l1_w_s, l1_b_s, l1_w_out, l1_ffn_norm, l1_w_router, l1_we_gate, l1_we_up, l1_we_down):
    batch, seq, d = x.shape
    n = batch * seq
    x0 = x.reshape(n, d)
    dq = N_HEADS * HEAD_DIM
    dkv = N_KV_HEADS * HEAD_DIM

    head_gain = jnp.concatenate([
        jnp.tile(l0_q_norm * (HEAD_DIM ** -0.5), N_HEADS),
        jnp.tile(l0_k_norm, N_KV_HEADS),
        jnp.ones((dkv,), F32),
    ])[None, :]
    qkv = _qkv_proj(x0, l0_mix_norm[None, :], l0_w_qkv.astype(BF16), head_gain, dq + dkv)
    attn = _attention(qkv, l0_sink, batch, seq)
    x1 = _proj_residual(attn, l0_w_o.astype(BF16), x0, name="attn_out_proj")

    x2 = _dense_ffn(x1, l0_ffn_norm[None, :], l0_w_gate_up.astype(BF16), l0_w_down.astype(BF16))

    u, v = _gmlp_in(x2, l1_mix_norm[None, :], l1_w_in.astype(BF16), l1_v_norm[None, :])
    b_full = jnp.repeat(l1_b_s.T, LANES, axis=1)
    x3 = _gmlp_out(u, v, l1_w_s.astype(BF16), b_full, l1_w_out.astype(BF16), x2)

    w_router = jnp.pad(l1_w_router, ((0, 0), (0, LANES - N_EXPERTS)))
    h, ridx, rgate = _router(x3, l1_ffn_norm[None, :], w_router)
    tm = 512
    tile_expert, num_tiles, row_token, gate_sorted, pos = _route_plan(
        ridx[:, :TOP_K], rgate[:, :TOP_K], tm)
    y_sorted = _moe_ffn(h, tile_expert, num_tiles, row_token, gate_sorted,
                        l1_we_gate.astype(BF16), l1_we_up.astype(BF16), l1_we_down.astype(BF16), tm)
    out = _moe_combine(x3, y_sorted, pos)
    return out.reshape(batch, seq, d)
```

```python
import functools

import jax
import jax.numpy as jnp
from jax import lax
from jax.experimental import pallas as pl
from jax.experimental.pallas import tpu as pltpu

F32 = jnp.float32
BF16 = jnp.bfloat16

EPS = 1e-6
NEG_INF = -1e30
LANES = 128
HEAD_DIM = 128
N_HEADS = 16
N_KV_HEADS = 4
WINDOW = 128
BLOCK = 128
CHUNK = 128
N_EXPERTS = 8
TOP_K = 2

MIB = 1024 * 1024
VMEM_LIMIT = 56 * MIB


def _params(*sem):
    return pltpu.CompilerParams(dimension_semantics=sem, vmem_limit_bytes=VMEM_LIMIT)


def _rms(x, gain):
    ms = jnp.mean(x * x, axis=-1, keepdims=True)
    return x * lax.rsqrt(ms + EPS) * gain


def _qkv_kernel(x_ref, g_ref, w_ref, hg_ref, o_ref, *, n_norm_cols, col_chunk):
    h = _rms(x_ref[...], g_ref[...]).astype(BF16)
    n_out = o_ref.shape[1]
    for c0 in range(0, n_out, col_chunk):
        y = jnp.dot(h, w_ref[:, c0:c0 + col_chunk], preferred_element_type=F32)
        for h0 in range(0, col_chunk, HEAD_DIM):
            col = c0 + h0
            yh = y[:, h0:h0 + HEAD_DIM]
            if col < n_norm_cols:
                yh = _rms(yh, hg_ref[:, col:col + HEAD_DIM])
            o_ref[:, col:col + HEAD_DIM] = yh.astype(BF16)


def _qkv_proj(x, gain, w, head_gain, n_norm_cols, tm=512):
    n, d = x.shape
    nq = w.shape[1]
    kern = functools.partial(_qkv_kernel, n_norm_cols=n_norm_cols, col_chunk=512)
    return pl.pallas_call(
        kern,
        grid=(n // tm,),
        in_specs=[
            pl.BlockSpec((tm, d), lambda i: (i, 0)),
            pl.BlockSpec((1, d), lambda i: (0, 0)),
            pl.BlockSpec((d, nq), lambda i: (0, 0)),
            pl.BlockSpec((1, nq), lambda i: (0, 0)),
        ],
        out_specs=pl.BlockSpec((tm, nq), lambda i: (i, 0)),
        out_shape=jax.ShapeDtypeStruct((n, nq), BF16),
        compiler_params=_params("parallel"),
        name="qkv_proj",
    )(x, gain, w, head_gain)


def _attn_kernel(sink_ref, q_ref, kp_ref, kc_ref, kn_ref, vp_ref, vc_ref, vn_ref, bias_ref, o_ref):
    kcat = jnp.concatenate([kp_ref[...], kc_ref[...], kn_ref[...]], axis=0)
    vcat = jnp.concatenate([vp_ref[...], vc_ref[...], vn_ref[...]], axis=0)
    group = N_HEADS // N_KV_HEADS
    band = 3 * BLOCK
    for kh in range(N_KV_HEADS):
        k_h = kcat[:, kh * HEAD_DIM:(kh + 1) * HEAD_DIM]
        v_h = vcat[:, kh * HEAD_DIM:(kh + 1) * HEAD_DIM]
        heads = [kh * group + g for g in range(group)]
        q_g = jnp.concatenate([q_ref[:, hd * HEAD_DIM:(hd + 1) * HEAD_DIM] for hd in heads], axis=0)
        sink = jnp.concatenate([jnp.full((1, BLOCK), sink_ref[hd], F32) for hd in heads], axis=1)
        s = lax.dot_general(k_h, q_g, (((1,), (1,)), ((), ())), preferred_element_type=F32)
        s = s + bias_ref[kh * band:(kh + 1) * band, :]
        m = jnp.maximum(jnp.max(s, axis=0, keepdims=True), sink)
        p = jnp.exp(s - m)
        denom = jnp.sum(p, axis=0, keepdims=True) + jnp.exp(sink - m)
        o_t = lax.dot_general(v_h, p.astype(BF16), (((0,), (0,)), ((), ())),
                              preferred_element_type=F32) * (1.0 / denom)
        for g, hd in enumerate(heads):
            o_ref[:, hd * HEAD_DIM:(hd + 1) * HEAD_DIM] = o_t[:, g * BLOCK:(g + 1) * BLOCK].T.astype(BF16)


def _attention_bias():
    group = N_HEADS // N_KV_HEADS
    qi = jnp.arange(BLOCK)[:, None]
    sj = jnp.arange(3 * BLOCK)[None, :]
    dist = jnp.abs(qi - sj + BLOCK)
    slopes = jnp.exp2(-8.0 * jnp.arange(1, N_HEADS + 1, dtype=F32) / N_HEADS)
    bias = jnp.where(dist <= WINDOW, -slopes[:, None, None] * dist.astype(F32), NEG_INF)
    first = (sj >= BLOCK)[None]
    last = (sj < 2 * BLOCK)[None]
    edge = jnp.stack([jnp.where(first, bias, NEG_INF), bias, jnp.where(last, bias, NEG_INF)])
    edge = edge.reshape(3, N_KV_HEADS, group, BLOCK, 3 * BLOCK).transpose(0, 1, 4, 2, 3)
    return edge.reshape(3, N_KV_HEADS * 3 * BLOCK, group * BLOCK)


def _attention(qkv, sink, batch, seq):
    n = qkv.shape[0]
    nb = seq // BLOCK
    dq = N_HEADS * HEAD_DIM
    dkv = N_KV_HEADS * HEAD_DIM
    k_col = dq // dkv
    v_col = k_col + 1

    def prev_blk(b, j):
        return b * nb + jnp.maximum(j - 1, 0)

    def next_blk(b, j):
        return b * nb + jnp.minimum(j + 1, nb - 1)

    assert nb >= 2, "first and last query block must differ"
    group = N_HEADS // N_KV_HEADS

    def bias_variant(b, j):
        return (jnp.where(j == 0, 0, jnp.where(j == nb - 1, 2, 1)), 0, 0)

    return pl.pallas_call(
        _attn_kernel,
        grid=(batch, nb),
        in_specs=[
            pl.BlockSpec(memory_space=pltpu.SMEM),
            pl.BlockSpec((BLOCK, dq), lambda b, j: (b * nb + j, 0)),
            pl.BlockSpec((BLOCK, dkv), lambda b, j: (prev_blk(b, j), k_col)),
            pl.BlockSpec((BLOCK, dkv), lambda b, j: (b * nb + j, k_col)),
            pl.BlockSpec((BLOCK, dkv), lambda b, j: (next_blk(b, j), k_col)),
            pl.BlockSpec((BLOCK, dkv), lambda b, j: (prev_blk(b, j), v_col)),
            pl.BlockSpec((BLOCK, dkv), lambda b, j: (b * nb + j, v_col)),
            pl.BlockSpec((BLOCK, dkv), lambda b, j: (next_blk(b, j), v_col)),
            pl.BlockSpec((None, N_KV_HEADS * 3 * BLOCK, group * BLOCK), bias_variant),
        ],
        out_specs=pl.BlockSpec((BLOCK, dq), lambda b, j: (b * nb + j, 0)),
        out_shape=jax.ShapeDtypeStruct((n, dq), BF16),
        compiler_params=_params("parallel", "parallel"),
        name="window_attention",
    )(sink, qkv, qkv, qkv, qkv, qkv, qkv, qkv, _attention_bias())


def _proj_res_kernel(a_ref, w_ref, r_ref, o_ref):
    o_ref[...] = r_ref[...] + jnp.dot(a_ref[...], w_ref[...], preferred_element_type=F32)


def _proj_residual(a, w, res, tm=512, name="proj_residual"):
    n, k = a.shape
    d = w.shape[1]
    return pl.pallas_call(
        _proj_res_kernel,
        grid=(n // tm,),
        in_specs=[
            pl.BlockSpec((tm, k), lambda i: (i, 0)),
            pl.BlockSpec((k, d), lambda i: (0, 0)),
            pl.BlockSpec((tm, d), lambda i: (i, 0)),
        ],
        out_specs=pl.BlockSpec((tm, d), lambda i: (i, 0)),
        out_shape=jax.ShapeDtypeStruct((n, d), F32),
        compiler_params=_params("parallel"),
        name=name,
    )(a, w, res)


def _dense_ffn_kernel(x_ref, g_ref, wg_ref, wu_ref, wd_ref, o_ref, h_ref):
    @pl.when(pl.program_id(1) == 0)
    def _():
        x = x_ref[...]
        h_ref[...] = _rms(x, g_ref[...]).astype(BF16)
        o_ref[...] = x

    h = h_ref[...]
    gt = jnp.dot(h, wg_ref[...], preferred_element_type=F32)
    up = jnp.dot(h, wu_ref[...], preferred_element_type=F32)
    act = (gt * jax.nn.sigmoid(gt) * up).astype(BF16)
    o_ref[...] += jnp.dot(act, wd_ref[...], preferred_element_type=F32)


def _dense_ffn(x, gain, w_gate_up, w_down, tm=512, tf=512):
    n, d = x.shape
    f = w_down.shape[0]
    nf = f // tf
    return pl.pallas_call(
        _dense_ffn_kernel,
        grid=(n // tm, nf),
        in_specs=[
            pl.BlockSpec((tm, d), lambda i, j: (i, 0)),
            pl.BlockSpec((1, d), lambda i, j: (0, 0)),
            pl.BlockSpec((d, tf), lambda i, j: (0, j)),
            pl.BlockSpec((d, tf), lambda i, j: (0, nf + j)),
            pl.BlockSpec((tf, d), lambda i, j: (j, 0)),
        ],
        out_specs=pl.BlockSpec((tm, d), lambda i, j: (i, 0)),
        out_shape=jax.ShapeDtypeStruct((n, d), F32),
        scratch_shapes=[pltpu.VMEM((tm, d), BF16)],
        compiler_params=_params("parallel", "arbitrary"),
        name="dense_ffn",
    )(x, gain, w_gate_up, w_gate_up, w_down)


def _gmlp_in_kernel(x_ref, g_ref, w_ref, vg_ref, u_ref, v_ref, vf_ref, *, col_chunk):
    h = _rms(x_ref[...], g_ref[...]).astype(BF16)
    width = u_ref.shape[1]
    for c0 in range(0, width, col_chunk):
        z = jnp.dot(h, w_ref[:, c0:c0 + col_chunk], preferred_element_type=F32)
        u_ref[:, c0:c0 + col_chunk] = jax.nn.gelu(z).astype(BF16)
    ssq = jnp.zeros((x_ref.shape[0], 1), F32)
    for c0 in range(0, width, col_chunk):
        z = jnp.dot(h, w_ref[:, width + c0:width + c0 + col_chunk], preferred_element_type=F32)
        v = jax.nn.gelu(z)
        ssq = ssq + jnp.sum(v * v, axis=-1, keepdims=True)
        vf_ref[:, c0:c0 + col_chunk] = v
    inv = lax.rsqrt(ssq / width + EPS)
    v_ref[...] = (vf_ref[...] * inv * vg_ref[...]).astype(BF16)


def _gmlp_in(x, gain, w_in, v_gain, tm=256):
    n, d = x.shape
    width = w_in.shape[1] // 2
    kern = functools.partial(_gmlp_in_kernel, col_chunk=512)
    return pl.pallas_call(
        kern,
        grid=(n // tm,),
        in_specs=[
            pl.BlockSpec((tm, d), lambda i: (i, 0)),
            pl.BlockSpec((1, d), lambda i: (0, 0)),
            pl.BlockSpec((d, 2 * width), lambda i: (0, 0)),
            pl.BlockSpec((1, width), lambda i: (0, 0)),
        ],
        out_specs=[
            pl.BlockSpec((tm, width), lambda i: (i, 0)),
            pl.BlockSpec((tm, width), lambda i: (i, 0)),
        ],
        out_shape=[
            jax.ShapeDtypeStruct((n, width), BF16),
            jax.ShapeDtypeStruct((n, width), BF16),
        ],
        scratch_shapes=[pltpu.VMEM((tm, width), F32)],
        compiler_params=_params("parallel"),
        name="gmlp_in",
    )(x, gain, w_in, v_gain)


def _gmlp_out_kernel(u_ref, v_ref, ws_ref, bs_ref, wo_ref, r_ref, o_ref, y_ref):
    tm, width = u_ref.shape
    for c0 in range(0, tm, CHUNK):
        for g in range(width // LANES):
            cols = slice(g * LANES, (g + 1) * LANES)
            vv = v_ref[c0:c0 + CHUNK, cols]
            mixed = jnp.dot(ws_ref[g], vv, preferred_element_type=F32) + bs_ref[:, cols]
            y_ref[c0:c0 + CHUNK, cols] = (u_ref[c0:c0 + CHUNK, cols].astype(F32) * mixed).astype(BF16)
    o_ref[...] = r_ref[...] + jnp.dot(y_ref[...], wo_ref[...], preferred_element_type=F32)


def _gmlp_out(u, v, w_s, b_full, w_out, res, tm=512):
    n, width = u.shape
    d = w_out.shape[1]
    groups = w_s.shape[0]
    return pl.pallas_call(
        _gmlp_out_kernel,
        grid=(n // tm,),
        in_specs=[
            pl.BlockSpec((tm, width), lambda i: (i, 0)),
            pl.BlockSpec((tm, width), lambda i: (i, 0)),
            pl.BlockSpec((groups, CHUNK, CHUNK), lambda i: (0, 0, 0)),
            pl.BlockSpec((CHUNK, width), lambda i: (0, 0)),
            pl.BlockSpec((width, d), lambda i: (0, 0)),
            pl.BlockSpec((tm, d), lambda i: (i, 0)),
        ],
        out_specs=pl.BlockSpec((tm, d), lambda i: (i, 0)),
        out_shape=jax.ShapeDtypeStruct((n, d), F32),
        scratch_shapes=[pltpu.VMEM((tm, width), BF16)],
        compiler_params=_params("parallel"),
        name="gmlp_out",
    )(u, v, w_s, b_full, w_out, res)


def _router_kernel(x_ref, g_ref, wr_ref, h_ref, idx_ref, gate_ref):
    h = _rms(x_ref[...], g_ref[...])
    h_ref[...] = h
    logits = jnp.dot(h, wr_ref[...], preferred_element_type=F32, precision=lax.Precision.HIGHEST)
    lane = lax.broadcasted_iota(jnp.int32, logits.shape, 1)
    lg = jnp.where(lane < N_EXPERTS, logits, -jnp.inf)
    m1 = jnp.max(lg, axis=-1, keepdims=True)
    i1 = jnp.min(jnp.where(lg == m1, lane, LANES), axis=-1, keepdims=True)
    lg2 = jnp.where(lane == i1, -jnp.inf, lg)
    m2 = jnp.max(lg2, axis=-1, keepdims=True)
    i2 = jnp.min(jnp.where(lg2 == m2, lane, LANES), axis=-1, keepdims=True)
    e2 = jnp.exp(m2 - m1)
    w1 = 1.0 / (1.0 + e2)
    w2 = e2 / (1.0 + e2)
    idx_ref[...] = jnp.where(lane == 0, i1, jnp.where(lane == 1, i2, 0))
    gate_ref[...] = jnp.where(lane == 0, w1, jnp.where(lane == 1, w2, 0.0))


def _router(x, gain, w_router_padded, tm=512):
    n, d = x.shape
    return pl.pallas_call(
        _router_kernel,
        grid=(n // tm,),
        in_specs=[
            pl.BlockSpec((tm, d), lambda i: (i, 0)),
            pl.BlockSpec((1, d), lambda i: (0, 0)),
            pl.BlockSpec((d, LANES), lambda i: (0, 0)),
        ],
        out_specs=[
            pl.BlockSpec((tm, d), lambda i: (i, 0)),
            pl.BlockSpec((tm, LANES), lambda i: (i, 0)),
            pl.BlockSpec((tm, LANES), lambda i: (i, 0)),
        ],
        out_shape=[
            jax.ShapeDtypeStruct((n, d), F32),
            jax.ShapeDtypeStruct((n, LANES), jnp.int32),
            jax.ShapeDtypeStruct((n, LANES), F32),
        ],
        compiler_params=_params("parallel"),
        name="moe_router",
    )(x, gain, w_router_padded)


def _moe_ffn_kernel(te_ref, nt_ref, tok_ref, h_hbm, wg_ref, wu_ref, wd_ref, o_ref,
                    xbuf, hb_ref, sems, *, rows_per_step):
    i = pl.program_id(0)
    j = pl.program_id(1)
    n_tiles = pl.num_programs(0)
    nf = pl.num_programs(1)
    tm = hb_ref.shape[0]
    rows_buf = xbuf.shape[1]
    slot = i % 2

    def row_copy(tile, r, dst_slot):
        tok = tok_ref[tile * tm + r]
        return pltpu.make_async_copy(h_hbm.at[pl.ds(tok, 1), :], xbuf.at[dst_slot, pl.ds(r, 1), :],
                                     sems.at[dst_slot])

    def wait_slot(s):
        pltpu.make_async_copy(h_hbm.at[pl.ds(0, rows_buf), :], xbuf.at[s], sems.at[s]).wait()

    def prefetch_next_tile():
        for k in range(rows_per_step):
            row_copy(i + 1, j * rows_per_step + k, 1 - slot).start()

    @pl.when((i == 0) & (j == 0))
    def _():
        def issue(r, carry):
            row_copy(0, r, 0).start()
            return carry
        lax.fori_loop(0, rows_buf, issue, 0)

    @pl.when(j == 0)
    def _():
        wait_slot(slot)
        hb_ref[...] = xbuf[slot, 0:tm, :].astype(BF16)
        o_ref[...] = jnp.zeros_like(o_ref)

    active = i < nt_ref[0]

    @pl.when(active)
    def _():
        prefetch_next_tile()
        h = hb_ref[...]
        gt = jnp.dot(h, wg_ref[...], preferred_element_type=F32)
        up = jnp.dot(h, wu_ref[...], preferred_element_type=F32)
        act = (gt * jax.nn.sigmoid(gt) * up).astype(BF16)
        o_ref[...] += jnp.dot(act, wd_ref[...], preferred_element_type=F32)

    @pl.when(jnp.logical_not(active))
    def _():
        prefetch_next_tile()

    @pl.when((i == n_tiles - 1) & (j == nf - 1))
    def _():
        wait_slot(1 - slot)


def _moe_ffn(h, tile_expert, num_tiles, row_token, we_gate, we_up, we_down, tm, tf=1024):
    n, d = h.shape
    f = we_gate.shape[2]
    nf = f // tf
    n_tiles = tile_expert.shape[0]
    p = n_tiles * tm
    sublanes = 8
    rows_per_step = -(-tm // (nf * sublanes)) * sublanes
    rows_buf = rows_per_step * nf
    row_token = jnp.pad(row_token, (0, tm + rows_buf))

    def w_in_map(i, j, te, nt, tok):
        return (te[i], 0, jnp.where(i < nt[0], j, nf - 1))

    def w_out_map(i, j, te, nt, tok):
        return (te[i], jnp.where(i < nt[0], j, nf - 1), 0)

    grid_spec = pltpu.PrefetchScalarGridSpec(
        num_scalar_prefetch=3,
        grid=(n_tiles, nf),
        in_specs=[
            pl.BlockSpec(memory_space=pl.ANY),
            pl.BlockSpec((None, d, tf), w_in_map),
            pl.BlockSpec((None, d, tf), w_in_map),
            pl.BlockSpec((None, tf, d), w_out_map),
        ],
        out_specs=pl.BlockSpec((tm, d), lambda i, j, te, nt, tok: (i, 0)),
        scratch_shapes=[
            pltpu.VMEM((2, rows_buf, d), F32),
            pltpu.VMEM((tm, d), BF16),
            pltpu.SemaphoreType.DMA((2,)),
        ],
    )
    return pl.pallas_call(
        functools.partial(_moe_ffn_kernel, rows_per_step=rows_per_step),
        grid_spec=grid_spec,
        out_shape=jax.ShapeDtypeStruct((p, d), F32),
        compiler_params=_params("arbitrary", "arbitrary"),
        name="moe_ffn",
    )(tile_expert, num_tiles, row_token, h, we_gate, we_up, we_down)


def _moe_combine_kernel(pos_ref, x_ref, gate_ref, y_hbm, o_ref, buf, sem):
    i = pl.program_id(0)
    tm = x_ref.shape[0]

    def issue(r, carry):
        for k in range(TOP_K):
            p = pos_ref[(i * tm + r) * TOP_K + k]
            pltpu.make_async_copy(y_hbm.at[pl.ds(p, 1), :], buf.at[k, pl.ds(r, 1), :], sem).start()
        return carry
    lax.fori_loop(0, tm, issue, 0)
    for k in range(TOP_K):
        pltpu.make_async_copy(y_hbm.at[pl.ds(0, tm), :], buf.at[k], sem).wait()
    gate = gate_ref[...]
    o_ref[...] = x_ref[...] + (gate[:, 0:1] * buf[0] + gate[:, 1:2] * buf[1])


def _moe_combine(x, gate, y_sorted, pos, tm=256):
    n, d = x.shape
    grid_spec = pltpu.PrefetchScalarGridSpec(
        num_scalar_prefetch=1,
        grid=(n // tm,),
        in_specs=[
            pl.BlockSpec((tm, d), lambda i, pos: (i, 0)),
            pl.BlockSpec((tm, LANES), lambda i, pos: (i, 0)),
            pl.BlockSpec(memory_space=pl.ANY),
        ],
        out_specs=pl.BlockSpec((tm, d), lambda i, pos: (i, 0)),
        scratch_shapes=[
            pltpu.VMEM((TOP_K, tm, d), F32),
            pltpu.SemaphoreType.DMA(()),
        ],
    )
    return pl.pallas_call(
        _moe_combine_kernel,
        grid_spec=grid_spec,
        out_shape=jax.ShapeDtypeStruct((n, d), F32),
        compiler_params=_params("arbitrary"),
        name="moe_combine",
    )(pos, x, gate, y_sorted)


def _route_plan(top_idx, tm):
    n = top_idx.shape[0]
    n_pairs = n * TOP_K
    n_tiles = (n_pairs + N_EXPERTS * (tm - 1)) // tm
    e_flat = top_idx.reshape(n_pairs)
    onehot = (e_flat[:, None] == jnp.arange(N_EXPERTS, dtype=jnp.int32)[None, :]).astype(jnp.int32)
    csum = jnp.cumsum(onehot, axis=0)
    rank = jnp.sum(onehot * (csum - 1), axis=1)
    counts = csum[-1]
    tiles_per = (counts + tm - 1) // tm
    tile_end = jnp.cumsum(tiles_per)
    row_start = (tile_end - tiles_per) * tm
    pos = (row_start[e_flat] + rank).astype(jnp.int32)
    num_tiles = tile_end[-1:].astype(jnp.int32)
    tile_ids = jnp.arange(n_tiles, dtype=jnp.int32)
    tile_expert = jnp.sum((tile_end[None, :] <= tile_ids[:, None]).astype(jnp.int32), axis=1)
    tile_expert = jnp.minimum(tile_expert, N_EXPERTS - 1).astype(jnp.int32)
    row_token = jnp.zeros((n_tiles * tm,), jnp.int32).at[pos].set(
        jnp.arange(n_pairs, dtype=jnp.int32) // TOP_K)
    return tile_expert, num_tiles, row_token, pos


def kernel(x, l0_mix_norm, l0_w_qkv, l0_q_norm, l0_k_norm, l0_sink, l0_w_o, l0_ffn_norm, l0_w_gate_up, l0_w_down, l1_mix_norm, l1_w_in, l1_v_norm, l1_w_s, l1_b_s, l1_w_out, l1_ffn_norm, l1_w_router, l1_we_gate, l1_we_up, l1_we_down):
    batch, seq, d = x.shape
    n = batch * seq
    x0 = x.reshape(n, d)
    dq = N_HEADS * HEAD_DIM
    dkv = N_KV_HEADS * HEAD_DIM

    head_gain = jnp.concatenate([
        jnp.tile(l0_q_norm * (HEAD_DIM ** -0.5), N_HEADS),
        jnp.tile(l0_k_norm, N_KV_HEADS),
        jnp.ones((dkv,), F32),
    ])[None, :]
    qkv = _qkv_proj(x0, l0_mix_norm[None, :], l0_w_qkv.astype(BF16), head_gain, dq + dkv)
    attn = _attention(qkv, l0_sink, batch, seq)
    x1 = _proj_residual(attn, l0_w_o.astype(BF16), x0, name="attn_out_proj")

    x2 = _dense_ffn(x1, l0_ffn_norm[None, :], l0_w_gate_up.astype(BF16), l0_w_down.astype(BF16))

    u, v = _gmlp_in(x2, l1_mix_norm[None, :], l1_w_in.astype(BF16), l1_v_norm[None, :])
    b_full = jnp.repeat(l1_b_s.T, LANES, axis=1)
    x3 = _gmlp_out(u, v, l1_w_s.astype(BF16), b_full, l1_w_out.astype(BF16), x2)

    w_router = jnp.pad(l1_w_router, ((0, 0), (0, LANES - N_EXPERTS)))
    h, ridx, rgate = _router(x3, l1_ffn_norm[None, :], w_router)
    tm = 512
    tile_expert, num_tiles, row_token, pos = _route_plan(ridx[:, :TOP_K], tm)
    y_sorted = _moe_ffn(h, tile_expert, num_tiles, row_token,
                        l1_we_gate.astype(BF16), l1_we_up.astype(BF16), l1_we_down.astype(BF16), tm)
    out = _moe_combine(x3, rgate, y_sorted, pos)
    return out.reshape(batch, seq, d)
```

```python
import functools

import jax
import jax.numpy as jnp
from jax import lax
from jax.experimental import pallas as pl
from jax.experimental.pallas import tpu as pltpu

F32 = jnp.float32
BF16 = jnp.bfloat16

EPS = 1e-6
NEG_INF = -1e30
LANES = 128
SUBLANES = 8
HEAD_DIM = 128
N_HEADS = 16
N_KV_HEADS = 4
WINDOW = 128
BLOCK = 128
CHUNK = 128
N_EXPERTS = 8
TOP_K = 2

MIB = 1024 * 1024
VMEM_LIMIT = 56 * MIB


def _params(*sem):
    return pltpu.CompilerParams(dimension_semantics=sem, vmem_limit_bytes=VMEM_LIMIT)


def _resident(shape):
    zeros = (0,) * len(shape)
    return pl.BlockSpec(shape, lambda *_: zeros, pipeline_mode=pl.Buffered(1))


def _rms(x, gain):
    ms = jnp.mean(x * x, axis=-1, keepdims=True)
    return x * lax.rsqrt(ms + EPS) * gain


def _qkv_kernel(x_ref, g_ref, w_ref, hg_ref, o_ref, *, n_norm_cols, col_chunk):
    h = _rms(x_ref[...], g_ref[...]).astype(BF16)
    n_out = o_ref.shape[1]
    for c0 in range(0, n_out, col_chunk):
        y = jnp.dot(h, w_ref[:, c0:c0 + col_chunk], preferred_element_type=F32)
        for h0 in range(0, col_chunk, HEAD_DIM):
            col = c0 + h0
            yh = y[:, h0:h0 + HEAD_DIM]
            if col < n_norm_cols:
                yh = _rms(yh, hg_ref[:, col:col + HEAD_DIM])
            o_ref[:, col:col + HEAD_DIM] = yh.astype(BF16)


def _qkv_proj(x, gain, w, head_gain, n_norm_cols, tm=512):
    n, d = x.shape
    nq = w.shape[1]
    kern = functools.partial(_qkv_kernel, n_norm_cols=n_norm_cols, col_chunk=512)
    return pl.pallas_call(
        kern,
        grid=(n // tm,),
        in_specs=[
            pl.BlockSpec((tm, d), lambda i: (i, 0)),
            _resident((1, d)),
            _resident((d, nq)),
            _resident((1, nq)),
        ],
        out_specs=pl.BlockSpec((tm, nq), lambda i: (i, 0)),
        out_shape=jax.ShapeDtypeStruct((n, nq), BF16),
        compiler_params=_params("parallel"),
        name="qkv_proj",
    )(x, gain, w, head_gain)


def _attn_kernel(sink_ref, q_ref, kp_ref, kc_ref, kn_ref, vp_ref, vc_ref, vn_ref, bias_ref, o_ref):
    kcat = jnp.concatenate([kp_ref[...], kc_ref[...], kn_ref[...]], axis=0)
    vcat = jnp.concatenate([vp_ref[...], vc_ref[...], vn_ref[...]], axis=0)
    group = N_HEADS // N_KV_HEADS
    band = 3 * BLOCK
    scores = []
    for kh in range(N_KV_HEADS):
        k_h = kcat[:, kh * HEAD_DIM:(kh + 1) * HEAD_DIM]
        heads = [kh * group + g for g in range(group)]
        q_g = jnp.concatenate([q_ref[:, hd * HEAD_DIM:(hd + 1) * HEAD_DIM] for hd in heads], axis=0)
        scores.append(lax.dot_general(k_h, q_g, (((1,), (1,)), ((), ())), preferred_element_type=F32))
    for kh in range(N_KV_HEADS):
        v_h = vcat[:, kh * HEAD_DIM:(kh + 1) * HEAD_DIM]
        heads = [kh * group + g for g in range(group)]
        sink = jnp.concatenate([jnp.full((1, BLOCK), sink_ref[hd], F32) for hd in heads], axis=1)
        s = scores[kh] + bias_ref[kh * band:(kh + 1) * band, :]
        m = jnp.maximum(jnp.max(s, axis=0, keepdims=True), sink)
        p = jnp.exp(s - m)
        denom = jnp.sum(p, axis=0, keepdims=True) + jnp.exp(sink - m)
        o_t = lax.dot_general(v_h, p.astype(BF16), (((0,), (0,)), ((), ())),
                              preferred_element_type=F32) * (1.0 / denom)
        for g, hd in enumerate(heads):
            o_ref[:, hd * HEAD_DIM:(hd + 1) * HEAD_DIM] = o_t[:, g * BLOCK:(g + 1) * BLOCK].T.astype(BF16)


def _attention_bias():
    group = N_HEADS // N_KV_HEADS
    qi = jnp.arange(BLOCK)[:, None]
    sj = jnp.arange(3 * BLOCK)[None, :]
    dist = jnp.abs(qi - sj + BLOCK)
    slopes = jnp.exp2(-8.0 * jnp.arange(1, N_HEADS + 1, dtype=F32) / N_HEADS)
    bias = jnp.where(dist <= WINDOW, -slopes[:, None, None] * dist.astype(F32), NEG_INF)
    first = (sj >= BLOCK)[None]
    last = (sj < 2 * BLOCK)[None]
    edge = jnp.stack([jnp.where(first, bias, NEG_INF), bias, jnp.where(last, bias, NEG_INF)])
    edge = edge.reshape(3, N_KV_HEADS, group, BLOCK, 3 * BLOCK).transpose(0, 1, 4, 2, 3)
    return edge.reshape(3, N_KV_HEADS * 3 * BLOCK, group * BLOCK)


def _attention(qkv, sink, batch, seq):
    n = qkv.shape[0]
    nb = seq // BLOCK
    dq = N_HEADS * HEAD_DIM
    dkv = N_KV_HEADS * HEAD_DIM
    k_col = dq // dkv
    v_col = k_col + 1

    def prev_blk(b, j):
        return b * nb + jnp.maximum(j - 1, 0)

    def next_blk(b, j):
        return b * nb + jnp.minimum(j + 1, nb - 1)

    assert nb >= 2, "first and last query block must differ"
    group = N_HEADS // N_KV_HEADS

    def bias_variant(b, j):
        return (jnp.where(j == 0, 0, jnp.where(j == nb - 1, 2, 1)), 0, 0)

    return pl.pallas_call(
        _attn_kernel,
        grid=(batch, nb),
        in_specs=[
            pl.BlockSpec(memory_space=pltpu.SMEM),
            pl.BlockSpec((BLOCK, dq), lambda b, j: (b * nb + j, 0)),
            pl.BlockSpec((BLOCK, dkv), lambda b, j: (prev_blk(b, j), k_col)),
            pl.BlockSpec((BLOCK, dkv), lambda b, j: (b * nb + j, k_col)),
            pl.BlockSpec((BLOCK, dkv), lambda b, j: (next_blk(b, j), k_col)),
            pl.BlockSpec((BLOCK, dkv), lambda b, j: (prev_blk(b, j), v_col)),
            pl.BlockSpec((BLOCK, dkv), lambda b, j: (b * nb + j, v_col)),
            pl.BlockSpec((BLOCK, dkv), lambda b, j: (next_blk(b, j), v_col)),
            pl.BlockSpec((None, N_KV_HEADS * 3 * BLOCK, group * BLOCK), bias_variant),
        ],
        out_specs=pl.BlockSpec((BLOCK, dq), lambda b, j: (b * nb + j, 0)),
        out_shape=jax.ShapeDtypeStruct((n, dq), BF16),
        compiler_params=_params("parallel", "parallel"),
        name="window_attention",
    )(sink, qkv, qkv, qkv, qkv, qkv, qkv, qkv, _attention_bias())


def _proj_res_kernel(a_ref, w_ref, r_ref, o_ref):
    o_ref[...] = r_ref[...] + jnp.dot(a_ref[...], w_ref[...], preferred_element_type=F32)


def _proj_residual(a, w, res, tm=512, name="proj_residual"):
    n, k = a.shape
    d = w.shape[1]
    return pl.pallas_call(
        _proj_res_kernel,
        grid=(n // tm,),
        in_specs=[
            pl.BlockSpec((tm, k), lambda i: (i, 0)),
            _resident((k, d)),
            pl.BlockSpec((tm, d), lambda i: (i, 0)),
        ],
        out_specs=pl.BlockSpec((tm, d), lambda i: (i, 0)),
        out_shape=jax.ShapeDtypeStruct((n, d), F32),
        compiler_params=_params("parallel"),
        name=name,
    )(a, w, res)


def _dense_ffn_kernel(x_ref, g_ref, wg_ref, wu_ref, wd_ref, o_ref, h_ref):
    @pl.when(pl.program_id(1) == 0)
    def _():
        x = x_ref[...]
        h_ref[...] = _rms(x, g_ref[...]).astype(BF16)
        o_ref[...] = x

    h = h_ref[...]
    gt = jnp.dot(h, wg_ref[...], preferred_element_type=F32)
    up = jnp.dot(h, wu_ref[...], preferred_element_type=F32)
    act = (gt * jax.nn.sigmoid(gt) * up).astype(BF16)
    o_ref[...] += jnp.dot(act, wd_ref[...], preferred_element_type=F32)


def _dense_ffn(x, gain, w_gate_up, w_down, tm=512, tf=512):
    n, d = x.shape
    f = w_down.shape[0]
    nf = f // tf
    return pl.pallas_call(
        _dense_ffn_kernel,
        grid=(n // tm, nf),
        in_specs=[
            pl.BlockSpec((tm, d), lambda i, j: (i, 0)),
            _resident((1, d)),
            pl.BlockSpec((d, tf), lambda i, j: (0, j)),
            pl.BlockSpec((d, tf), lambda i, j: (0, nf + j)),
            pl.BlockSpec((tf, d), lambda i, j: (j, 0)),
        ],
        out_specs=pl.BlockSpec((tm, d), lambda i, j: (i, 0)),
        out_shape=jax.ShapeDtypeStruct((n, d), F32),
        scratch_shapes=[pltpu.VMEM((tm, d), BF16)],
        compiler_params=_params("parallel", "arbitrary"),
        name="dense_ffn",
    )(x, gain, w_gate_up, w_gate_up, w_down)


def _gmlp_in_kernel(x_ref, g_ref, w_ref, vg_ref, u_ref, v_ref, vf_ref, *, col_chunk):
    h = _rms(x_ref[...], g_ref[...]).astype(BF16)
    width = u_ref.shape[1]
    for c0 in range(0, width, col_chunk):
        z = jnp.dot(h, w_ref[:, c0:c0 + col_chunk], preferred_element_type=F32)
        u_ref[:, c0:c0 + col_chunk] = jax.nn.gelu(z).astype(BF16)
    ssq = jnp.zeros((x_ref.shape[0], 1), F32)
    for c0 in range(0, width, col_chunk):
        z = jnp.dot(h, w_ref[:, width + c0:width + c0 + col_chunk], preferred_element_type=F32)
        v = jax.nn.gelu(z)
        ssq = ssq + jnp.sum(v * v, axis=-1, keepdims=True)
        vf_ref[:, c0:c0 + col_chunk] = v
    inv = lax.rsqrt(ssq / width + EPS)
    v_ref[...] = (vf_ref[...] * inv * vg_ref[...]).astype(BF16)


def _gmlp_in(x, gain, w_in, v_gain, tm=512):
    n, d = x.shape
    width = w_in.shape[1] // 2
    kern = functools.partial(_gmlp_in_kernel, col_chunk=512)
    return pl.pallas_call(
        kern,
        grid=(n // tm,),
        in_specs=[
            pl.BlockSpec((tm, d), lambda i: (i, 0)),
            _resident((1, d)),
            _resident((d, 2 * width)),
            _resident((1, width)),
        ],
        out_specs=[
            pl.BlockSpec((tm, width), lambda i: (i, 0)),
            pl.BlockSpec((tm, width), lambda i: (i, 0)),
        ],
        out_shape=[
            jax.ShapeDtypeStruct((n, width), BF16),
            jax.ShapeDtypeStruct((n, width), BF16),
        ],
        scratch_shapes=[pltpu.VMEM((tm, width), F32)],
        compiler_params=_params("parallel"),
        name="gmlp_in",
    )(x, gain, w_in, v_gain)


def _top2_route(logits):
    lane = lax.broadcasted_iota(jnp.int32, logits.shape, 1)
    lg = jnp.where(lane < N_EXPERTS, logits, -jnp.inf)
    m1 = jnp.max(lg, axis=-1, keepdims=True)
    i1 = jnp.min(jnp.where(lg == m1, lane, LANES), axis=-1, keepdims=True)
    lg2 = jnp.where(lane == i1, -jnp.inf, lg)
    m2 = jnp.max(lg2, axis=-1, keepdims=True)
    i2 = jnp.min(jnp.where(lg2 == m2, lane, LANES), axis=-1, keepdims=True)
    e2 = jnp.exp(m2 - m1)
    w1 = 1.0 / (1.0 + e2)
    w2 = e2 / (1.0 + e2)
    idx = jnp.where(lane == 0, i1, jnp.where(lane == 1, i2, 0))
    gate = jnp.where(lane == 0, w1, jnp.where(lane == 1, w2, 0.0))
    return idx, gate


def _gmlp_out_router_kernel(u_ref, v_ref, ws_ref, bs_ref, wo_ref, r_ref, g_ref, wr_ref,
                            x_ref, h_ref, idx_ref, gate_ref, y_ref):
    tm, width = u_ref.shape
    for c0 in range(0, tm, CHUNK):
        for g in range(width // LANES):
            cols = slice(g * LANES, (g + 1) * LANES)
            vv = v_ref[c0:c0 + CHUNK, cols]
            mixed = jnp.dot(ws_ref[g], vv, preferred_element_type=F32) + bs_ref[:, cols]
            y_ref[c0:c0 + CHUNK, cols] = (u_ref[c0:c0 + CHUNK, cols].astype(F32) * mixed).astype(BF16)
    x = r_ref[...] + jnp.dot(y_ref[...], wo_ref[...], preferred_element_type=F32)
    x_ref[...] = x
    h = _rms(x, g_ref[...])
    h_ref[...] = h
    h_hi = h.astype(BF16)
    h_lo = (h - h_hi.astype(F32)).astype(BF16)
    a = jnp.dot(h_hi, wr_ref[...], preferred_element_type=F32)
    b = jnp.dot(h_lo, wr_ref[:, 0:LANES], preferred_element_type=F32)
    logits = a[:, 0:LANES] + (a[:, LANES:2 * LANES] + b)
    idx_ref[...], gate_ref[...] = _top2_route(logits)


def _gmlp_out_router(u, v, w_s, b_full, w_out, res, ffn_gain, w_router_split, tm=512):
    n, width = u.shape
    d = w_out.shape[1]
    groups = w_s.shape[0]
    row = lambda i: (i, 0)
    return pl.pallas_call(
        _gmlp_out_router_kernel,
        grid=(n // tm,),
        in_specs=[
            pl.BlockSpec((tm, width), row),
            pl.BlockSpec((tm, width), row),
            _resident((groups, CHUNK, CHUNK)),
            _resident((CHUNK, width)),
            _resident((width, d)),
            pl.BlockSpec((tm, d), row),
            _resident((1, d)),
            _resident((d, 2 * LANES)),
        ],
        out_specs=[
            pl.BlockSpec((tm, d), row),
            pl.BlockSpec((tm, d), row),
            pl.BlockSpec((tm, LANES), row),
            pl.BlockSpec((tm, LANES), row),
        ],
        out_shape=[
            jax.ShapeDtypeStruct((n, d), F32),
            jax.ShapeDtypeStruct((n, d), F32),
            jax.ShapeDtypeStruct((n, LANES), jnp.int32),
            jax.ShapeDtypeStruct((n, LANES), F32),
        ],
        scratch_shapes=[pltpu.VMEM((tm, width), BF16)],
        compiler_params=_params("parallel"),
        name="gmlp_out_router",
    )(u, v, w_s, b_full, w_out, res, ffn_gain, w_router_split)


def _moe_ffn_kernel(te_ref, nt_ref, tok_ref, h_hbm, wg_ref, wu_ref, wd_ref, o_ref,
                    xbuf, hb_ref, sems, *, rows_per_step):
    i = pl.program_id(0)
    j = pl.program_id(1)
    n_tiles = pl.num_programs(0)
    nf = pl.num_programs(1)
    tm = hb_ref.shape[0]
    rows_buf = xbuf.shape[1]
    slot = i % 2

    def row_copy(tile, r, dst_slot):
        tok = tok_ref[tile * tm + r]
        return pltpu.make_async_copy(h_hbm.at[pl.ds(tok, 1), :], xbuf.at[dst_slot, pl.ds(r, 1), :],
                                     sems.at[dst_slot])

    def wait_slot(s):
        pltpu.make_async_copy(h_hbm.at[pl.ds(0, rows_buf), :], xbuf.at[s], sems.at[s]).wait()

    def prefetch_next_tile():
        for k in range(rows_per_step):
            row_copy(i + 1, j * rows_per_step + k, 1 - slot).start()

    @pl.when((i == 0) & (j == 0))
    def _():
        def issue(r, carry):
            row_copy(0, r, 0).start()
            return carry
        lax.fori_loop(0, rows_buf, issue, 0)

    @pl.when(j == 0)
    def _():
        wait_slot(slot)
        hb_ref[...] = xbuf[slot, 0:tm, :].astype(BF16)
        o_ref[...] = jnp.zeros_like(o_ref)

    active = i < nt_ref[0]

    @pl.when(active)
    def _():
        prefetch_next_tile()
        h = hb_ref[...]
        gt = jnp.dot(h, wg_ref[...], preferred_element_type=F32)
        up = jnp.dot(h, wu_ref[...], preferred_element_type=F32)
        act = (gt * jax.nn.sigmoid(gt) * up).astype(BF16)
        o_ref[...] += jnp.dot(act, wd_ref[...], preferred_element_type=F32)

    @pl.when(jnp.logical_not(active))
    def _():
        prefetch_next_tile()

    @pl.when((i == n_tiles - 1) & (j == nf - 1))
    def _():
        wait_slot(1 - slot)


def _moe_ffn(h, tile_expert, num_tiles, row_token, we_gate, we_up, we_down, tm, tf=1024):
    n, d = h.shape
    f = we_gate.shape[2]
    nf = f // tf
    n_tiles = tile_expert.shape[0]
    p = n_tiles * tm
    rows_per_step = -(-tm // (nf * SUBLANES)) * SUBLANES
    rows_buf = rows_per_step * nf
    row_token = jnp.pad(row_token, (0, tm + rows_buf))

    def w_in_map(i, j, te, nt, tok):
        return (te[i], 0, jnp.where(i < nt[0], j, nf - 1))

    def w_out_map(i, j, te, nt, tok):
        return (te[i], jnp.where(i < nt[0], j, nf - 1), 0)

    grid_spec = pltpu.PrefetchScalarGridSpec(
        num_scalar_prefetch=3,
        grid=(n_tiles, nf),
        in_specs=[
            pl.BlockSpec(memory_space=pl.ANY),
            pl.BlockSpec((None, d, tf), w_in_map),
            pl.BlockSpec((None, d, tf), w_in_map),
            pl.BlockSpec((None, tf, d), w_out_map),
        ],
        out_specs=pl.BlockSpec((tm, d), lambda i, j, te, nt, tok: (i, 0)),
        scratch_shapes=[
            pltpu.VMEM((2, rows_buf, d), F32),
            pltpu.VMEM((tm, d), BF16),
            pltpu.SemaphoreType.DMA((2,)),
        ],
    )
    return pl.pallas_call(
        functools.partial(_moe_ffn_kernel, rows_per_step=rows_per_step),
        grid_spec=grid_spec,
        out_shape=jax.ShapeDtypeStruct((p, d), F32),
        compiler_params=_params("arbitrary", "arbitrary"),
        name="moe_ffn",
    )(tile_expert, num_tiles, row_token, h, we_gate, we_up, we_down)


def _moe_combine_kernel(pos_ref, x_ref, gate_ref, y_hbm, o_ref, buf, sems):
    i = pl.program_id(0)
    tm = x_ref.shape[0]
    slot = i % 2

    def issue_tile(tile, dst_slot):
        def issue(r, carry):
            for k in range(TOP_K):
                p = pos_ref[(tile * tm + r) * TOP_K + k]
                pltpu.make_async_copy(y_hbm.at[pl.ds(p, 1), :], buf.at[dst_slot, k, pl.ds(r, 1), :],
                                      sems.at[dst_slot]).start()
            return carry
        lax.fori_loop(0, tm, issue, 0, unroll=8)

    @pl.when(i == 0)
    def _():
        issue_tile(0, 0)

    @pl.when(i + 1 < pl.num_programs(0))
    def _():
        issue_tile(i + 1, 1 - slot)

    for k in range(TOP_K):
        pltpu.make_async_copy(y_hbm.at[pl.ds(0, tm), :], buf.at[slot, k], sems.at[slot]).wait()
    gate = gate_ref[...]
    o_ref[...] = x_ref[...] + (gate[:, 0:1] * buf[slot, 0] + gate[:, 1:2] * buf[slot, 1])


def _moe_combine(x, gate, y_sorted, pos, tm=256):
    n, d = x.shape
    grid_spec = pltpu.PrefetchScalarGridSpec(
        num_scalar_prefetch=1,
        grid=(n // tm,),
        in_specs=[
            pl.BlockSpec((tm, d), lambda i, pos: (i, 0)),
            pl.BlockSpec((tm, LANES), lambda i, pos: (i, 0)),
            pl.BlockSpec(memory_space=pl.ANY),
        ],
        out_specs=pl.BlockSpec((tm, d), lambda i, pos: (i, 0)),
        scratch_shapes=[
            pltpu.VMEM((2, TOP_K, tm, d), F32),
            pltpu.SemaphoreType.DMA((2,)),
        ],
    )
    return pl.pallas_call(
        _moe_combine_kernel,
        grid_spec=grid_spec,
        out_shape=jax.ShapeDtypeStruct((n, d), F32),
        compiler_params=_params("arbitrary"),
        name="moe_combine",
    )(pos, x, gate, y_sorted)


def _route_plan(top_idx, tm):
    n = top_idx.shape[0]
    n_pairs = n * TOP_K
    n_tiles = (n_pairs + N_EXPERTS * (tm - 1)) // tm
    e_flat = top_idx.reshape(n_pairs)
    onehot = (e_flat[:, None] == jnp.arange(N_EXPERTS, dtype=jnp.int32)[None, :]).astype(jnp.int32)
    csum = jnp.cumsum(onehot, axis=0)
    rank = jnp.sum(onehot * (csum - 1), axis=1)
    counts = csum[-1]
    tiles_per = (counts + tm - 1) // tm
    tile_end = jnp.cumsum(tiles_per)
    row_start = (tile_end - tiles_per) * tm
    pos = (row_start[e_flat] + rank).astype(jnp.int32)
    num_tiles = tile_end[-1:].astype(jnp.int32)
    tile_ids = jnp.arange(n_tiles, dtype=jnp.int32)
    tile_expert = jnp.sum((tile_end[None, :] <= tile_ids[:, None]).astype(jnp.int32), axis=1)
    tile_expert = jnp.minimum(tile_expert, N_EXPERTS - 1).astype(jnp.int32)
    row_token = jnp.zeros((n_tiles * tm,), jnp.int32).at[pos].set(
        jnp.arange(n_pairs, dtype=jnp.int32) // TOP_K)
    return tile_expert, num_tiles, row_token, pos


def kernel(x, l0_mix_norm, l0_w_qkv, l0_q_norm, l0_k_norm, l0_sink, l0_w_o, l0_ffn_norm, l0_w_gate_up, l0_w_down, l1_mix_norm, l1_w_in, l1_v_norm, l1_w_s, l1_b_s, l1_w_out, l1_ffn_norm, l1_w_router, l1_we_gate, l1_we_up, l1_we_down):
    batch, seq, d = x.shape
    n = batch * seq
    x0 = x.reshape(n, d)
    dq = N_HEADS * HEAD_DIM
    dkv = N_KV_HEADS * HEAD_DIM

    head_gain = jnp.concatenate([
        jnp.tile(l0_q_norm * (HEAD_DIM ** -0.5), N_HEADS),
        jnp.tile(l0_k_norm, N_KV_HEADS),
        jnp.ones((dkv,), F32),
    ])[None, :]
    qkv = _qkv_proj(x0, l0_mix_norm[None, :], l0_w_qkv.astype(BF16), head_gain, dq + dkv)
    attn = _attention(qkv, l0_sink, batch, seq)
    x1 = _proj_residual(attn, l0_w_o.astype(BF16), x0, name="attn_out_proj")

    x2 = _dense_ffn(x1, l0_ffn_norm[None, :], l0_w_gate_up.astype(BF16), l0_w_down.astype(BF16))

    u, v = _gmlp_in(x2, l1_mix_norm[None, :], l1_w_in.astype(BF16), l1_v_norm[None, :])
    b_full = jnp.repeat(l1_b_s.T, LANES, axis=1)
    w_router = jnp.pad(l1_w_router, ((0, 0), (0, LANES - N_EXPERTS)))
    w_router_hi = w_router.astype(BF16)
    w_router_lo = (w_router - w_router_hi.astype(F32)).astype(BF16)
    w_router_split = jnp.concatenate([w_router_hi, w_router_lo], axis=1)
    x3, h, ridx, rgate = _gmlp_out_router(u, v, l1_w_s.astype(BF16), b_full, l1_w_out.astype(BF16), x2,
                                          l1_ffn_norm[None, :], w_router_split)

    tm = 512
    tile_expert, num_tiles, row_token, pos = _route_plan(ridx[:, :TOP_K], tm)
    y_sorted = _moe_ffn(h, tile_expert, num_tiles, row_token,
                        l1_we_gate.astype(BF16), l1_we_up.astype(BF16), l1_we_down.astype(BF16), tm)
    out = _moe_combine(x3, rgate, y_sorted, pos)
    return out.reshape(batch, seq, d)
```

```python
import functools

import jax
import jax.numpy as jnp
from jax import lax
from jax.experimental import pallas as pl
from jax.experimental.pallas import tpu as pltpu

F32 = jnp.float32
BF16 = jnp.bfloat16

EPS = 1e-6
NEG_INF = -1e30
LANES = 128
SUBLANES = 8
HEAD_DIM = 128
N_HEADS = 16
N_KV_HEADS = 4
WINDOW = 128
BLOCK = 128
CHUNK = 128
N_EXPERTS = 8
TOP_K = 2

MIB = 1024 * 1024
VMEM_LIMIT = 56 * MIB


def _params(*sem):
    return pltpu.CompilerParams(dimension_semantics=sem, vmem_limit_bytes=VMEM_LIMIT)


def _resident(shape):
    zeros = (0,) * len(shape)
    return pl.BlockSpec(shape, lambda *_: zeros, pipeline_mode=pl.Buffered(1))


class _CastJob:
    def __init__(self, w, block_rows, grid):
        self.shape = w.shape
        cols = w.shape[-1]
        self.src = w.reshape(-1, cols)
        rows = self.src.shape[0]
        assert rows % block_rows == 0
        self.n_blocks = rows // block_rows
        steps = 1
        for g in grid:
            steps *= g
        assert self.n_blocks <= steps, (self.n_blocks, steps)
        self.always = self.n_blocks == steps
        strides = [1] * len(grid)
        for a in range(len(grid) - 2, -1, -1):
            strides[a] = strides[a + 1] * grid[a + 1]
        self.strides = tuple(strides)
        self.block = (block_rows, cols)

    def step(self, ids):
        return sum(i * s for i, s in zip(ids, self.strides))

    def spec(self):
        last = self.n_blocks - 1
        return pl.BlockSpec(self.block, lambda *ids: (jnp.minimum(self.step(ids[:len(self.strides)]), last), 0))

    def out_shape(self):
        return jax.ShapeDtypeStruct(self.src.shape, BF16)

    def run(self, src_ref, dst_ref):
        if self.always:
            dst_ref[...] = src_ref[...].astype(BF16)
        else:
            ids = [pl.program_id(a) for a in range(len(self.strides))]

            @pl.when(self.step(ids) < self.n_blocks)
            def _():
                dst_ref[...] = src_ref[...].astype(BF16)

    def result(self, dst):
        return dst.reshape(self.shape)


def _rms(x, gain):
    ms = jnp.mean(x * x, axis=-1, keepdims=True)
    return x * lax.rsqrt(ms + EPS) * gain


def _qkv_kernel(x_ref, g_ref, w_ref, hg_ref, o_ref, *, n_norm_cols, col_chunk):
    h = _rms(x_ref[...], g_ref[...]).astype(BF16)
    n_out = o_ref.shape[1]
    for c0 in range(0, n_out, col_chunk):
        y = jnp.dot(h, w_ref[:, c0:c0 + col_chunk], preferred_element_type=F32)
        for h0 in range(0, col_chunk, HEAD_DIM):
            col = c0 + h0
            yh = y[:, h0:h0 + HEAD_DIM]
            if col < n_norm_cols:
                yh = _rms(yh, hg_ref[:, col:col + HEAD_DIM])
            o_ref[:, col:col + HEAD_DIM] = yh.astype(BF16)


def _qkv_proj(x, gain, w, head_gain, n_norm_cols, tm=512):
    n, d = x.shape
    nq = w.shape[1]
    kern = functools.partial(_qkv_kernel, n_norm_cols=n_norm_cols, col_chunk=512)
    return pl.pallas_call(
        kern,
        grid=(n // tm,),
        in_specs=[
            pl.BlockSpec((tm, d), lambda i: (i, 0)),
            _resident((1, d)),
            _resident((d, nq)),
            _resident((1, nq)),
        ],
        out_specs=pl.BlockSpec((tm, nq), lambda i: (i, 0)),
        out_shape=jax.ShapeDtypeStruct((n, nq), BF16),
        compiler_params=_params("parallel"),
        name="qkv_proj",
    )(x, gain, w, head_gain)


def _attn_kernel(sink_ref, q_ref, kp_ref, kc_ref, kn_ref, vp_ref, vc_ref, vn_ref, bias_ref, wsrc_ref,
                 o_ref, wdst_ref, *, cast):
    cast.run(wsrc_ref, wdst_ref)
    kcat = jnp.concatenate([kp_ref[...], kc_ref[...], kn_ref[...]], axis=0)
    vcat = jnp.concatenate([vp_ref[...], vc_ref[...], vn_ref[...]], axis=0)
    group = N_HEADS // N_KV_HEADS
    band = 3 * BLOCK
    scores = []
    for kh in range(N_KV_HEADS):
        k_h = kcat[:, kh * HEAD_DIM:(kh + 1) * HEAD_DIM]
        heads = [kh * group + g for g in range(group)]
        q_g = jnp.concatenate([q_ref[:, hd * HEAD_DIM:(hd + 1) * HEAD_DIM] for hd in heads], axis=0)
        scores.append(lax.dot_general(k_h, q_g, (((1,), (1,)), ((), ())), preferred_element_type=F32))
    for kh in range(N_KV_HEADS):
        v_h = vcat[:, kh * HEAD_DIM:(kh + 1) * HEAD_DIM]
        heads = [kh * group + g for g in range(group)]
        sink = jnp.concatenate([jnp.full((1, BLOCK), sink_ref[hd], F32) for hd in heads], axis=1)
        s = scores[kh] + bias_ref[kh * band:(kh + 1) * band, :]
        m = jnp.maximum(jnp.max(s, axis=0, keepdims=True), sink)
        p = jnp.exp(s - m)
        denom = jnp.sum(p, axis=0, keepdims=True) + jnp.exp(sink - m)
        o_t = lax.dot_general(v_h, p.astype(BF16), (((0,), (0,)), ((), ())),
                              preferred_element_type=F32) * (1.0 / denom)
        for g, hd in enumerate(heads):
            o_ref[:, hd * HEAD_DIM:(hd + 1) * HEAD_DIM] = o_t[:, g * BLOCK:(g + 1) * BLOCK].T.astype(BF16)


def _attention_bias():
    group = N_HEADS // N_KV_HEADS
    qi = jnp.arange(BLOCK)[:, None]
    sj = jnp.arange(3 * BLOCK)[None, :]
    dist = jnp.abs(qi - sj + BLOCK)
    slopes = jnp.exp2(-8.0 * jnp.arange(1, N_HEADS + 1, dtype=F32) / N_HEADS)
    bias = jnp.where(dist <= WINDOW, -slopes[:, None, None] * dist.astype(F32), NEG_INF)
    first = (sj >= BLOCK)[None]
    last = (sj < 2 * BLOCK)[None]
    edge = jnp.stack([jnp.where(first, bias, NEG_INF), bias, jnp.where(last, bias, NEG_INF)])
    edge = edge.reshape(3, N_KV_HEADS, group, BLOCK, 3 * BLOCK).transpose(0, 1, 4, 2, 3)
    return edge.reshape(3, N_KV_HEADS * 3 * BLOCK, group * BLOCK)


def _attention(qkv, sink, batch, seq, cast_w):
    n = qkv.shape[0]
    nb = seq // BLOCK
    dq = N_HEADS * HEAD_DIM
    dkv = N_KV_HEADS * HEAD_DIM
    k_col = dq // dkv
    v_col = k_col + 1

    def prev_blk(b, j):
        return b * nb + jnp.maximum(j - 1, 0)

    def next_blk(b, j):
        return b * nb + jnp.minimum(j + 1, nb - 1)

    assert nb >= 2, "first and last query block must differ"
    group = N_HEADS // N_KV_HEADS

    def bias_variant(b, j):
        return (jnp.where(j == 0, 0, jnp.where(j == nb - 1, 2, 1)), 0, 0)

    grid = (batch, nb)
    cast = _CastJob(cast_w, cast_w.size // cast_w.shape[-1] // (batch * nb), grid)
    attn, w_bf16 = pl.pallas_call(
        functools.partial(_attn_kernel, cast=cast),
        grid=grid,
        in_specs=[
            pl.BlockSpec(memory_space=pltpu.SMEM),
            pl.BlockSpec((BLOCK, dq), lambda b, j: (b * nb + j, 0)),
            pl.BlockSpec((BLOCK, dkv), lambda b, j: (prev_blk(b, j), k_col)),
            pl.BlockSpec((BLOCK, dkv), lambda b, j: (b * nb + j, k_col)),
            pl.BlockSpec((BLOCK, dkv), lambda b, j: (next_blk(b, j), k_col)),
            pl.BlockSpec((BLOCK, dkv), lambda b, j: (prev_blk(b, j), v_col)),
            pl.BlockSpec((BLOCK, dkv), lambda b, j: (b * nb + j, v_col)),
            pl.BlockSpec((BLOCK, dkv), lambda b, j: (next_blk(b, j), v_col)),
            pl.BlockSpec((None, N_KV_HEADS * 3 * BLOCK, group * BLOCK), bias_variant),
            cast.spec(),
        ],
        out_specs=[pl.BlockSpec((BLOCK, dq), lambda b, j: (b * nb + j, 0)), cast.spec()],
        out_shape=[jax.ShapeDtypeStruct((n, dq), BF16), cast.out_shape()],
        compiler_params=_params("arbitrary", "arbitrary"),
        name="window_attention",
    )(sink, qkv, qkv, qkv, qkv, qkv, qkv, qkv, _attention_bias(), cast.src)
    return attn, cast.result(w_bf16)


def _proj_res_kernel(a_ref, w_ref, r_ref, o_ref):
    o_ref[...] = r_ref[...] + jnp.dot(a_ref[...], w_ref[...], preferred_element_type=F32)


def _proj_residual(a, w, res, tm=512, name="proj_residual"):
    n, k = a.shape
    d = w.shape[1]
    return pl.pallas_call(
        _proj_res_kernel,
        grid=(n // tm,),
        in_specs=[
            pl.BlockSpec((tm, k), lambda i: (i, 0)),
            _resident((k, d)),
            pl.BlockSpec((tm, d), lambda i: (i, 0)),
        ],
        out_specs=pl.BlockSpec((tm, d), lambda i: (i, 0)),
        out_shape=jax.ShapeDtypeStruct((n, d), F32),
        compiler_params=_params("parallel"),
        name=name,
    )(a, w, res)


def _dense_ffn_kernel(x_ref, g_ref, wg_ref, wu_ref, wd_ref, wsrc_ref, o_ref, wdst_ref, h_ref, *, cast):
    cast.run(wsrc_ref, wdst_ref)

    @pl.when(pl.program_id(1) == 0)
    def _():
        x = x_ref[...]
        h_ref[...] = _rms(x, g_ref[...]).astype(BF16)
        o_ref[...] = x

    h = h_ref[...]
    gt = jnp.dot(h, wg_ref[...], preferred_element_type=F32)
    up = jnp.dot(h, wu_ref[...], preferred_element_type=F32)
    act = (gt * jax.nn.sigmoid(gt) * up).astype(BF16)
    o_ref[...] += jnp.dot(act, wd_ref[...], preferred_element_type=F32)


def _dense_ffn(x, gain, w_gate_up, w_down, cast_w, tm=512, tf=512, cast_rows=256):
    n, d = x.shape
    f = w_down.shape[0]
    nf = f // tf
    grid = (n // tm, nf)
    cast = _CastJob(cast_w, cast_rows, grid)
    out, w_bf16 = pl.pallas_call(
        functools.partial(_dense_ffn_kernel, cast=cast),
        grid=grid,
        in_specs=[
            pl.BlockSpec((tm, d), lambda i, j: (i, 0)),
            _resident((1, d)),
            pl.BlockSpec((d, tf), lambda i, j: (0, j)),
            pl.BlockSpec((d, tf), lambda i, j: (0, nf + j)),
            pl.BlockSpec((tf, d), lambda i, j: (j, 0)),
            cast.spec(),
        ],
        out_specs=[pl.BlockSpec((tm, d), lambda i, j: (i, 0)), cast.spec()],
        out_shape=[jax.ShapeDtypeStruct((n, d), F32), cast.out_shape()],
        scratch_shapes=[pltpu.VMEM((tm, d), BF16)],
        compiler_params=_params("arbitrary", "arbitrary"),
        name="dense_ffn",
    )(x, gain, w_gate_up, w_gate_up, w_down, cast.src)
    return out, cast.result(w_bf16)


def _gmlp_in_kernel(x_ref, g_ref, w_ref, vg_ref, wsrc_ref, u_ref, v_ref, wdst_ref, vf_ref,
                    *, col_chunk, cast):
    cast.run(wsrc_ref, wdst_ref)
    h = _rms(x_ref[...], g_ref[...]).astype(BF16)
    width = u_ref.shape[1]
    for c0 in range(0, width, col_chunk):
        z = jnp.dot(h, w_ref[:, c0:c0 + col_chunk], preferred_element_type=F32)
        u_ref[:, c0:c0 + col_chunk] = jax.nn.gelu(z).astype(BF16)
    ssq = jnp.zeros((x_ref.shape[0], 1), F32)
    for c0 in range(0, width, col_chunk):
        z = jnp.dot(h, w_ref[:, width + c0:width + c0 + col_chunk], preferred_element_type=F32)
        v = jax.nn.gelu(z)
        ssq = ssq + jnp.sum(v * v, axis=-1, keepdims=True)
        vf_ref[:, c0:c0 + col_chunk] = v
    inv = lax.rsqrt(ssq / width + EPS)
    v_ref[...] = (vf_ref[...] * inv * vg_ref[...]).astype(BF16)


def _gmlp_in(x, gain, w_in, v_gain, cast_w, tm=256):
    n, d = x.shape
    width = w_in.shape[1] // 2
    grid = (n // tm,)
    cast = _CastJob(cast_w, cast_w.size // cast_w.shape[-1] // grid[0], grid)
    kern = functools.partial(_gmlp_in_kernel, col_chunk=512, cast=cast)
    u, v, w_bf16 = pl.pallas_call(
        kern,
        grid=grid,
        in_specs=[
            pl.BlockSpec((tm, d), lambda i: (i, 0)),
            _resident((1, d)),
            _resident((d, 2 * width)),
            _resident((1, width)),
            cast.spec(),
        ],
        out_specs=[
            pl.BlockSpec((tm, width), lambda i: (i, 0)),
            pl.BlockSpec((tm, width), lambda i: (i, 0)),
            cast.spec(),
        ],
        out_shape=[
            jax.ShapeDtypeStruct((n, width), BF16),
            jax.ShapeDtypeStruct((n, width), BF16),
            cast.out_shape(),
        ],
        scratch_shapes=[pltpu.VMEM((tm, width), F32)],
        compiler_params=_params("arbitrary"),
        name="gmlp_in",
    )(x, gain, w_in, v_gain, cast.src)
    return u, v, cast.result(w_bf16)


def _top2_route(logits):
    lane = lax.broadcasted_iota(jnp.int32, logits.shape, 1)
    lg = jnp.where(lane < N_EXPERTS, logits, -jnp.inf)
    m1 = jnp.max(lg, axis=-1, keepdims=True)
    i1 = jnp.min(jnp.where(lg == m1, lane, LANES), axis=-1, keepdims=True)
    lg2 = jnp.where(lane == i1, -jnp.inf, lg)
    m2 = jnp.max(lg2, axis=-1, keepdims=True)
    i2 = jnp.min(jnp.where(lg2 == m2, lane, LANES), axis=-1, keepdims=True)
    e2 = jnp.exp(m2 - m1)
    w1 = 1.0 / (1.0 + e2)
    w2 = e2 / (1.0 + e2)
    idx = jnp.where(lane == 0, i1, jnp.where(lane == 1, i2, 0))
    gate = jnp.where(lane == 0, w1, jnp.where(lane == 1, w2, 0.0))
    return idx, gate


def _gmlp_out_router_kernel(u_ref, v_ref, ws_ref, bs_ref, wo_ref, r_ref, g_ref, wr_ref,
                            x_ref, h_ref, idx_ref, gate_ref, y_ref):
    tm, width = u_ref.shape
    for c0 in range(0, tm, CHUNK):
        for g in range(width // LANES):
            cols = slice(g * LANES, (g + 1) * LANES)
            vv = v_ref[c0:c0 + CHUNK, cols]
            mixed = jnp.dot(ws_ref[g], vv, preferred_element_type=F32) + bs_ref[:, cols]
            y_ref[c0:c0 + CHUNK, cols] = (u_ref[c0:c0 + CHUNK, cols].astype(F32) * mixed).astype(BF16)
    x = r_ref[...] + jnp.dot(y_ref[...], wo_ref[...], preferred_element_type=F32)
    x_ref[...] = x
    h = _rms(x, g_ref[...])
    h_ref[...] = h
    h_hi = h.astype(BF16)
    h_lo = (h - h_hi.astype(F32)).astype(BF16)
    a = jnp.dot(h_hi, wr_ref[...], preferred_element_type=F32)
    b = jnp.dot(h_lo, wr_ref[:, 0:LANES], preferred_element_type=F32)
    logits = a[:, 0:LANES] + (a[:, LANES:2 * LANES] + b)
    idx_ref[...], gate_ref[...] = _top2_route(logits)


def _gmlp_out_router(u, v, w_s, b_full, w_out, res, ffn_gain, w_router_split, tm=512):
    n, width = u.shape
    d = w_out.shape[1]
    groups = w_s.shape[0]
    row = lambda i: (i, 0)
    return pl.pallas_call(
        _gmlp_out_router_kernel,
        grid=(n // tm,),
        in_specs=[
            pl.BlockSpec((tm, width), row),
            pl.BlockSpec((tm, width), row),
            _resident((groups, CHUNK, CHUNK)),
            _resident((CHUNK, width)),
            _resident((width, d)),
            pl.BlockSpec((tm, d), row),
            _resident((1, d)),
            _resident((d, 2 * LANES)),
        ],
        out_specs=[
            pl.BlockSpec((tm, d), row),
            pl.BlockSpec((tm, d), row),
            pl.BlockSpec((tm, LANES), row),
            pl.BlockSpec((tm, LANES), row),
        ],
        out_shape=[
            jax.ShapeDtypeStruct((n, d), F32),
            jax.ShapeDtypeStruct((n, d), F32),
            jax.ShapeDtypeStruct((n, LANES), jnp.int32),
            jax.ShapeDtypeStruct((n, LANES), F32),
        ],
        scratch_shapes=[pltpu.VMEM((tm, width), BF16)],
        compiler_params=_params("parallel"),
        name="gmlp_out_router",
    )(u, v, w_s, b_full, w_out, res, ffn_gain, w_router_split)


def _moe_ffn_kernel(te_ref, nt_ref, tok_ref, h_hbm, wg_ref, wu_ref, wd_ref, o_ref,
                    xbuf, hb_ref, sems, *, rows_per_step):
    i = pl.program_id(0)
    j = pl.program_id(1)
    n_tiles = pl.num_programs(0)
    nf = pl.num_programs(1)
    tm = hb_ref.shape[0]
    rows_buf = xbuf.shape[1]
    slot = i % 2

    def row_copy(tile, r, dst_slot):
        tok = tok_ref[tile * tm + r]
        return pltpu.make_async_copy(h_hbm.at[pl.ds(tok, 1), :], xbuf.at[dst_slot, pl.ds(r, 1), :],
                                     sems.at[dst_slot])

    def wait_slot(s):
        pltpu.make_async_copy(h_hbm.at[pl.ds(0, rows_buf), :], xbuf.at[s], sems.at[s]).wait()

    def prefetch_next_tile():
        for k in range(rows_per_step):
            row_copy(i + 1, j * rows_per_step + k, 1 - slot).start()

    @pl.when((i == 0) & (j == 0))
    def _():
        def issue(r, carry):
            row_copy(0, r, 0).start()
            return carry
        lax.fori_loop(0, rows_buf, issue, 0)

    @pl.when(j == 0)
    def _():
        wait_slot(slot)
        hb_ref[...] = xbuf[slot, 0:tm, :].astype(BF16)
        o_ref[...] = jnp.zeros_like(o_ref)

    active = i < nt_ref[0]

    @pl.when(active)
    def _():
        prefetch_next_tile()
        h = hb_ref[...]
        gt = jnp.dot(h, wg_ref[...], preferred_element_type=F32)
        up = jnp.dot(h, wu_ref[...], preferred_element_type=F32)
        act = (gt * jax.nn.sigmoid(gt) * up).astype(BF16)
        o_ref[...] += jnp.dot(act, wd_ref[...], preferred_element_type=F32)

    @pl.when(jnp.logical_not(active))
    def _():
        prefetch_next_tile()

    @pl.when((i == n_tiles - 1) & (j == nf - 1))
    def _():
        wait_slot(1 - slot)


def _moe_ffn(h, tile_expert, num_tiles, row_token, we_gate, we_up, we_down, tm, tf=1024):
    n, d = h.shape
    f = we_gate.shape[2]
    nf = f // tf
    n_tiles = tile_expert.shape[0]
    p = n_tiles * tm
    rows_per_step = -(-tm // (nf * SUBLANES)) * SUBLANES
    rows_buf = rows_per_step * nf
    row_token = jnp.pad(row_token, (0, tm + rows_buf))

    def w_in_map(i, j, te, nt, tok):
        return (te[i], 0, jnp.where(i < nt[0], j, nf - 1))

    def w_out_map(i, j, te, nt, tok):
        return (te[i], jnp.where(i < nt[0], j, nf - 1), 0)

    grid_spec = pltpu.PrefetchScalarGridSpec(
        num_scalar_prefetch=3,
        grid=(n_tiles, nf),
        in_specs=[
            pl.BlockSpec(memory_space=pl.ANY),
            pl.BlockSpec((None, d, tf), w_in_map),
            pl.BlockSpec((None, d, tf), w_in_map),
            pl.BlockSpec((None, tf, d), w_out_map),
        ],
        out_specs=pl.BlockSpec((tm, d), lambda i, j, te, nt, tok: (i, 0)),
        scratch_shapes=[
            pltpu.VMEM((2, rows_buf, d), F32),
            pltpu.VMEM((tm, d), BF16),
            pltpu.SemaphoreType.DMA((2,)),
        ],
    )
    return pl.pallas_call(
        functools.partial(_moe_ffn_kernel, rows_per_step=rows_per_step),
        grid_spec=grid_spec,
        out_shape=jax.ShapeDtypeStruct((p, d), F32),
        compiler_params=_params("arbitrary", "arbitrary"),
        name="moe_ffn",
    )(tile_expert, num_tiles, row_token, h, we_gate, we_up, we_down)


def _moe_combine_kernel(pos_ref, x_ref, gate_ref, y_hbm, o_ref, buf, sems):
    i = pl.program_id(0)
    tm = x_ref.shape[0]
    slot = i % 2

    def issue_tile(tile, dst_slot):
        def issue(r, carry):
            for k in range(TOP_K):
                p = pos_ref[(tile * tm + r) * TOP_K + k]
                pltpu.make_async_copy(y_hbm.at[pl.ds(p, 1), :], buf.at[dst_slot, k, pl.ds(r, 1), :],
                                      sems.at[dst_slot]).start()
            return carry
        lax.fori_loop(0, tm, issue, 0, unroll=8)

    @pl.when(i == 0)
    def _():
        issue_tile(0, 0)

    @pl.when(i + 1 < pl.num_programs(0))
    def _():
        issue_tile(i + 1, 1 - slot)

    for k in range(TOP_K):
        pltpu.make_async_copy(y_hbm.at[pl.ds(0, tm), :], buf.at[slot, k], sems.at[slot]).wait()
    gate = gate_ref[...]
    o_ref[...] = x_ref[...] + (gate[:, 0:1] * buf[slot, 0] + gate[:, 1:2] * buf[slot, 1])


def _moe_combine(x, gate, y_sorted, pos, tm=256):
    n, d = x.shape
    grid_spec = pltpu.PrefetchScalarGridSpec(
        num_scalar_prefetch=1,
        grid=(n // tm,),
        in_specs=[
            pl.BlockSpec((tm, d), lambda i, pos: (i, 0)),
            pl.BlockSpec((tm, LANES), lambda i, pos: (i, 0)),
            pl.BlockSpec(memory_space=pl.ANY),
        ],
        out_specs=pl.BlockSpec((tm, d), lambda i, pos: (i, 0)),
        scratch_shapes=[
            pltpu.VMEM((2, TOP_K, tm, d), F32),
            pltpu.SemaphoreType.DMA((2,)),
        ],
    )
    return pl.pallas_call(
        _moe_combine_kernel,
        grid_spec=grid_spec,
        out_shape=jax.ShapeDtypeStruct((n, d), F32),
        compiler_params=_params("arbitrary"),
        name="moe_combine",
    )(pos, x, gate, y_sorted)


def _route_plan(top_idx, tm):
    n = top_idx.shape[0]
    n_pairs = n * TOP_K
    n_tiles = (n_pairs + N_EXPERTS * (tm - 1)) // tm
    e_flat = top_idx.reshape(n_pairs)
    onehot = (e_flat[:, None] == jnp.arange(N_EXPERTS, dtype=jnp.int32)[None, :]).astype(jnp.int32)
    csum = jnp.cumsum(onehot, axis=0)
    rank = jnp.sum(onehot * (csum - 1), axis=1)
    counts = csum[-1]
    tiles_per = (counts + tm - 1) // tm
    tile_end = jnp.cumsum(tiles_per)
    row_start = (tile_end - tiles_per) * tm
    pos = (row_start[e_flat] + rank).astype(jnp.int32)
    num_tiles = tile_end[-1:].astype(jnp.int32)
    tile_ids = jnp.arange(n_tiles, dtype=jnp.int32)
    tile_expert = jnp.sum((tile_end[None, :] <= tile_ids[:, None]).astype(jnp.int32), axis=1)
    tile_expert = jnp.minimum(tile_expert, N_EXPERTS - 1).astype(jnp.int32)
    row_token = jnp.zeros((n_tiles * tm,), jnp.int32).at[pos].set(
        jnp.arange(n_pairs, dtype=jnp.int32) // TOP_K)
    return tile_expert, num_tiles, row_token, pos


def kernel(x, l0_mix_norm, l0_w_qkv, l0_q_norm, l0_k_norm, l0_sink, l0_w_o, l0_ffn_norm, l0_w_gate_up, l0_w_down, l1_mix_norm, l1_w_in, l1_v_norm, l1_w_s, l1_b_s, l1_w_out, l1_ffn_norm, l1_w_router, l1_we_gate, l1_we_up, l1_we_down):
    batch, seq, d = x.shape
    n = batch * seq
    x0 = x.reshape(n, d)
    dq = N_HEADS * HEAD_DIM
    dkv = N_KV_HEADS * HEAD_DIM

    head_gain = jnp.concatenate([
        jnp.tile(l0_q_norm * (HEAD_DIM ** -0.5), N_HEADS),
        jnp.tile(l0_k_norm, N_KV_HEADS),
        jnp.ones((dkv,), F32),
    ])[None, :]
    qkv = _qkv_proj(x0, l0_mix_norm[None, :], l0_w_qkv.astype(BF16), head_gain, dq + dkv)
    attn, we_gate = _attention(qkv, l0_sink, batch, seq, l1_we_gate)
    x1 = _proj_residual(attn, l0_w_o.astype(BF16), x0, name="attn_out_proj")

    x2, we_down = _dense_ffn(x1, l0_ffn_norm[None, :], l0_w_gate_up.astype(BF16), l0_w_down.astype(BF16),
                             l1_we_down)

    u, v, we_up = _gmlp_in(x2, l1_mix_norm[None, :], l1_w_in.astype(BF16), l1_v_norm[None, :], l1_we_up)
    b_full = jnp.repeat(l1_b_s.T, LANES, axis=1)
    w_router = jnp.pad(l1_w_router, ((0, 0), (0, LANES - N_EXPERTS)))
    w_router_hi = w_router.astype(BF16)
    w_router_lo = (w_router - w_router_hi.astype(F32)).astype(BF16)
    w_router_split = jnp.concatenate([w_router_hi, w_router_lo], axis=1)
    x3, h, ridx, rgate = _gmlp_out_router(u, v, l1_w_s.astype(BF16), b_full, l1_w_out.astype(BF16), x2,
                                          l1_ffn_norm[None, :], w_router_split)

    tm = 512
    tile_expert, num_tiles, row_token, pos = _route_plan(ridx[:, :TOP_K], tm)
    y_sorted = _moe_ffn(h, tile_expert, num_tiles, row_token, we_gate, we_up, we_down, tm)
    out = _moe_combine(x3, rgate, y_sorted, pos)
    return out.reshape(batch, seq, d)
```

```python
import functools

import jax
import jax.numpy as jnp
from jax import lax
from jax.experimental import pallas as pl
from jax.experimental.pallas import tpu as pltpu

F32 = jnp.float32
BF16 = jnp.bfloat16

EPS = 1e-6
NEG_INF = -1e30
LANES = 128
SUBLANES = 8
HEAD_DIM = 128
N_HEADS = 16
N_KV_HEADS = 4
WINDOW = 128
BLOCK = 128
CHUNK = 128
N_EXPERTS = 8
TOP_K = 2
MOE_TF = 1024

MIB = 1024 * 1024
VMEM_LIMIT = 56 * MIB


def _params(*sem):
    return pltpu.CompilerParams(dimension_semantics=sem, vmem_limit_bytes=VMEM_LIMIT)


def _resident(shape):
    zeros = (0,) * len(shape)
    return pl.BlockSpec(shape, lambda *_: zeros, pipeline_mode=pl.Buffered(1))


class _CastJob:
    def __init__(self, w, block_rows, grid, col_tile=None):
        self.e, self.k, self.f = w.shape
        self.src = w.reshape(self.e * self.k, self.f)
        assert (self.e * self.k) % block_rows == 0 and self.k % block_rows == 0
        self.block_rows = block_rows
        self.n_blocks = self.e * self.k // block_rows
        steps = 1
        for g in grid:
            steps *= g
        assert self.n_blocks <= steps, (self.n_blocks, steps)
        self.always = self.n_blocks == steps
        strides = [1] * len(grid)
        for a in range(len(grid) - 2, -1, -1):
            strides[a] = strides[a + 1] * grid[a + 1]
        self.strides = tuple(strides)
        self.col_tile = col_tile
        assert col_tile is None or self.f % col_tile == 0

    def _step(self, ids):
        return sum(i * s for i, s in zip(ids, self.strides))

    def _block(self, ids):
        return jnp.minimum(self._step(ids[:len(self.strides)]), self.n_blocks - 1)

    def src_spec(self):
        return pl.BlockSpec((self.block_rows, self.f), lambda *ids: (self._block(ids), 0))

    def dst_spec(self):
        if self.col_tile is None:
            return self.src_spec()
        per_expert = self.k // self.block_rows
        return pl.BlockSpec(
            (None, self.f // self.col_tile, self.block_rows, self.col_tile),
            lambda *ids: (self._block(ids) // per_expert, 0, self._block(ids) % per_expert, 0))

    def out_shape(self):
        if self.col_tile is None:
            return jax.ShapeDtypeStruct(self.src.shape, BF16)
        return jax.ShapeDtypeStruct((self.e, self.f // self.col_tile, self.k, self.col_tile), BF16)

    def _cast_block(self, src_ref, dst_ref):
        if self.col_tile is None:
            dst_ref[...] = src_ref[...].astype(BF16)
        else:
            for c in range(self.f // self.col_tile):
                dst_ref[c] = src_ref[:, c * self.col_tile:(c + 1) * self.col_tile].astype(BF16)

    def run(self, src_ref, dst_ref):
        if self.always:
            self._cast_block(src_ref, dst_ref)
        else:
            ids = [pl.program_id(a) for a in range(len(self.strides))]

            @pl.when(self._step(ids) < self.n_blocks)
            def _():
                self._cast_block(src_ref, dst_ref)

    def result(self, dst):
        return dst if self.col_tile is not None else dst.reshape(self.e, self.k, self.f)


def _rms(x, gain):
    ms = jnp.mean(x * x, axis=-1, keepdims=True)
    return x * lax.rsqrt(ms + EPS) * gain


def _qkv_kernel(x_ref, g_ref, w_ref, hg_ref, o_ref, *, n_norm_cols, col_chunk):
    h = _rms(x_ref[...], g_ref[...]).astype(BF16)
    n_out = o_ref.shape[1]
    for c0 in range(0, n_out, col_chunk):
        y = jnp.dot(h, w_ref[:, c0:c0 + col_chunk], preferred_element_type=F32)
        for h0 in range(0, col_chunk, HEAD_DIM):
            col = c0 + h0
            yh = y[:, h0:h0 + HEAD_DIM]
            if col < n_norm_cols:
                yh = _rms(yh, hg_ref[:, col:col + HEAD_DIM])
            o_ref[:, col:col + HEAD_DIM] = yh.astype(BF16)


def _qkv_proj(x, gain, w, head_gain, n_norm_cols, tm=512):
    n, d = x.shape
    nq = w.shape[1]
    kern = functools.partial(_qkv_kernel, n_norm_cols=n_norm_cols, col_chunk=512)
    return pl.pallas_call(
        kern,
        grid=(n // tm,),
        in_specs=[
            pl.BlockSpec((tm, d), lambda i: (i, 0)),
            _resident((1, d)),
            _resident((d, nq)),
            _resident((1, nq)),
        ],
        out_specs=pl.BlockSpec((tm, nq), lambda i: (i, 0)),
        out_shape=jax.ShapeDtypeStruct((n, nq), BF16),
        compiler_params=_params("parallel"),
        name="qkv_proj",
    )(x, gain, w, head_gain)


def _attn_kernel(sink_ref, q_ref, kp_ref, kc_ref, kn_ref, vp_ref, vc_ref, vn_ref, bias_ref, wsrc_ref,
                 o_ref, wdst_ref, *, cast):
    cast.run(wsrc_ref, wdst_ref)
    kcat = jnp.concatenate([kp_ref[...], kc_ref[...], kn_ref[...]], axis=0)
    vcat = jnp.concatenate([vp_ref[...], vc_ref[...], vn_ref[...]], axis=0)
    group = N_HEADS // N_KV_HEADS
    band = 3 * BLOCK
    scores = []
    for kh in range(N_KV_HEADS):
        k_h = kcat[:, kh * HEAD_DIM:(kh + 1) * HEAD_DIM]
        heads = [kh * group + g for g in range(group)]
        q_g = jnp.concatenate([q_ref[:, hd * HEAD_DIM:(hd + 1) * HEAD_DIM] for hd in heads], axis=0)
        scores.append(lax.dot_general(k_h, q_g, (((1,), (1,)), ((), ())), preferred_element_type=F32))
    for kh in range(N_KV_HEADS):
        v_h = vcat[:, kh * HEAD_DIM:(kh + 1) * HEAD_DIM]
        heads = [kh * group + g for g in range(group)]
        sink = jnp.concatenate([jnp.full((1, BLOCK), sink_ref[hd], F32) for hd in heads], axis=1)
        s = scores[kh] + bias_ref[kh * band:(kh + 1) * band, :]
        m = jnp.maximum(jnp.max(s, axis=0, keepdims=True), sink)
        p = jnp.exp(s - m)
        denom = jnp.sum(p, axis=0, keepdims=True) + jnp.exp(sink - m)
        o_t = lax.dot_general(v_h, p.astype(BF16), (((0,), (0,)), ((), ())),
                              preferred_element_type=F32) * (1.0 / denom)
        for g, hd in enumerate(heads):
            o_ref[:, hd * HEAD_DIM:(hd + 1) * HEAD_DIM] = o_t[:, g * BLOCK:(g + 1) * BLOCK].T.astype(BF16)


def _attention_bias():
    group = N_HEADS // N_KV_HEADS
    qi = jnp.arange(BLOCK)[:, None]
    sj = jnp.arange(3 * BLOCK)[None, :]
    dist = jnp.abs(qi - sj + BLOCK)
    slopes = jnp.exp2(-8.0 * jnp.arange(1, N_HEADS + 1, dtype=F32) / N_HEADS)
    bias = jnp.where(dist <= WINDOW, -slopes[:, None, None] * dist.astype(F32), NEG_INF)
    first = (sj >= BLOCK)[None]
    last = (sj < 2 * BLOCK)[None]
    edge = jnp.stack([jnp.where(first, bias, NEG_INF), bias, jnp.where(last, bias, NEG_INF)])
    edge = edge.reshape(3, N_KV_HEADS, group, BLOCK, 3 * BLOCK).transpose(0, 1, 4, 2, 3)
    return edge.reshape(3, N_KV_HEADS * 3 * BLOCK, group * BLOCK)


def _attention(qkv, sink, batch, seq, cast_w, cast_col_tile):
    n = qkv.shape[0]
    nb = seq // BLOCK
    dq = N_HEADS * HEAD_DIM
    dkv = N_KV_HEADS * HEAD_DIM
    k_col = dq // dkv
    v_col = k_col + 1

    def prev_blk(b, j):
        return b * nb + jnp.maximum(j - 1, 0)

    def next_blk(b, j):
        return b * nb + jnp.minimum(j + 1, nb - 1)

    assert nb >= 2, "first and last query block must differ"
    group = N_HEADS // N_KV_HEADS

    def bias_variant(b, j):
        return (jnp.where(j == 0, 0, jnp.where(j == nb - 1, 2, 1)), 0, 0)

    grid = (batch, nb)
    cast = _CastJob(cast_w, cast_w.size // cast_w.shape[-1] // (batch * nb), grid, cast_col_tile)
    attn, w_bf16 = pl.pallas_call(
        functools.partial(_attn_kernel, cast=cast),
        grid=grid,
        in_specs=[
            pl.BlockSpec(memory_space=pltpu.SMEM),
            pl.BlockSpec((BLOCK, dq), lambda b, j: (b * nb + j, 0)),
            pl.BlockSpec((BLOCK, dkv), lambda b, j: (prev_blk(b, j), k_col)),
            pl.BlockSpec((BLOCK, dkv), lambda b, j: (b * nb + j, k_col)),
            pl.BlockSpec((BLOCK, dkv), lambda b, j: (next_blk(b, j), k_col)),
            pl.BlockSpec((BLOCK, dkv), lambda b, j: (prev_blk(b, j), v_col)),
            pl.BlockSpec((BLOCK, dkv), lambda b, j: (b * nb + j, v_col)),
            pl.BlockSpec((BLOCK, dkv), lambda b, j: (next_blk(b, j), v_col)),
            pl.BlockSpec((None, N_KV_HEADS * 3 * BLOCK, group * BLOCK), bias_variant),
            cast.src_spec(),
        ],
        out_specs=[pl.BlockSpec((BLOCK, dq), lambda b, j: (b * nb + j, 0)), cast.dst_spec()],
        out_shape=[jax.ShapeDtypeStruct((n, dq), BF16), cast.out_shape()],
        compiler_params=_params("arbitrary", "arbitrary"),
        name="window_attention",
    )(sink, qkv, qkv, qkv, qkv, qkv, qkv, qkv, _attention_bias(), cast.src)
    return attn, cast.result(w_bf16)


def _proj_res_kernel(a_ref, w_ref, r_ref, o_ref):
    o_ref[...] = r_ref[...] + jnp.dot(a_ref[...], w_ref[...], preferred_element_type=F32)


def _proj_residual(a, w, res, tm=512, name="proj_residual"):
    n, k = a.shape
    d = w.shape[1]
    return pl.pallas_call(
        _proj_res_kernel,
        grid=(n // tm,),
        in_specs=[
            pl.BlockSpec((tm, k), lambda i: (i, 0)),
            _resident((k, d)),
            pl.BlockSpec((tm, d), lambda i: (i, 0)),
        ],
        out_specs=pl.BlockSpec((tm, d), lambda i: (i, 0)),
        out_shape=jax.ShapeDtypeStruct((n, d), F32),
        compiler_params=_params("parallel"),
        name=name,
    )(a, w, res)


def _dense_ffn_kernel(x_ref, g_ref, wg_ref, wu_ref, wd_ref, wsrc_ref, o_ref, wdst_ref, h_ref, *, cast):
    cast.run(wsrc_ref, wdst_ref)

    @pl.when(pl.program_id(1) == 0)
    def _():
        x = x_ref[...]
        h_ref[...] = _rms(x, g_ref[...]).astype(BF16)
        o_ref[...] = x

    h = h_ref[...]
    gt = jnp.dot(h, wg_ref[...], preferred_element_type=F32)
    up = jnp.dot(h, wu_ref[...], preferred_element_type=F32)
    act = (gt * jax.nn.sigmoid(gt) * up).astype(BF16)
    o_ref[...] += jnp.dot(act, wd_ref[...], preferred_element_type=F32)


def _dense_ffn(x, gain, w_gate_up, w_down, cast_w, tm=512, tf=512, cast_rows=256):
    n, d = x.shape
    f = w_down.shape[0]
    nf = f // tf
    grid = (n // tm, nf)
    cast = _CastJob(cast_w, cast_rows, grid)
    out, w_bf16 = pl.pallas_call(
        functools.partial(_dense_ffn_kernel, cast=cast),
        grid=grid,
        in_specs=[
            pl.BlockSpec((tm, d), lambda i, j: (i, 0)),
            _resident((1, d)),
            pl.BlockSpec((d, tf), lambda i, j: (0, j)),
            pl.BlockSpec((d, tf), lambda i, j: (0, nf + j)),
            pl.BlockSpec((tf, d), lambda i, j: (j, 0)),
            cast.src_spec(),
        ],
        out_specs=[pl.BlockSpec((tm, d), lambda i, j: (i, 0)), cast.dst_spec()],
        out_shape=[jax.ShapeDtypeStruct((n, d), F32), cast.out_shape()],
        scratch_shapes=[pltpu.VMEM((tm, d), BF16)],
        compiler_params=_params("arbitrary", "arbitrary"),
        name="dense_ffn",
    )(x, gain, w_gate_up, w_gate_up, w_down, cast.src)
    return out, cast.result(w_bf16)


def _gmlp_in_kernel(x_ref, g_ref, w_ref, vg_ref, wsrc_ref, u_ref, v_ref, wdst_ref, vf_ref,
                    *, col_chunk, cast):
    cast.run(wsrc_ref, wdst_ref)
    h = _rms(x_ref[...], g_ref[...]).astype(BF16)
    width = u_ref.shape[1]
    for c0 in range(0, width, col_chunk):
        z = jnp.dot(h, w_ref[:, c0:c0 + col_chunk], preferred_element_type=F32)
        u_ref[:, c0:c0 + col_chunk] = jax.nn.gelu(z).astype(BF16)
    ssq = jnp.zeros((x_ref.shape[0], 1), F32)
    for c0 in range(0, width, col_chunk):
        z = jnp.dot(h, w_ref[:, width + c0:width + c0 + col_chunk], preferred_element_type=F32)
        v = jax.nn.gelu(z)
        ssq = ssq + jnp.sum(v * v, axis=-1, keepdims=True)
        vf_ref[:, c0:c0 + col_chunk] = v
    inv = lax.rsqrt(ssq / width + EPS)
    v_ref[...] = (vf_ref[...] * inv * vg_ref[...]).astype(BF16)


def _gmlp_in(x, gain, w_in, v_gain, cast_w, cast_col_tile, tm=256):
    n, d = x.shape
    width = w_in.shape[1] // 2
    grid = (n // tm,)
    cast = _CastJob(cast_w, cast_w.size // cast_w.shape[-1] // grid[0], grid, cast_col_tile)
    kern = functools.partial(_gmlp_in_kernel, col_chunk=512, cast=cast)
    u, v, w_bf16 = pl.pallas_call(
        kern,
        grid=grid,
        in_specs=[
            pl.BlockSpec((tm, d), lambda i: (i, 0)),
            _resident((1, d)),
            _resident((d, 2 * width)),
            _resident((1, width)),
            cast.src_spec(),
        ],
        out_specs=[
            pl.BlockSpec((tm, width), lambda i: (i, 0)),
            pl.BlockSpec((tm, width), lambda i: (i, 0)),
            cast.dst_spec(),
        ],
        out_shape=[
            jax.ShapeDtypeStruct((n, width), BF16),
            jax.ShapeDtypeStruct((n, width), BF16),
            cast.out_shape(),
        ],
        scratch_shapes=[pltpu.VMEM((tm, width), F32)],
        compiler_params=_params("arbitrary"),
        name="gmlp_in",
    )(x, gain, w_in, v_gain, cast.src)
    return u, v, cast.result(w_bf16)


def _top2_route(logits):
    lane = lax.broadcasted_iota(jnp.int32, logits.shape, 1)
    lg = jnp.where(lane < N_EXPERTS, logits, -jnp.inf)
    m1 = jnp.max(lg, axis=-1, keepdims=True)
    i1 = jnp.min(jnp.where(lg == m1, lane, LANES), axis=-1, keepdims=True)
    lg2 = jnp.where(lane == i1, -jnp.inf, lg)
    m2 = jnp.max(lg2, axis=-1, keepdims=True)
    i2 = jnp.min(jnp.where(lg2 == m2, lane, LANES), axis=-1, keepdims=True)
    e2 = jnp.exp(m2 - m1)
    w1 = 1.0 / (1.0 + e2)
    w2 = e2 / (1.0 + e2)
    idx = jnp.where(lane == 0, i1, jnp.where(lane == 1, i2, 0))
    gate = jnp.where(lane == 0, w1, jnp.where(lane == 1, w2, 0.0))
    return idx, gate


def _gmlp_out_router_kernel(u_ref, v_ref, ws_ref, bs_ref, wo_ref, r_ref, g_ref, wr_ref,
                            x_ref, h_ref, idx_ref, gate_ref, y_ref):
    tm, width = u_ref.shape
    for c0 in range(0, tm, CHUNK):
        for g in range(width // LANES):
            cols = slice(g * LANES, (g + 1) * LANES)
            vv = v_ref[c0:c0 + CHUNK, cols]
            mixed = jnp.dot(ws_ref[g], vv, preferred_element_type=F32) + bs_ref[:, cols]
            y_ref[c0:c0 + CHUNK, cols] = (u_ref[c0:c0 + CHUNK, cols].astype(F32) * mixed).astype(BF16)
    x = r_ref[...] + jnp.dot(y_ref[...], wo_ref[...], preferred_element_type=F32)
    x_ref[...] = x
    h = _rms(x, g_ref[...])
    h_ref[...] = h
    h_hi = h.astype(BF16)
    h_lo = (h - h_hi.astype(F32)).astype(BF16)
    a = jnp.dot(h_hi, wr_ref[...], preferred_element_type=F32)
    b = jnp.dot(h_lo, wr_ref[:, 0:LANES], preferred_element_type=F32)
    logits = a[:, 0:LANES] + (a[:, LANES:2 * LANES] + b)
    idx_ref[...], gate_ref[...] = _top2_route(logits)


def _gmlp_out_router(u, v, w_s, b_full, w_out, res, ffn_gain, w_router_split, tm=512):
    n, width = u.shape
    d = w_out.shape[1]
    groups = w_s.shape[0]
    row = lambda i: (i, 0)
    return pl.pallas_call(
        _gmlp_out_router_kernel,
        grid=(n // tm,),
        in_specs=[
            pl.BlockSpec((tm, width), row),
            pl.BlockSpec((tm, width), row),
            _resident((groups, CHUNK, CHUNK)),
            _resident((CHUNK, width)),
            _resident((width, d)),
            pl.BlockSpec((tm, d), row),
            _resident((1, d)),
            _resident((d, 2 * LANES)),
        ],
        out_specs=[
            pl.BlockSpec((tm, d), row),
            pl.BlockSpec((tm, d), row),
            pl.BlockSpec((tm, LANES), row),
            pl.BlockSpec((tm, LANES), row),
        ],
        out_shape=[
            jax.ShapeDtypeStruct((n, d), F32),
            jax.ShapeDtypeStruct((n, d), F32),
            jax.ShapeDtypeStruct((n, LANES), jnp.int32),
            jax.ShapeDtypeStruct((n, LANES), F32),
        ],
        scratch_shapes=[pltpu.VMEM((tm, width), BF16)],
        compiler_params=_params("parallel"),
        name="gmlp_out_router",
    )(u, v, w_s, b_full, w_out, res, ffn_gain, w_router_split)


def _moe_ffn_kernel(te_ref, nv_ref, tok_ref, h_hbm, wg_ref, wu_ref, wd_ref, o_ref,
                    xbuf, hb_ref, sems, *, rows_per_step, small_rows):
    i = pl.program_id(0)
    j = pl.program_id(1)
    n_tiles = pl.num_programs(0)
    nf = pl.num_programs(1)
    tm = hb_ref.shape[0]
    rows_buf = xbuf.shape[1]
    slot = i % 2

    def row_copy(tile, r, dst_slot):
        tok = tok_ref[tile * tm + r]
        return pltpu.make_async_copy(h_hbm.at[pl.ds(tok, 1), :], xbuf.at[dst_slot, pl.ds(r, 1), :],
                                     sems.at[dst_slot])

    def wait_slot(s):
        pltpu.make_async_copy(h_hbm.at[pl.ds(0, rows_buf), :], xbuf.at[s], sems.at[s]).wait()

    def prefetch_next_tile():
        for k in range(rows_per_step):
            row_copy(i + 1, j * rows_per_step + k, 1 - slot).start()

    @pl.when((i == 0) & (j == 0))
    def _():
        def issue(r, carry):
            row_copy(0, r, 0).start()
            return carry
        lax.fori_loop(0, rows_buf, issue, 0)

    @pl.when(j == 0)
    def _():
        wait_slot(slot)
        hb_ref[...] = xbuf[slot, 0:tm, :].astype(BF16)
        o_ref[...] = jnp.zeros_like(o_ref)

    def swiglu(rows):
        h = hb_ref[0:rows, :]
        gt = jnp.dot(h, wg_ref[...], preferred_element_type=F32)
        up = jnp.dot(h, wu_ref[...], preferred_element_type=F32)
        act = (gt * jax.nn.sigmoid(gt) * up).astype(BF16)
        o_ref[0:rows, :] += jnp.dot(act, wd_ref[...], preferred_element_type=F32)

    n_valid = nv_ref[i]

    @pl.when(n_valid > small_rows)
    def _():
        prefetch_next_tile()
        swiglu(tm)

    @pl.when((n_valid > 0) & (n_valid <= small_rows))
    def _():
        prefetch_next_tile()
        swiglu(small_rows)

    @pl.when(n_valid == 0)
    def _():
        prefetch_next_tile()

    @pl.when((i == n_tiles - 1) & (j == nf - 1))
    def _():
        wait_slot(1 - slot)


def _moe_ffn(h, tile_expert, tile_valid, row_token, we_gate, we_up, we_down, tm, small_rows=128):
    n, d = h.shape
    nf, tf = we_gate.shape[1], we_gate.shape[3]
    n_tiles = tile_expert.shape[0]
    p = n_tiles * tm
    rows_per_step = -(-tm // (nf * SUBLANES)) * SUBLANES
    rows_buf = rows_per_step * nf
    row_token = jnp.pad(row_token, (0, tm + rows_buf))

    def f_tile(i, j, nv):
        return jnp.where(nv[i] > 0, j, nf - 1)

    def w_in_map(i, j, te, nv, tok):
        return (te[i], f_tile(i, j, nv), 0, 0)

    def w_out_map(i, j, te, nv, tok):
        return (te[i], f_tile(i, j, nv), 0)

    grid_spec = pltpu.PrefetchScalarGridSpec(
        num_scalar_prefetch=3,
        grid=(n_tiles, nf),
        in_specs=[
            pl.BlockSpec(memory_space=pl.ANY),
            pl.BlockSpec((None, None, d, tf), w_in_map),
            pl.BlockSpec((None, None, d, tf), w_in_map),
            pl.BlockSpec((None, tf, d), w_out_map),
        ],
        out_specs=pl.BlockSpec((tm, d), lambda i, j, te, nv, tok: (i, 0)),
        scratch_shapes=[
            pltpu.VMEM((2, rows_buf, d), F32),
            pltpu.VMEM((tm, d), BF16),
            pltpu.SemaphoreType.DMA((2,)),
        ],
    )
    return pl.pallas_call(
        functools.partial(_moe_ffn_kernel, rows_per_step=rows_per_step, small_rows=small_rows),
        grid_spec=grid_spec,
        out_shape=jax.ShapeDtypeStruct((p, d), F32),
        compiler_params=_params("arbitrary", "arbitrary"),
        name="moe_ffn",
    )(tile_expert, tile_valid, row_token, h, we_gate, we_up, we_down)


def _moe_combine_kernel(pos_ref, x_ref, gate_ref, y_hbm, o_ref, buf, sems):
    i = pl.program_id(0)
    tm = x_ref.shape[0]
    slot = i % 2

    def issue_tile(tile, dst_slot):
        def issue(r, carry):
            for k in range(TOP_K):
                p = pos_ref[(tile * tm + r) * TOP_K + k]
                pltpu.make_async_copy(y_hbm.at[pl.ds(p, 1), :], buf.at[dst_slot, k, pl.ds(r, 1), :],
                                      sems.at[dst_slot]).start()
            return carry
        lax.fori_loop(0, tm, issue, 0, unroll=8)

    @pl.when(i == 0)
    def _():
        issue_tile(0, 0)

    @pl.when(i + 1 < pl.num_programs(0))
    def _():
        issue_tile(i + 1, 1 - slot)

    for k in range(TOP_K):
        pltpu.make_async_copy(y_hbm.at[pl.ds(0, tm), :], buf.at[slot, k], sems.at[slot]).wait()
    gate = gate_ref[...]
    o_ref[...] = x_ref[...] + (gate[:, 0:1] * buf[slot, 0] + gate[:, 1:2] * buf[slot, 1])


def _moe_combine(x, gate, y_sorted, pos, tm=256):
    n, d = x.shape
    grid_spec = pltpu.PrefetchScalarGridSpec(
        num_scalar_prefetch=1,
        grid=(n // tm,),
        in_specs=[
            pl.BlockSpec((tm, d), lambda i, pos: (i, 0)),
            pl.BlockSpec((tm, LANES), lambda i, pos: (i, 0)),
            pl.BlockSpec(memory_space=pl.ANY),
        ],
        out_specs=pl.BlockSpec((tm, d), lambda i, pos: (i, 0)),
        scratch_shapes=[
            pltpu.VMEM((2, TOP_K, tm, d), F32),
            pltpu.SemaphoreType.DMA((2,)),
        ],
    )
    return pl.pallas_call(
        _moe_combine_kernel,
        grid_spec=grid_spec,
        out_shape=jax.ShapeDtypeStruct((n, d), F32),
        compiler_params=_params("arbitrary"),
        name="moe_combine",
    )(pos, x, gate, y_sorted)


def _route_plan(top_idx, tm):
    n = top_idx.shape[0]
    n_pairs = n * TOP_K
    n_tiles = (n_pairs + N_EXPERTS * (tm - 1)) // tm
    e_flat = top_idx.reshape(n_pairs)
    onehot = (e_flat[:, None] == jnp.arange(N_EXPERTS, dtype=jnp.int32)[None, :]).astype(jnp.int32)
    csum = jnp.cumsum(onehot, axis=0)
    rank = jnp.sum(onehot * (csum - 1), axis=1)
    counts = csum[-1]
    tiles_per = (counts + tm - 1) // tm
    tile_end = jnp.cumsum(tiles_per)
    row_start = (tile_end - tiles_per) * tm
    pos = (row_start[e_flat] + rank).astype(jnp.int32)
    tile_ids = jnp.arange(n_tiles, dtype=jnp.int32)
    tile_expert = jnp.sum((tile_end[None, :] <= tile_ids[:, None]).astype(jnp.int32), axis=1)
    tile_expert = jnp.minimum(tile_expert, N_EXPERTS - 1).astype(jnp.int32)
    in_expert = tile_ids - (tile_end - tiles_per)[tile_expert]
    tile_valid = jnp.clip(counts[tile_expert] - in_expert * tm, 0, tm)
    tile_valid = jnp.where(tile_ids < tile_end[-1], tile_valid, 0).astype(jnp.int32)
    row_token = jnp.zeros((n_tiles * tm,), jnp.int32).at[pos].set(
        jnp.arange(n_pairs, dtype=jnp.int32) // TOP_K)
    return tile_expert, tile_valid, row_token, pos


def kernel(x, l0_mix_norm, l0_w_qkv, l0_q_norm, l0_k_norm, l0_sink, l0_w_o, l0_ffn_norm, l0_w_gate_up, l0_w_down, l1_mix_norm, l1_w_in, l1_v_norm, l1_w_s, l1_b_s, l1_w_out, l1_ffn_norm, l1_w_router, l1_we_gate, l1_we_up, l1_we_down):
    batch, seq, d = x.shape
    n = batch * seq
    x0 = x.reshape(n, d)
    dq = N_HEADS * HEAD_DIM
    dkv = N_KV_HEADS * HEAD_DIM

    head_gain = jnp.concatenate([
        jnp.tile(l0_q_norm * (HEAD_DIM ** -0.5), N_HEADS),
        jnp.tile(l0_k_norm, N_KV_HEADS),
        jnp.ones((dkv,), F32),
    ])[None, :]
    qkv = _qkv_proj(x0, l0_mix_norm[None, :], l0_w_qkv.astype(BF16), head_gain, dq + dkv)
    attn, we_gate = _attention(qkv, l0_sink, batch, seq, l1_we_gate, MOE_TF)
    x1 = _proj_residual(attn, l0_w_o.astype(BF16), x0, name="attn_out_proj")

    x2, we_down = _dense_ffn(x1, l0_ffn_norm[None, :], l0_w_gate_up.astype(BF16), l0_w_down.astype(BF16),
                             l1_we_down)

    u, v, we_up = _gmlp_in(x2, l1_mix_norm[None, :], l1_w_in.astype(BF16), l1_v_norm[None, :], l1_we_up,
                           MOE_TF)
    b_full = jnp.repeat(l1_b_s.T, LANES, axis=1)
    w_router = jnp.pad(l1_w_router, ((0, 0), (0, LANES - N_EXPERTS)))
    w_router_hi = w_router.astype(BF16)
    w_router_lo = (w_router - w_router_hi.astype(F32)).astype(BF16)
    w_router_split = jnp.concatenate([w_router_hi, w_router_lo], axis=1)
    x3, h, ridx, rgate = _gmlp_out_router(u, v, l1_w_s.astype(BF16), b_full, l1_w_out.astype(BF16), x2,
                                          l1_ffn_norm[None, :], w_router_split)

    tm = 512
    tile_expert, tile_valid, row_token, pos = _route_plan(ridx[:, :TOP_K], tm)
    y_sorted = _moe_ffn(h, tile_expert, tile_valid, row_token, we_gate, we_up, we_down, tm)
    out = _moe_combine(x3, rgate, y_sorted, pos)
    return out.reshape(batch, seq, d)
```

```python
import functools

import jax
import jax.numpy as jnp
from jax import lax
from jax.experimental import pallas as pl
from jax.experimental.pallas import tpu as pltpu

F32 = jnp.float32
BF16 = jnp.bfloat16

EPS = 1e-6
NEG_INF = -1e30
LANES = 128
SUBLANES = 8
HEAD_DIM = 128
N_HEADS = 16
N_KV_HEADS = 4
WINDOW = 128
BLOCK = 128
CHUNK = 128
N_EXPERTS = 8
TOP_K = 2
MOE_TF = 1024
DENSE_TF = 512

MIB = 1024 * 1024
VMEM_LIMIT = 56 * MIB


def _params(*sem):
    return pltpu.CompilerParams(dimension_semantics=sem, vmem_limit_bytes=VMEM_LIMIT)


def _resident(shape):
    zeros = (0,) * len(shape)
    return pl.BlockSpec(shape, lambda *_: zeros, pipeline_mode=pl.Buffered(1))


class _CastJob:
    def __init__(self, w, grid, col_tile=None, block_rows=None):
        self.stacked = w.ndim == 3
        self.e, self.k, self.f = w.shape if self.stacked else (1,) + w.shape
        self.src = w.reshape(self.e * self.k, self.f)
        steps = 1
        for g in grid:
            steps *= g
        if block_rows is None:
            assert (self.e * self.k) % steps == 0
            block_rows = self.e * self.k // steps
        assert (self.e * self.k) % block_rows == 0 and self.k % block_rows == 0
        self.block_rows = block_rows
        self.n_blocks = self.e * self.k // block_rows
        assert self.n_blocks <= steps, (self.n_blocks, steps)
        self.always = self.n_blocks == steps
        strides = [1] * len(grid)
        for a in range(len(grid) - 2, -1, -1):
            strides[a] = strides[a + 1] * grid[a + 1]
        self.strides = tuple(strides)
        self.col_tile = col_tile
        assert col_tile is None or self.f % col_tile == 0

    def _step(self, ids):
        return sum(i * s for i, s in zip(ids, self.strides))

    def _block(self, ids):
        return jnp.minimum(self._step(ids[:len(self.strides)]), self.n_blocks - 1)

    def src_spec(self):
        return pl.BlockSpec((self.block_rows, self.f), lambda *ids: (self._block(ids), 0))

    def dst_spec(self):
        if self.col_tile is None:
            return self.src_spec()
        per_expert = self.k // self.block_rows
        return pl.BlockSpec(
            (None, self.f // self.col_tile, self.block_rows, self.col_tile),
            lambda *ids: (self._block(ids) // per_expert, 0, self._block(ids) % per_expert, 0))

    def out_shape(self):
        if self.col_tile is None:
            return jax.ShapeDtypeStruct(self.src.shape, BF16)
        return jax.ShapeDtypeStruct((self.e, self.f // self.col_tile, self.k, self.col_tile), BF16)

    def _cast_block(self, src_ref, dst_ref):
        if self.col_tile is None:
            dst_ref[...] = src_ref[...].astype(BF16)
        else:
            for c in range(self.f // self.col_tile):
                dst_ref[c] = src_ref[:, c * self.col_tile:(c + 1) * self.col_tile].astype(BF16)

    def run(self, src_ref, dst_ref):
        if self.always:
            self._cast_block(src_ref, dst_ref)
        else:
            ids = [pl.program_id(a) for a in range(len(self.strides))]

            @pl.when(self._step(ids) < self.n_blocks)
            def _():
                self._cast_block(src_ref, dst_ref)

    def result(self, dst):
        out = dst if self.col_tile is not None else dst.reshape(self.e, self.k, self.f)
        return out if self.stacked else out[0]


def _pallas_with_casts(kernel_fn, *, grid, in_specs, out_specs, out_shape, operands, casts=(),
                       scratch_shapes=(), name):
    jobs = [_CastJob(w, grid, col_tile, block_rows) for (w, col_tile, block_rows) in casts]
    n_in, n_out, n_jobs = len(in_specs), len(out_specs), len(jobs)

    def body(*refs):
        ins, srcs = refs[:n_in], refs[n_in:n_in + n_jobs]
        outs = refs[n_in + n_jobs:n_in + n_jobs + n_out]
        dsts = refs[n_in + n_jobs + n_out:n_in + 2 * n_jobs + n_out]
        for job, src, dst in zip(jobs, srcs, dsts):
            job.run(src, dst)
        kernel_fn(*ins, *outs, *refs[n_in + 2 * n_jobs + n_out:])

    results = pl.pallas_call(
        body,
        grid=grid,
        in_specs=list(in_specs) + [job.src_spec() for job in jobs],
        out_specs=list(out_specs) + [job.dst_spec() for job in jobs],
        out_shape=list(out_shape) + [job.out_shape() for job in jobs],
        scratch_shapes=list(scratch_shapes),
        compiler_params=_params(*(["arbitrary"] * len(grid))),
        name=name,
    )(*operands, *[job.src for job in jobs])
    return results[:n_out], [job.result(r) for job, r in zip(jobs, results[n_out:])]


def _rms(x, gain):
    ms = jnp.mean(x * x, axis=-1, keepdims=True)
    return x * lax.rsqrt(ms + EPS) * gain


def _qkv_kernel(x_ref, g_ref, w_ref, hg_ref, o_ref, *, n_norm_cols, col_chunk):
    h = _rms(x_ref[...], g_ref[...]).astype(BF16)
    n_out = o_ref.shape[1]
    for c0 in range(0, n_out, col_chunk):
        y = jnp.dot(h, w_ref[:, c0:c0 + col_chunk], preferred_element_type=F32)
        for h0 in range(0, col_chunk, HEAD_DIM):
            col = c0 + h0
            yh = y[:, h0:h0 + HEAD_DIM]
            if col < n_norm_cols:
                yh = _rms(yh, hg_ref[:, col:col + HEAD_DIM])
            o_ref[:, col:col + HEAD_DIM] = yh.astype(BF16)


def _qkv_proj(x, gain, w, head_gain, n_norm_cols, casts, tm=512):
    n, d = x.shape
    nq = w.shape[1]
    kern = functools.partial(_qkv_kernel, n_norm_cols=n_norm_cols, col_chunk=512)
    (qkv,), cast_out = _pallas_with_casts(
        kern,
        grid=(n // tm,),
        in_specs=[
            pl.BlockSpec((tm, d), lambda i: (i, 0)),
            _resident((1, d)),
            _resident((d, nq)),
            _resident((1, nq)),
        ],
        out_specs=[pl.BlockSpec((tm, nq), lambda i: (i, 0))],
        out_shape=[jax.ShapeDtypeStruct((n, nq), BF16)],
        operands=(x, gain, w, head_gain),
        casts=casts,
        name="qkv_proj",
    )
    return qkv, cast_out


def _attn_kernel(sink_ref, q_ref, kp_ref, kc_ref, kn_ref, vp_ref, vc_ref, vn_ref, bias_ref, o_ref):
    kcat = jnp.concatenate([kp_ref[...], kc_ref[...], kn_ref[...]], axis=0)
    vcat = jnp.concatenate([vp_ref[...], vc_ref[...], vn_ref[...]], axis=0)
    group = N_HEADS // N_KV_HEADS
    band = 3 * BLOCK
    scores = []
    for kh in range(N_KV_HEADS):
        k_h = kcat[:, kh * HEAD_DIM:(kh + 1) * HEAD_DIM]
        heads = [kh * group + g for g in range(group)]
        q_g = jnp.concatenate([q_ref[:, hd * HEAD_DIM:(hd + 1) * HEAD_DIM] for hd in heads], axis=0)
        scores.append(lax.dot_general(k_h, q_g, (((1,), (1,)), ((), ())), preferred_element_type=F32))
    for kh in range(N_KV_HEADS):
        v_h = vcat[:, kh * HEAD_DIM:(kh + 1) * HEAD_DIM]
        heads = [kh * group + g for g in range(group)]
        sink = jnp.concatenate([jnp.full((1, BLOCK), sink_ref[hd], F32) for hd in heads], axis=1)
        s = scores[kh] + bias_ref[kh * band:(kh + 1) * band, :]
        m = jnp.maximum(jnp.max(s, axis=0, keepdims=True), sink)
        p = jnp.exp(s - m)
        denom = jnp.sum(p, axis=0, keepdims=True) + jnp.exp(sink - m)
        o_t = lax.dot_general(v_h, p.astype(BF16), (((0,), (0,)), ((), ())),
                              preferred_element_type=F32) * (1.0 / denom)
        for g, hd in enumerate(heads):
            o_ref[:, hd * HEAD_DIM:(hd + 1) * HEAD_DIM] = o_t[:, g * BLOCK:(g + 1) * BLOCK].T.astype(BF16)


def _attention_bias():
    group = N_HEADS // N_KV_HEADS
    qi = jnp.arange(BLOCK)[:, None]
    sj = jnp.arange(3 * BLOCK)[None, :]
    dist = jnp.abs(qi - sj + BLOCK)
    slopes = jnp.exp2(-8.0 * jnp.arange(1, N_HEADS + 1, dtype=F32) / N_HEADS)
    bias = jnp.where(dist <= WINDOW, -slopes[:, None, None] * dist.astype(F32), NEG_INF)
    first = (sj >= BLOCK)[None]
    last = (sj < 2 * BLOCK)[None]
    edge = jnp.stack([jnp.where(first, bias, NEG_INF), bias, jnp.where(last, bias, NEG_INF)])
    edge = edge.reshape(3, N_KV_HEADS, group, BLOCK, 3 * BLOCK).transpose(0, 1, 4, 2, 3)
    return edge.reshape(3, N_KV_HEADS * 3 * BLOCK, group * BLOCK)


def _attention(qkv, sink, batch, seq, casts):
    n = qkv.shape[0]
    nb = seq // BLOCK
    dq = N_HEADS * HEAD_DIM
    dkv = N_KV_HEADS * HEAD_DIM
    k_col = dq // dkv
    v_col = k_col + 1

    def prev_blk(b, j):
        return b * nb + jnp.maximum(j - 1, 0)

    def next_blk(b, j):
        return b * nb + jnp.minimum(j + 1, nb - 1)

    assert nb >= 2, "first and last query block must differ"
    group = N_HEADS // N_KV_HEADS

    def bias_variant(b, j):
        return (jnp.where(j == 0, 0, jnp.where(j == nb - 1, 2, 1)), 0, 0)

    (attn,), cast_out = _pallas_with_casts(
        _attn_kernel,
        grid=(batch, nb),
        in_specs=[
            pl.BlockSpec(memory_space=pltpu.SMEM),
            pl.BlockSpec((BLOCK, dq), lambda b, j: (b * nb + j, 0)),
            pl.BlockSpec((BLOCK, dkv), lambda b, j: (prev_blk(b, j), k_col)),
            pl.BlockSpec((BLOCK, dkv), lambda b, j: (b * nb + j, k_col)),
            pl.BlockSpec((BLOCK, dkv), lambda b, j: (next_blk(b, j), k_col)),
            pl.BlockSpec((BLOCK, dkv), lambda b, j: (prev_blk(b, j), v_col)),
            pl.BlockSpec((BLOCK, dkv), lambda b, j: (b * nb + j, v_col)),
            pl.BlockSpec((BLOCK, dkv), lambda b, j: (next_blk(b, j), v_col)),
            pl.BlockSpec((None, N_KV_HEADS * 3 * BLOCK, group * BLOCK), bias_variant),
        ],
        out_specs=[pl.BlockSpec((BLOCK, dq), lambda b, j: (b * nb + j, 0))],
        out_shape=[jax.ShapeDtypeStruct((n, dq), BF16)],
        operands=(sink, qkv, qkv, qkv, qkv, qkv, qkv, qkv, _attention_bias()),
        casts=casts,
        name="window_attention",
    )
    return attn, cast_out


def _proj_res_kernel(a_ref, w_ref, r_ref, o_ref):
    o_ref[...] = r_ref[...] + jnp.dot(a_ref[...], w_ref[...], preferred_element_type=F32)


def _proj_residual(a, w, res, casts, tm=512, name="proj_residual"):
    n, k = a.shape
    d = w.shape[1]
    (out,), cast_out = _pallas_with_casts(
        _proj_res_kernel,
        grid=(n // tm,),
        in_specs=[
            pl.BlockSpec((tm, k), lambda i: (i, 0)),
            _resident((k, d)),
            pl.BlockSpec((tm, d), lambda i: (i, 0)),
        ],
        out_specs=[pl.BlockSpec((tm, d), lambda i: (i, 0))],
        out_shape=[jax.ShapeDtypeStruct((n, d), F32)],
        operands=(a, w, res),
        casts=casts,
        name=name,
    )
    return out, cast_out


def _dense_ffn_kernel(x_ref, g_ref, wg_ref, wu_ref, wd_ref, o_ref, h_ref):
    @pl.when(pl.program_id(1) == 0)
    def _():
        x = x_ref[...]
        h_ref[...] = _rms(x, g_ref[...]).astype(BF16)
        o_ref[...] = x

    h = h_ref[...]
    gt = jnp.dot(h, wg_ref[...], preferred_element_type=F32)
    up = jnp.dot(h, wu_ref[...], preferred_element_type=F32)
    act = (gt * jax.nn.sigmoid(gt) * up).astype(BF16)
    o_ref[...] += jnp.dot(act, wd_ref[...], preferred_element_type=F32)


def _dense_ffn(x, gain, w_gate_up, w_down, casts, tm=512):
    n, d = x.shape
    f = w_down.shape[0]
    tf = w_gate_up.shape[2]
    nf = f // tf
    (out,), cast_out = _pallas_with_casts(
        _dense_ffn_kernel,
        grid=(n // tm, nf),
        in_specs=[
            pl.BlockSpec((tm, d), lambda i, j: (i, 0)),
            _resident((1, d)),
            pl.BlockSpec((None, d, tf), lambda i, j: (j, 0, 0)),
            pl.BlockSpec((None, d, tf), lambda i, j: (nf + j, 0, 0)),
            pl.BlockSpec((tf, d), lambda i, j: (j, 0)),
        ],
        out_specs=[pl.BlockSpec((tm, d), lambda i, j: (i, 0))],
        out_shape=[jax.ShapeDtypeStruct((n, d), F32)],
        operands=(x, gain, w_gate_up, w_gate_up, w_down),
        casts=casts,
        scratch_shapes=[pltpu.VMEM((tm, d), BF16)],
        name="dense_ffn",
    )
    return out, cast_out


def _gmlp_in_kernel(x_ref, g_ref, w_ref, vg_ref, u_ref, v_ref, vf_ref, *, col_chunk):
    h = _rms(x_ref[...], g_ref[...]).astype(BF16)
    width = u_ref.shape[1]
    for c0 in range(0, width, col_chunk):
        z = jnp.dot(h, w_ref[:, c0:c0 + col_chunk], preferred_element_type=F32)
        u_ref[:, c0:c0 + col_chunk] = jax.nn.gelu(z).astype(BF16)
    ssq = jnp.zeros((x_ref.shape[0], 1), F32)
    for c0 in range(0, width, col_chunk):
        z = jnp.dot(h, w_ref[:, width + c0:width + c0 + col_chunk], preferred_element_type=F32)
        v = jax.nn.gelu(z)
        ssq = ssq + jnp.sum(v * v, axis=-1, keepdims=True)
        vf_ref[:, c0:c0 + col_chunk] = v
    inv = lax.rsqrt(ssq / width + EPS)
    v_ref[...] = (vf_ref[...] * inv * vg_ref[...]).astype(BF16)


def _gmlp_in(x, gain, w_in, v_gain, casts, tm=256):
    n, d = x.shape
    width = w_in.shape[1] // 2
    kern = functools.partial(_gmlp_in_kernel, col_chunk=2048)
    (u, v), cast_out = _pallas_with_casts(
        kern,
        grid=(n // tm,),
        in_specs=[
            pl.BlockSpec((tm, d), lambda i: (i, 0)),
            _resident((1, d)),
            _resident((d, 2 * width)),
            _resident((1, width)),
        ],
        out_specs=[
            pl.BlockSpec((tm, width), lambda i: (i, 0)),
            pl.BlockSpec((tm, width), lambda i: (i, 0)),
        ],
        out_shape=[
            jax.ShapeDtypeStruct((n, width), BF16),
            jax.ShapeDtypeStruct((n, width), BF16),
        ],
        operands=(x, gain, w_in, v_gain),
        casts=casts,
        scratch_shapes=[pltpu.VMEM((tm, width), F32)],
        name="gmlp_in",
    )
    return u, v, cast_out


def _top2_route(logits):
    lane = lax.broadcasted_iota(jnp.int32, logits.shape, 1)
    lg = jnp.where(lane < N_EXPERTS, logits, -jnp.inf)
    m1 = jnp.max(lg, axis=-1, keepdims=True)
    i1 = jnp.min(jnp.where(lg == m1, lane, LANES), axis=-1, keepdims=True)
    lg2 = jnp.where(lane == i1, -jnp.inf, lg)
    m2 = jnp.max(lg2, axis=-1, keepdims=True)
    i2 = jnp.min(jnp.where(lg2 == m2, lane, LANES), axis=-1, keepdims=True)
    e2 = jnp.exp(m2 - m1)
    w1 = 1.0 / (1.0 + e2)
    w2 = e2 / (1.0 + e2)
    idx = jnp.where(lane == 0, i1, jnp.where(lane == 1, i2, 0))
    gate = jnp.where(lane == 0, w1, jnp.where(lane == 1, w2, 0.0))
    return idx, gate


def _gmlp_out_router_kernel(u_ref, v_ref, ws_ref, bs_ref, wo_ref, r_ref, g_ref, wr_ref,
                            x_ref, h_ref, idx_ref, gate_ref, y_ref):
    tm, width = u_ref.shape
    for c0 in range(0, tm, CHUNK):
        for g in range(width // LANES):
            cols = slice(g * LANES, (g + 1) * LANES)
            vv = v_ref[c0:c0 + CHUNK, cols]
            mixed = jnp.dot(ws_ref[g], vv, preferred_element_type=F32) + bs_ref[:, cols]
            y_ref[c0:c0 + CHUNK, cols] = (u_ref[c0:c0 + CHUNK, cols].astype(F32) * mixed).astype(BF16)
    x = r_ref[...] + jnp.dot(y_ref[...], wo_ref[...], preferred_element_type=F32)
    x_ref[...] = x
    h = _rms(x, g_ref[...])
    h_ref[...] = h
    h_hi = h.astype(BF16)
    h_lo = (h - h_hi.astype(F32)).astype(BF16)
    a = jnp.dot(h_hi, wr_ref[...], preferred_element_type=F32)
    b = jnp.dot(h_lo, wr_ref[:, 0:LANES], preferred_element_type=F32)
    logits = a[:, 0:LANES] + (a[:, LANES:2 * LANES] + b)
    idx_ref[...], gate_ref[...] = _top2_route(logits)


def _gmlp_out_router(u, v, w_s, b_full, w_out, res, ffn_gain, w_router_split, tm=512):
    n, width = u.shape
    d = w_out.shape[1]
    groups = w_s.shape[0]
    row = lambda i: (i, 0)
    return pl.pallas_call(
        _gmlp_out_router_kernel,
        grid=(n // tm,),
        in_specs=[
            pl.BlockSpec((tm, width), row),
            pl.BlockSpec((tm, width), row),
            _resident((groups, CHUNK, CHUNK)),
            _resident((CHUNK, width)),
            _resident((width, d)),
            pl.BlockSpec((tm, d), row),
            _resident((1, d)),
            _resident((d, 2 * LANES)),
        ],
        out_specs=[
            pl.BlockSpec((tm, d), row),
            pl.BlockSpec((tm, d), row),
            pl.BlockSpec((tm, LANES), row),
            pl.BlockSpec((tm, LANES), row),
        ],
        out_shape=[
            jax.ShapeDtypeStruct((n, d), F32),
            jax.ShapeDtypeStruct((n, d), F32),
            jax.ShapeDtypeStruct((n, LANES), jnp.int32),
            jax.ShapeDtypeStruct((n, LANES), F32),
        ],
        scratch_shapes=[pltpu.VMEM((tm, width), BF16)],
        compiler_params=_params("parallel"),
        name="gmlp_out_router",
    )(u, v, w_s, b_full, w_out, res, ffn_gain, w_router_split)


def _moe_ffn_kernel(te_ref, nv_ref, tok_ref, h_hbm, wg_ref, wu_ref, wd_ref, o_ref,
                    xbuf, hb_ref, sems, *, rows_per_step, small_rows):
    i = pl.program_id(0)
    j = pl.program_id(1)
    n_tiles = pl.num_programs(0)
    nf = pl.num_programs(1)
    tm = hb_ref.shape[0]
    rows_buf = xbuf.shape[1]
    slot = i % 2

    def row_copy(tile, r, dst_slot):
        tok = tok_ref[tile * tm + r]
        return pltpu.make_async_copy(h_hbm.at[pl.ds(tok, 1), :], xbuf.at[dst_slot, pl.ds(r, 1), :],
                                     sems.at[dst_slot])

    def wait_slot(s):
        pltpu.make_async_copy(h_hbm.at[pl.ds(0, rows_buf), :], xbuf.at[s], sems.at[s]).wait()

    def prefetch_next_tile():
        for k in range(rows_per_step):
            row_copy(i + 1, j * rows_per_step + k, 1 - slot).start()

    @pl.when((i == 0) & (j == 0))
    def _():
        def issue(r, carry):
            row_copy(0, r, 0).start()
            return carry
        lax.fori_loop(0, rows_buf, issue, 0)

    @pl.when(j == 0)
    def _():
        wait_slot(slot)
        hb_ref[...] = xbuf[slot, 0:tm, :].astype(BF16)
        o_ref[...] = jnp.zeros_like(o_ref)

    def swiglu(rows):
        h = hb_ref[0:rows, :]
        gt = jnp.dot(h, wg_ref[...], preferred_element_type=F32)
        up = jnp.dot(h, wu_ref[...], preferred_element_type=F32)
        act = (gt * jax.nn.sigmoid(gt) * up).astype(BF16)
        o_ref[0:rows, :] += jnp.dot(act, wd_ref[...], preferred_element_type=F32)

    n_valid = nv_ref[i]

    @pl.when(n_valid > small_rows)
    def _():
        prefetch_next_tile()
        swiglu(tm)

    @pl.when((n_valid > 0) & (n_valid <= small_rows))
    def _():
        prefetch_next_tile()
        swiglu(small_rows)

    @pl.when(n_valid == 0)
    def _():
        prefetch_next_tile()

    @pl.when((i == n_tiles - 1) & (j == nf - 1))
    def _():
        wait_slot(1 - slot)


def _moe_ffn(h, tile_expert, tile_valid, row_token, we_gate, we_up, we_down, tm, small_rows=128):
    n, d = h.shape
    nf, tf = we_gate.shape[1], we_gate.shape[3]
    n_tiles = tile_expert.shape[0]
    p = n_tiles * tm
    rows_per_step = -(-tm // (nf * SUBLANES)) * SUBLANES
    rows_buf = rows_per_step * nf
    row_token = jnp.pad(row_token, (0, tm + rows_buf))

    def f_tile(i, j, nv):
        return jnp.where(nv[i] > 0, j, nf - 1)

    def w_in_map(i, j, te, nv, tok):
        return (te[i], f_tile(i, j, nv), 0, 0)

    def w_out_map(i, j, te, nv, tok):
        return (te[i], f_tile(i, j, nv), 0)

    grid_spec = pltpu.PrefetchScalarGridSpec(
        num_scalar_prefetch=3,
        grid=(n_tiles, nf),
        in_specs=[
            pl.BlockSpec(memory_space=pl.ANY),
            pl.BlockSpec((None, None, d, tf), w_in_map),
            pl.BlockSpec((None, None, d, tf), w_in_map),
            pl.BlockSpec((None, tf, d), w_out_map),
        ],
        out_specs=pl.BlockSpec((tm, d), lambda i, j, te, nv, tok: (i, 0)),
        scratch_shapes=[
            pltpu.VMEM((2, rows_buf, d), F32),
            pltpu.VMEM((tm, d), BF16),
            pltpu.SemaphoreType.DMA((2,)),
        ],
    )
    return pl.pallas_call(
        functools.partial(_moe_ffn_kernel, rows_per_step=rows_per_step, small_rows=small_rows),
        grid_spec=grid_spec,
        out_shape=jax.ShapeDtypeStruct((p, d), F32),
        compiler_params=_params("arbitrary", "arbitrary"),
        name="moe_ffn",
    )(tile_expert, tile_valid, row_token, h, we_gate, we_up, we_down)


def _moe_combine_kernel(pos_ref, x_ref, gate_ref, y_hbm, o_ref, buf, sems):
    i = pl.program_id(0)
    tm = x_ref.shape[0]
    slot = i % 2

    def issue_tile(tile, dst_slot):
        def issue(r, carry):
            for k in range(TOP_K):
                p = pos_ref[(tile * tm + r) * TOP_K + k]
                pltpu.make_async_copy(y_hbm.at[pl.ds(p, 1), :], buf.at[dst_slot, k, pl.ds(r, 1), :],
                                      sems.at[dst_slot]).start()
            return carry
        lax.fori_loop(0, tm, issue, 0, unroll=8)

    @pl.when(i == 0)
    def _():
        issue_tile(0, 0)

    @pl.when(i + 1 < pl.num_programs(0))
    def _():
        issue_tile(i + 1, 1 - slot)

    for k in range(TOP_K):
        pltpu.make_async_copy(y_hbm.at[pl.ds(0, tm), :], buf.at[slot, k], sems.at[slot]).wait()
    gate = gate_ref[...]
    o_ref[...] = x_ref[...] + (gate[:, 0:1] * buf[slot, 0] + gate[:, 1:2] * buf[slot, 1])


def _moe_combine(x, gate, y_sorted, pos, tm=256):
    n, d = x.shape
    grid_spec = pltpu.PrefetchScalarGridSpec(
        num_scalar_prefetch=1,
        grid=(n // tm,),
        in_specs=[
            pl.BlockSpec((tm, d), lambda i, pos: (i, 0)),
            pl.BlockSpec((tm, LANES), lambda i, pos: (i, 0)),
            pl.BlockSpec(memory_space=pl.ANY),
        ],
        out_specs=pl.BlockSpec((tm, d), lambda i, pos: (i, 0)),
        scratch_shapes=[
            pltpu.VMEM((2, TOP_K, tm, d), F32),
            pltpu.SemaphoreType.DMA((2,)),
        ],
    )
    return pl.pallas_call(
        _moe_combine_kernel,
        grid_spec=grid_spec,
        out_shape=jax.ShapeDtypeStruct((n, d), F32),
        compiler_params=_params("arbitrary"),
        name="moe_combine",
    )(pos, x, gate, y_sorted)


def _route_plan(top_idx, tm):
    n = top_idx.shape[0]
    n_pairs = n * TOP_K
    n_tiles = (n_pairs + N_EXPERTS * (tm - 1)) // tm
    e_flat = top_idx.reshape(n_pairs)
    onehot = (e_flat[:, None] == jnp.arange(N_EXPERTS, dtype=jnp.int32)[None, :]).astype(jnp.int32)
    csum = jnp.cumsum(onehot, axis=0)
    rank = jnp.sum(onehot * (csum - 1), axis=1)
    counts = csum[-1]
    tiles_per = (counts + tm - 1) // tm
    tile_end = jnp.cumsum(tiles_per)
    row_start = (tile_end - tiles_per) * tm
    pos = (row_start[e_flat] + rank).astype(jnp.int32)
    tile_ids = jnp.arange(n_tiles, dtype=jnp.int32)
    tile_expert = jnp.sum((tile_end[None, :] <= tile_ids[:, None]).astype(jnp.int32), axis=1)
    tile_expert = jnp.minimum(tile_expert, N_EXPERTS - 1).astype(jnp.int32)
    in_expert = tile_ids - (tile_end - tiles_per)[tile_expert]
    tile_valid = jnp.clip(counts[tile_expert] - in_expert * tm, 0, tm)
    tile_valid = jnp.where(tile_ids < tile_end[-1], tile_valid, 0).astype(jnp.int32)
    row_token = jnp.zeros((n_tiles * tm,), jnp.int32).at[pos].set(
        jnp.arange(n_pairs, dtype=jnp.int32) // TOP_K)
    return tile_expert, tile_valid, row_token, pos


def kernel(x, l0_mix_norm, l0_w_qkv, l0_q_norm, l0_k_norm, l0_sink, l0_w_o, l0_ffn_norm, l0_w_gate_up, l0_w_down, l1_mix_norm, l1_w_in, l1_v_norm, l1_w_s, l1_b_s, l1_w_out, l1_ffn_norm, l1_w_router, l1_we_gate, l1_we_up, l1_we_down):
    batch, seq, d = x.shape
    n = batch * seq
    x0 = x.reshape(n, d)
    dq = N_HEADS * HEAD_DIM
    dkv = N_KV_HEADS * HEAD_DIM

    head_gain = jnp.concatenate([
        jnp.tile(l0_q_norm * (HEAD_DIM ** -0.5), N_HEADS),
        jnp.tile(l0_k_norm, N_KV_HEADS),
        jnp.ones((dkv,), F32),
    ])[None, :]
    qkv, (w_gate_up, w_o, w_down) = _qkv_proj(
        x0, l0_mix_norm[None, :], l0_w_qkv.astype(BF16), head_gain, dq + dkv,
        casts=[(l0_w_gate_up, DENSE_TF, None), (l0_w_o, None, None), (l0_w_down, None, None)])
    attn, (we_gate,) = _attention(qkv, l0_sink, batch, seq, casts=[(l1_we_gate, MOE_TF, None)])
    x1, _ = _proj_residual(attn, w_o, x0, casts=[], name="attn_out_proj")

    x2, (we_down, w_in, w_out) = _dense_ffn(
        x1, l0_ffn_norm[None, :], w_gate_up, w_down,
        casts=[(l1_we_down, None, 256), (l1_w_in, None, 16), (l1_w_out, None, 16)])

    u, v, (we_up,) = _gmlp_in(x2, l1_mix_norm[None, :], w_in, l1_v_norm[None, :],
                              casts=[(l1_we_up, MOE_TF, None)])
    b_full = jnp.repeat(l1_b_s.T, LANES, axis=1)
    w_router = jnp.pad(l1_w_router, ((0, 0), (0, LANES - N_EXPERTS)))
    w_router_hi = w_router.astype(BF16)
    w_router_lo = (w_router - w_router_hi.astype(F32)).astype(BF16)
    w_router_split = jnp.concatenate([w_router_hi, w_router_lo], axis=1)
    x3, h, ridx, rgate = _gmlp_out_router(u, v, l1_w_s.astype(BF16), b_full, w_out, x2,
                                          l1_ffn_norm[None, :], w_router_split)

    tm = 512
    tile_expert, tile_valid, row_token, pos = _route_plan(ridx[:, :TOP_K], tm)
    y_sorted = _moe_ffn(h, tile_expert, tile_valid, row_token, we_gate, we_up, we_down, tm)
    out = _moe_combine(x3, rgate, y_sorted, pos)
    return out.reshape(batch, seq, d)
```

```python
import functools

import jax
import jax.numpy as jnp
from jax import lax
from jax.experimental import pallas as pl
from jax.experimental.pallas import tpu as pltpu

F32 = jnp.float32
BF16 = jnp.bfloat16

EPS = 1e-6
NEG_INF = -1e30
LANES = 128
SUBLANES = 8
HEAD_DIM = 128
N_HEADS = 16
N_KV_HEADS = 4
WINDOW = 128
BLOCK = 128
CHUNK = 128
N_EXPERTS = 8
TOP_K = 2
MOE_TF = 512
MOE_TM = 1024
DENSE_TF = 512

MIB = 1024 * 1024
VMEM_LIMIT = 56 * MIB


def _params(*sem):
    return pltpu.CompilerParams(dimension_semantics=sem, vmem_limit_bytes=VMEM_LIMIT)


def _resident(shape):
    zeros = (0,) * len(shape)
    return pl.BlockSpec(shape, lambda *_: zeros, pipeline_mode=pl.Buffered(1))


class _CastJob:
    def __init__(self, w, grid, col_tile=None, block_rows=None):
        self.stacked = w.ndim == 3
        self.e, self.k, self.f = w.shape if self.stacked else (1,) + w.shape
        self.src = w.reshape(self.e * self.k, self.f)
        steps = 1
        for g in grid:
            steps *= g
        if block_rows is None:
            assert (self.e * self.k) % steps == 0
            block_rows = self.e * self.k // steps
        assert (self.e * self.k) % block_rows == 0 and self.k % block_rows == 0
        self.block_rows = block_rows
        self.n_blocks = self.e * self.k // block_rows
        assert self.n_blocks <= steps, (self.n_blocks, steps)
        self.always = self.n_blocks == steps
        strides = [1] * len(grid)
        for a in range(len(grid) - 2, -1, -1):
            strides[a] = strides[a + 1] * grid[a + 1]
        self.strides = tuple(strides)
        self.col_tile = col_tile
        assert col_tile is None or self.f % col_tile == 0

    def _step(self, ids):
        return sum(i * s for i, s in zip(ids, self.strides))

    def _block(self, ids):
        return jnp.minimum(self._step(ids[:len(self.strides)]), self.n_blocks - 1)

    def src_spec(self):
        return pl.BlockSpec((self.block_rows, self.f), lambda *ids: (self._block(ids), 0))

    def dst_spec(self):
        if self.col_tile is None:
            return self.src_spec()
        per_expert = self.k // self.block_rows
        return pl.BlockSpec(
            (None, self.f // self.col_tile, self.block_rows, self.col_tile),
            lambda *ids: (self._block(ids) // per_expert, 0, self._block(ids) % per_expert, 0))

    def out_shape(self):
        if self.col_tile is None:
            return jax.ShapeDtypeStruct(self.src.shape, BF16)
        return jax.ShapeDtypeStruct((self.e, self.f // self.col_tile, self.k, self.col_tile), BF16)

    def _cast_block(self, src_ref, dst_ref):
        if self.col_tile is None:
            dst_ref[...] = src_ref[...].astype(BF16)
        else:
            for c in range(self.f // self.col_tile):
                dst_ref[c] = src_ref[:, c * self.col_tile:(c + 1) * self.col_tile].astype(BF16)

    def run(self, src_ref, dst_ref):
        if self.always:
            self._cast_block(src_ref, dst_ref)
        else:
            ids = [pl.program_id(a) for a in range(len(self.strides))]

            @pl.when(self._step(ids) < self.n_blocks)
            def _():
                self._cast_block(src_ref, dst_ref)

    def result(self, dst):
        out = dst if self.col_tile is not None else dst.reshape(self.e, self.k, self.f)
        return out if self.stacked else out[0]


def _pallas_with_casts(kernel_fn, *, grid, in_specs, out_specs, out_shape, operands, casts=(),
                       scratch_shapes=(), name):
    jobs = [_CastJob(w, grid, col_tile, block_rows) for (w, col_tile, block_rows) in casts]
    n_in, n_out, n_jobs = len(in_specs), len(out_specs), len(jobs)

    def body(*refs):
        ins, srcs = refs[:n_in], refs[n_in:n_in + n_jobs]
        outs = refs[n_in + n_jobs:n_in + n_jobs + n_out]
        dsts = refs[n_in + n_jobs + n_out:n_in + 2 * n_jobs + n_out]
        for job, src, dst in zip(jobs, srcs, dsts):
            job.run(src, dst)
        kernel_fn(*ins, *outs, *refs[n_in + 2 * n_jobs + n_out:])

    results = pl.pallas_call(
        body,
        grid=grid,
        in_specs=list(in_specs) + [job.src_spec() for job in jobs],
        out_specs=list(out_specs) + [job.dst_spec() for job in jobs],
        out_shape=list(out_shape) + [job.out_shape() for job in jobs],
        scratch_shapes=list(scratch_shapes),
        compiler_params=_params(*(["arbitrary"] * len(grid))),
        name=name,
    )(*operands, *[job.src for job in jobs])
    return results[:n_out], [job.result(r) for job, r in zip(jobs, results[n_out:])]


def _rms(x, gain):
    ms = jnp.mean(x * x, axis=-1, keepdims=True)
    return x * lax.rsqrt(ms + EPS) * gain


def _qkv_kernel(x_ref, g_ref, w_ref, hg_ref, o_ref, *, n_norm_cols, col_chunk):
    h = _rms(x_ref[...], g_ref[...]).astype(BF16)
    n_out = o_ref.shape[1]
    for c0 in range(0, n_out, col_chunk):
        y = jnp.dot(h, w_ref[:, c0:c0 + col_chunk], preferred_element_type=F32)
        for h0 in range(0, col_chunk, HEAD_DIM):
            col = c0 + h0
            yh = y[:, h0:h0 + HEAD_DIM]
            if col < n_norm_cols:
                yh = _rms(yh, hg_ref[:, col:col + HEAD_DIM])
            o_ref[:, col:col + HEAD_DIM] = yh.astype(BF16)


def _qkv_proj(x, gain, w, head_gain, n_norm_cols, casts, tm=512):
    n, d = x.shape
    nq = w.shape[1]
    kern = functools.partial(_qkv_kernel, n_norm_cols=n_norm_cols, col_chunk=512)
    (qkv,), cast_out = _pallas_with_casts(
        kern,
        grid=(n // tm,),
        in_specs=[
            pl.BlockSpec((tm, d), lambda i: (i, 0)),
            _resident((1, d)),
            _resident((d, nq)),
            _resident((1, nq)),
        ],
        out_specs=[pl.BlockSpec((tm, nq), lambda i: (i, 0))],
        out_shape=[jax.ShapeDtypeStruct((n, nq), BF16)],
        operands=(x, gain, w, head_gain),
        casts=casts,
        name="qkv_proj",
    )
    return qkv, cast_out


def _attn_kernel(sink_ref, q_ref, kp_ref, kc_ref, kn_ref, vp_ref, vc_ref, vn_ref, bias_ref, o_ref):
    kcat = jnp.concatenate([kp_ref[...], kc_ref[...], kn_ref[...]], axis=0)
    vcat = jnp.concatenate([vp_ref[...], vc_ref[...], vn_ref[...]], axis=0)
    group = N_HEADS // N_KV_HEADS
    band = 3 * BLOCK
    scores = []
    for kh in range(N_KV_HEADS):
        k_h = kcat[:, kh * HEAD_DIM:(kh + 1) * HEAD_DIM]
        heads = [kh * group + g for g in range(group)]
        q_g = jnp.concatenate([q_ref[:, hd * HEAD_DIM:(hd + 1) * HEAD_DIM] for hd in heads], axis=0)
        scores.append(lax.dot_general(k_h, q_g, (((1,), (1,)), ((), ())), preferred_element_type=F32))
    for kh in range(N_KV_HEADS):
        v_h = vcat[:, kh * HEAD_DIM:(kh + 1) * HEAD_DIM]
        heads = [kh * group + g for g in range(group)]
        sink = jnp.concatenate([jnp.full((1, BLOCK), sink_ref[hd], F32) for hd in heads], axis=1)
        s = scores[kh] + bias_ref[kh * band:(kh + 1) * band, :]
        m = jnp.maximum(jnp.max(s, axis=0, keepdims=True), sink)
        p = jnp.exp(s - m)
        denom = jnp.sum(p, axis=0, keepdims=True) + jnp.exp(sink - m)
        o_t = lax.dot_general(v_h, p.astype(BF16), (((0,), (0,)), ((), ())),
                              preferred_element_type=F32) * (1.0 / denom)
        for g, hd in enumerate(heads):
            o_ref[:, hd * HEAD_DIM:(hd + 1) * HEAD_DIM] = o_t[:, g * BLOCK:(g + 1) * BLOCK].T.astype(BF16)


def _attention_bias():
    group = N_HEADS // N_KV_HEADS
    qi = jnp.arange(BLOCK)[:, None]
    sj = jnp.arange(3 * BLOCK)[None, :]
    dist = jnp.abs(qi - sj + BLOCK)
    slopes = jnp.exp2(-8.0 * jnp.arange(1, N_HEADS + 1, dtype=F32) / N_HEADS)
    bias = jnp.where(dist <= WINDOW, -slopes[:, None, None] * dist.astype(F32), NEG_INF)
    first = (sj >= BLOCK)[None]
    last = (sj < 2 * BLOCK)[None]
    edge = jnp.stack([jnp.where(first, bias, NEG_INF), bias, jnp.where(last, bias, NEG_INF)])
    edge = edge.reshape(3, N_KV_HEADS, group, BLOCK, 3 * BLOCK).transpose(0, 1, 4, 2, 3)
    return edge.reshape(3, N_KV_HEADS * 3 * BLOCK, group * BLOCK)


def _attention(qkv, sink, batch, seq, casts):
    n = qkv.shape[0]
    nb = seq // BLOCK
    dq = N_HEADS * HEAD_DIM
    dkv = N_KV_HEADS * HEAD_DIM
    k_col = dq // dkv
    v_col = k_col + 1

    def prev_blk(b, j):
        return b * nb + jnp.maximum(j - 1, 0)

    def next_blk(b, j):
        return b * nb + jnp.minimum(j + 1, nb - 1)

    assert nb >= 2, "first and last query block must differ"
    group = N_HEADS // N_KV_HEADS

    def bias_variant(b, j):
        return (jnp.where(j == 0, 0, jnp.where(j == nb - 1, 2, 1)), 0, 0)

    (attn,), cast_out = _pallas_with_casts(
        _attn_kernel,
        grid=(batch, nb),
        in_specs=[
            pl.BlockSpec(memory_space=pltpu.SMEM),
            pl.BlockSpec((BLOCK, dq), lambda b, j: (b * nb + j, 0)),
            pl.BlockSpec((BLOCK, dkv), lambda b, j: (prev_blk(b, j), k_col)),
            pl.BlockSpec((BLOCK, dkv), lambda b, j: (b * nb + j, k_col)),
            pl.BlockSpec((BLOCK, dkv), lambda b, j: (next_blk(b, j), k_col)),
            pl.BlockSpec((BLOCK, dkv), lambda b, j: (prev_blk(b, j), v_col)),
            pl.BlockSpec((BLOCK, dkv), lambda b, j: (b * nb + j, v_col)),
            pl.BlockSpec((BLOCK, dkv), lambda b, j: (next_blk(b, j), v_col)),
            pl.BlockSpec((None, N_KV_HEADS * 3 * BLOCK, group * BLOCK), bias_variant),
        ],
        out_specs=[pl.BlockSpec((BLOCK, dq), lambda b, j: (b * nb + j, 0))],
        out_shape=[jax.ShapeDtypeStruct((n, dq), BF16)],
        operands=(sink, qkv, qkv, qkv, qkv, qkv, qkv, qkv, _attention_bias()),
        casts=casts,
        name="window_attention",
    )
    return attn, cast_out


def _proj_res_kernel(a_ref, w_ref, r_ref, o_ref):
    o_ref[...] = r_ref[...] + jnp.dot(a_ref[...], w_ref[...], preferred_element_type=F32)


def _proj_residual(a, w, res, casts, tm=512, name="proj_residual"):
    n, k = a.shape
    d = w.shape[1]
    (out,), cast_out = _pallas_with_casts(
        _proj_res_kernel,
        grid=(n // tm,),
        in_specs=[
            pl.BlockSpec((tm, k), lambda i: (i, 0)),
            _resident((k, d)),
            pl.BlockSpec((tm, d), lambda i: (i, 0)),
        ],
        out_specs=[pl.BlockSpec((tm, d), lambda i: (i, 0))],
        out_shape=[jax.ShapeDtypeStruct((n, d), F32)],
        operands=(a, w, res),
        casts=casts,
        name=name,
    )
    return out, cast_out


def _dense_ffn_kernel(x_ref, g_ref, wg_ref, wu_ref, wd_ref, o_ref, h_ref):
    @pl.when(pl.program_id(1) == 0)
    def _():
        x = x_ref[...]
        h_ref[...] = _rms(x, g_ref[...]).astype(BF16)
        o_ref[...] = x

    h = h_ref[...]
    gt = jnp.dot(h, wg_ref[...], preferred_element_type=F32)
    up = jnp.dot(h, wu_ref[...], preferred_element_type=F32)
    act = (gt * jax.nn.sigmoid(gt) * up).astype(BF16)
    o_ref[...] += jnp.dot(act, wd_ref[...], preferred_element_type=F32)


def _dense_ffn(x, gain, w_gate_up, w_down, casts, tm=512):
    n, d = x.shape
    f = w_down.shape[0]
    tf = w_gate_up.shape[2]
    nf = f // tf
    (out,), cast_out = _pallas_with_casts(
        _dense_ffn_kernel,
        grid=(n // tm, nf),
        in_specs=[
            pl.BlockSpec((tm, d), lambda i, j: (i, 0)),
            _resident((1, d)),
            pl.BlockSpec((None, d, tf), lambda i, j: (j, 0, 0)),
            pl.BlockSpec((None, d, tf), lambda i, j: (nf + j, 0, 0)),
            pl.BlockSpec((tf, d), lambda i, j: (j, 0)),
        ],
        out_specs=[pl.BlockSpec((tm, d), lambda i, j: (i, 0))],
        out_shape=[jax.ShapeDtypeStruct((n, d), F32)],
        operands=(x, gain, w_gate_up, w_gate_up, w_down),
        casts=casts,
        scratch_shapes=[pltpu.VMEM((tm, d), BF16)],
        name="dense_ffn",
    )
    return out, cast_out


def _gmlp_in_kernel(x_ref, g_ref, w_ref, vg_ref, u_ref, v_ref, vf_ref, *, col_chunk):
    h = _rms(x_ref[...], g_ref[...]).astype(BF16)
    width = u_ref.shape[1]
    for c0 in range(0, width, col_chunk):
        z = jnp.dot(h, w_ref[:, c0:c0 + col_chunk], preferred_element_type=F32)
        u_ref[:, c0:c0 + col_chunk] = jax.nn.gelu(z).astype(BF16)
    ssq = jnp.zeros((x_ref.shape[0], 1), F32)
    for c0 in range(0, width, col_chunk):
        z = jnp.dot(h, w_ref[:, width + c0:width + c0 + col_chunk], preferred_element_type=F32)
        v = jax.nn.gelu(z)
        ssq = ssq + jnp.sum(v * v, axis=-1, keepdims=True)
        vf_ref[:, c0:c0 + col_chunk] = v
    inv = lax.rsqrt(ssq / width + EPS)
    v_ref[...] = (vf_ref[...] * inv * vg_ref[...]).astype(BF16)


def _gmlp_in(x, gain, w_in, v_gain, casts, tm=256):
    n, d = x.shape
    width = w_in.shape[1] // 2
    kern = functools.partial(_gmlp_in_kernel, col_chunk=2048)
    (u, v), cast_out = _pallas_with_casts(
        kern,
        grid=(n // tm,),
        in_specs=[
            pl.BlockSpec((tm, d), lambda i: (i, 0)),
            _resident((1, d)),
            _resident((d, 2 * width)),
            _resident((1, width)),
        ],
        out_specs=[
            pl.BlockSpec((tm, width), lambda i: (i, 0)),
            pl.BlockSpec((tm, width), lambda i: (i, 0)),
        ],
        out_shape=[
            jax.ShapeDtypeStruct((n, width), BF16),
            jax.ShapeDtypeStruct((n, width), BF16),
        ],
        operands=(x, gain, w_in, v_gain),
        casts=casts,
        scratch_shapes=[pltpu.VMEM((tm, width), F32)],
        name="gmlp_in",
    )
    return u, v, cast_out


def _top2_route(logits):
    lane = lax.broadcasted_iota(jnp.int32, logits.shape, 1)
    lg = jnp.where(lane < N_EXPERTS, logits, -jnp.inf)
    m1 = jnp.max(lg, axis=-1, keepdims=True)
    i1 = jnp.min(jnp.where(lg == m1, lane, LANES), axis=-1, keepdims=True)
    lg2 = jnp.where(lane == i1, -jnp.inf, lg)
    m2 = jnp.max(lg2, axis=-1, keepdims=True)
    i2 = jnp.min(jnp.where(lg2 == m2, lane, LANES), axis=-1, keepdims=True)
    e2 = jnp.exp(m2 - m1)
    w1 = 1.0 / (1.0 + e2)
    w2 = e2 / (1.0 + e2)
    idx = jnp.where(lane == 0, i1, jnp.where(lane == 1, i2, 0))
    gate = jnp.where(lane == 0, w1, jnp.where(lane == 1, w2, 0.0))
    return idx, gate


def _gmlp_out_router_kernel(u_ref, v_ref, ws_ref, bs_ref, wo_ref, r_ref, g_ref, wr_ref,
                            x_ref, h_ref, idx_ref, gate_ref, y_ref):
    tm, width = u_ref.shape
    for c0 in range(0, tm, CHUNK):
        for g in range(width // LANES):
            cols = slice(g * LANES, (g + 1) * LANES)
            vv = v_ref[c0:c0 + CHUNK, cols]
            mixed = jnp.dot(ws_ref[g], vv, preferred_element_type=F32) + bs_ref[:, cols]
            y_ref[c0:c0 + CHUNK, cols] = (u_ref[c0:c0 + CHUNK, cols].astype(F32) * mixed).astype(BF16)
    x = r_ref[...] + jnp.dot(y_ref[...], wo_ref[...], preferred_element_type=F32)
    x_ref[...] = x
    h = _rms(x, g_ref[...])
    h_ref[...] = h
    h_hi = h.astype(BF16)
    h_lo = (h - h_hi.astype(F32)).astype(BF16)
    a = jnp.dot(h_hi, wr_ref[...], preferred_element_type=F32)
    b = jnp.dot(h_lo, wr_ref[:, 0:LANES], preferred_element_type=F32)
    logits = a[:, 0:LANES] + (a[:, LANES:2 * LANES] + b)
    idx_ref[...], gate_ref[...] = _top2_route(logits)


def _gmlp_out_router(u, v, w_s, b_full, w_out, res, ffn_gain, w_router_split, tm=512):
    n, width = u.shape
    d = w_out.shape[1]
    groups = w_s.shape[0]
    row = lambda i: (i, 0)
    return pl.pallas_call(
        _gmlp_out_router_kernel,
        grid=(n // tm,),
        in_specs=[
            pl.BlockSpec((tm, width), row),
            pl.BlockSpec((tm, width), row),
            _resident((groups, CHUNK, CHUNK)),
            _resident((CHUNK, width)),
            _resident((width, d)),
            pl.BlockSpec((tm, d), row),
            _resident((1, d)),
            _resident((d, 2 * LANES)),
        ],
        out_specs=[
            pl.BlockSpec((tm, d), row),
            pl.BlockSpec((tm, d), row),
            pl.BlockSpec((tm, LANES), row),
            pl.BlockSpec((tm, LANES), row),
        ],
        out_shape=[
            jax.ShapeDtypeStruct((n, d), F32),
            jax.ShapeDtypeStruct((n, d), F32),
            jax.ShapeDtypeStruct((n, LANES), jnp.int32),
            jax.ShapeDtypeStruct((n, LANES), F32),
        ],
        scratch_shapes=[pltpu.VMEM((tm, width), BF16)],
        compiler_params=_params("parallel"),
        name="gmlp_out_router",
    )(u, v, w_s, b_full, w_out, res, ffn_gain, w_router_split)


def _moe_ffn_kernel(te_ref, nv_ref, tok_ref, h_hbm, wg_ref, wu_ref, wd_ref, o_ref,
                    xbuf, hb_ref, sem, *, rows_per_step, small_rows):
    i = pl.program_id(0)
    j = pl.program_id(1)
    n_tiles = pl.num_programs(0)
    nf = pl.num_programs(1)
    tm = hb_ref.shape[0]
    rows_buf = xbuf.shape[0]

    def row_copy(tile, r):
        tok = tok_ref[tile * tm + r]
        return pltpu.make_async_copy(h_hbm.at[pl.ds(tok, 1), :], xbuf.at[pl.ds(r, 1), :], sem)

    def wait_rows():
        pltpu.make_async_copy(h_hbm.at[pl.ds(0, rows_buf), :], xbuf, sem).wait()

    def prefetch_next_tile():
        for k in range(rows_per_step):
            row_copy(i + 1, j * rows_per_step + k).start()

    @pl.when((i == 0) & (j == 0))
    def _():
        def issue(r, carry):
            row_copy(0, r).start()
            return carry
        lax.fori_loop(0, rows_buf, issue, 0)

    @pl.when(j == 0)
    def _():
        wait_rows()
        hb_ref[...] = xbuf[0:tm, :].astype(BF16)
        o_ref[...] = jnp.zeros_like(o_ref)

    def swiglu(rows):
        h = hb_ref[0:rows, :]
        gt = jnp.dot(h, wg_ref[...], preferred_element_type=F32)
        up = jnp.dot(h, wu_ref[...], preferred_element_type=F32)
        act = (gt * jax.nn.sigmoid(gt) * up).astype(BF16)
        o_ref[0:rows, :] += jnp.dot(act, wd_ref[...], preferred_element_type=F32)

    n_valid = nv_ref[i]

    @pl.when(n_valid > small_rows)
    def _():
        prefetch_next_tile()
        swiglu(tm)

    @pl.when((n_valid > 0) & (n_valid <= small_rows))
    def _():
        prefetch_next_tile()
        swiglu(small_rows)

    @pl.when(n_valid == 0)
    def _():
        prefetch_next_tile()

    @pl.when((i == n_tiles - 1) & (j == nf - 1))
    def _():
        wait_rows()


def _moe_ffn(h, tile_expert, tile_valid, row_token, we_gate, we_up, we_down, tm, small_rows=128):
    n, d = h.shape
    nf, tf = we_gate.shape[1], we_gate.shape[3]
    n_tiles = tile_expert.shape[0]
    p = n_tiles * tm
    rows_per_step = -(-tm // (nf * SUBLANES)) * SUBLANES
    rows_buf = rows_per_step * nf
    row_token = jnp.pad(row_token, (0, tm + rows_buf))

    def f_tile(i, j, nv):
        return jnp.where(nv[i] > 0, j, nf - 1)

    def w_in_map(i, j, te, nv, tok):
        return (te[i], f_tile(i, j, nv), 0, 0)

    def w_out_map(i, j, te, nv, tok):
        return (te[i], f_tile(i, j, nv), 0)

    grid_spec = pltpu.PrefetchScalarGridSpec(
        num_scalar_prefetch=3,
        grid=(n_tiles, nf),
        in_specs=[
            pl.BlockSpec(memory_space=pl.ANY),
            pl.BlockSpec((None, None, d, tf), w_in_map),
            pl.BlockSpec((None, None, d, tf), w_in_map),
            pl.BlockSpec((None, tf, d), w_out_map),
        ],
        out_specs=pl.BlockSpec((tm, d), lambda i, j, te, nv, tok: (i, 0)),
        scratch_shapes=[
            pltpu.VMEM((rows_buf, d), F32),
            pltpu.VMEM((tm, d), BF16),
            pltpu.SemaphoreType.DMA(()),
        ],
    )
    return pl.pallas_call(
        functools.partial(_moe_ffn_kernel, rows_per_step=rows_per_step, small_rows=small_rows),
        grid_spec=grid_spec,
        out_shape=jax.ShapeDtypeStruct((p, d), F32),
        compiler_params=_params("arbitrary", "arbitrary"),
        name="moe_ffn",
    )(tile_expert, tile_valid, row_token, h, we_gate, we_up, we_down)


def _moe_combine_kernel(pos_ref, x_ref, gate_ref, y_hbm, o_ref, buf, sems):
    i = pl.program_id(0)
    tm = x_ref.shape[0]
    slot = i % 2

    def issue_tile(tile, dst_slot):
        def issue(r, carry):
            for k in range(TOP_K):
                p = pos_ref[(tile * tm + r) * TOP_K + k]
                pltpu.make_async_copy(y_hbm.at[pl.ds(p, 1), :], buf.at[dst_slot, k, pl.ds(r, 1), :],
                                      sems.at[dst_slot]).start()
            return carry
        lax.fori_loop(0, tm, issue, 0, unroll=8)

    @pl.when(i == 0)
    def _():
        issue_tile(0, 0)

    @pl.when(i + 1 < pl.num_programs(0))
    def _():
        issue_tile(i + 1, 1 - slot)

    for k in range(TOP_K):
        pltpu.make_async_copy(y_hbm.at[pl.ds(0, tm), :], buf.at[slot, k], sems.at[slot]).wait()
    gate = gate_ref[...]
    o_ref[...] = x_ref[...] + (gate[:, 0:1] * buf[slot, 0] + gate[:, 1:2] * buf[slot, 1])


def _moe_combine(x, gate, y_sorted, pos, tm=256):
    n, d = x.shape
    grid_spec = pltpu.PrefetchScalarGridSpec(
        num_scalar_prefetch=1,
        grid=(n // tm,),
        in_specs=[
            pl.BlockSpec((tm, d), lambda i, pos: (i, 0)),
            pl.BlockSpec((tm, LANES), lambda i, pos: (i, 0)),
            pl.BlockSpec(memory_space=pl.ANY),
        ],
        out_specs=pl.BlockSpec((tm, d), lambda i, pos: (i, 0)),
        scratch_shapes=[
            pltpu.VMEM((2, TOP_K, tm, d), F32),
            pltpu.SemaphoreType.DMA((2,)),
        ],
    )
    return pl.pallas_call(
        _moe_combine_kernel,
        grid_spec=grid_spec,
        out_shape=jax.ShapeDtypeStruct((n, d), F32),
        compiler_params=_params("arbitrary"),
        name="moe_combine",
    )(pos, x, gate, y_sorted)


def _route_plan(top_idx, tm):
    n = top_idx.shape[0]
    n_pairs = n * TOP_K
    n_tiles = (n_pairs + N_EXPERTS * (tm - 1)) // tm
    e_flat = top_idx.reshape(n_pairs)
    onehot = (e_flat[:, None] == jnp.arange(N_EXPERTS, dtype=jnp.int32)[None, :]).astype(jnp.int32)
    csum = jnp.cumsum(onehot, axis=0)
    rank = jnp.sum(onehot * (csum - 1), axis=1)
    counts = csum[-1]
    tiles_per = (counts + tm - 1) // tm
    tile_end = jnp.cumsum(tiles_per)
    row_start = (tile_end - tiles_per) * tm
    pos = (row_start[e_flat] + rank).astype(jnp.int32)
    tile_ids = jnp.arange(n_tiles, dtype=jnp.int32)
    tile_expert = jnp.sum((tile_end[None, :] <= tile_ids[:, None]).astype(jnp.int32), axis=1)
    tile_expert = jnp.minimum(tile_expert, N_EXPERTS - 1).astype(jnp.int32)
    in_expert = tile_ids - (tile_end - tiles_per)[tile_expert]
    tile_valid = jnp.clip(counts[tile_expert] - in_expert * tm, 0, tm)
    tile_valid = jnp.where(tile_ids < tile_end[-1], tile_valid, 0).astype(jnp.int32)
    row_token = jnp.zeros((n_tiles * tm,), jnp.int32).at[pos].set(
        jnp.arange(n_pairs, dtype=jnp.int32) // TOP_K)
    return tile_expert, tile_valid, row_token, pos


def kernel(x, l0_mix_norm, l0_w_qkv, l0_q_norm, l0_k_norm, l0_sink, l0_w_o, l0_ffn_norm, l0_w_gate_up, l0_w_down, l1_mix_norm, l1_w_in, l1_v_norm, l1_w_s, l1_b_s, l1_w_out, l1_ffn_norm, l1_w_router, l1_we_gate, l1_we_up, l1_we_down):
    batch, seq, d = x.shape
    n = batch * seq
    x0 = x.reshape(n, d)
    dq = N_HEADS * HEAD_DIM
    dkv = N_KV_HEADS * HEAD_DIM

    head_gain = jnp.concatenate([
        jnp.tile(l0_q_norm * (HEAD_DIM ** -0.5), N_HEADS),
        jnp.tile(l0_k_norm, N_KV_HEADS),
        jnp.ones((dkv,), F32),
    ])[None, :]
    qkv, (w_gate_up, w_o, w_down) = _qkv_proj(
        x0, l0_mix_norm[None, :], l0_w_qkv.astype(BF16), head_gain, dq + dkv,
        casts=[(l0_w_gate_up, DENSE_TF, None), (l0_w_o, None, None), (l0_w_down, None, None)])
    attn, (we_gate,) = _attention(qkv, l0_sink, batch, seq, casts=[(l1_we_gate, MOE_TF, None)])
    x1, _ = _proj_residual(attn, w_o, x0, casts=[], name="attn_out_proj")

    x2, (we_down, w_in, w_out) = _dense_ffn(
        x1, l0_ffn_norm[None, :], w_gate_up, w_down,
        casts=[(l1_we_down, None, 256), (l1_w_in, None, 16), (l1_w_out, None, 16)])

    u, v, (we_up,) = _gmlp_in(x2, l1_mix_norm[None, :], w_in, l1_v_norm[None, :],
                              casts=[(l1_we_up, MOE_TF, None)])
    b_full = jnp.repeat(l1_b_s.T, LANES, axis=1)
    w_router = jnp.pad(l1_w_router, ((0, 0), (0, LANES - N_EXPERTS)))
    w_router_hi = w_router.astype(BF16)
    w_router_lo = (w_router - w_router_hi.astype(F32)).astype(BF16)
    w_router_split = jnp.concatenate([w_router_hi, w_router_lo], axis=1)
    x3, h, ridx, rgate = _gmlp_out_router(u, v, l1_w_s.astype(BF16), b_full, w_out, x2,
                                          l1_ffn_norm[None, :], w_router_split)

    tm = MOE_TM
    tile_expert, tile_valid, row_token, pos = _route_plan(ridx[:, :TOP_K], tm)
    y_sorted = _moe_ffn(h, tile_expert, tile_valid, row_token, we_gate, we_up, we_down, tm)
    out = _moe_combine(x3, rgate, y_sorted, pos)
    return out.reshape(batch, seq, d)
```

```python
import functools

import jax
import jax.numpy as jnp
from jax import lax
from jax.experimental import pallas as pl
from jax.experimental.pallas import tpu as pltpu

F32 = jnp.float32
BF16 = jnp.bfloat16

EPS = 1e-6
NEG_INF = -1e30
LANES = 128
SUBLANES = 8
HEAD_DIM = 128
N_HEADS = 16
N_KV_HEADS = 4
WINDOW = 128
BLOCK = 128
CHUNK = 128
N_EXPERTS = 8
TOP_K = 2
MOE_TF = 1024
MOE_TM = 512
MXU_COLS = 256
DENSE_TF = 512

MIB = 1024 * 1024
VMEM_LIMIT = 56 * MIB


def _params(*sem):
    return pltpu.CompilerParams(dimension_semantics=sem, vmem_limit_bytes=VMEM_LIMIT)


def _resident(shape):
    zeros = (0,) * len(shape)
    return pl.BlockSpec(shape, lambda *_: zeros, pipeline_mode=pl.Buffered(1))


class _CastJob:
    def __init__(self, w, grid, col_tile=None, block_rows=None, pair_width=None):
        self.pair_width = pair_width
        self.stacked = w.ndim == 3
        self.e, self.k, self.f = w.shape if self.stacked else (1,) + w.shape
        self.src = w.reshape(self.e * self.k, self.f)
        steps = 1
        for g in grid:
            steps *= g
        if block_rows is None:
            assert (self.e * self.k) % steps == 0
            block_rows = self.e * self.k // steps
        assert (self.e * self.k) % block_rows == 0 and self.k % block_rows == 0
        self.block_rows = block_rows
        self.n_blocks = self.e * self.k // block_rows
        assert self.n_blocks <= steps, (self.n_blocks, steps)
        self.always = self.n_blocks == steps
        strides = [1] * len(grid)
        for a in range(len(grid) - 2, -1, -1):
            strides[a] = strides[a + 1] * grid[a + 1]
        self.strides = tuple(strides)
        self.col_tile = col_tile
        assert col_tile is None or self.f % col_tile == 0

    def _step(self, ids):
        return sum(i * s for i, s in zip(ids, self.strides))

    def _block(self, ids):
        return jnp.minimum(self._step(ids[:len(self.strides)]), self.n_blocks - 1)

    def src_spec(self):
        return pl.BlockSpec((self.block_rows, self.f), lambda *ids: (self._block(ids), 0))

    def _tiles(self):
        if self.pair_width is not None:
            return self.f // (2 * self.col_tile), 2 * self.col_tile
        return self.f // self.col_tile, self.col_tile

    def dst_spec(self):
        if self.col_tile is None:
            return self.src_spec()
        per_expert = self.k // self.block_rows
        n_tiles, width = self._tiles()
        return pl.BlockSpec(
            (None, n_tiles, self.block_rows, width),
            lambda *ids: (self._block(ids) // per_expert, 0, self._block(ids) % per_expert, 0))

    def out_shape(self):
        if self.col_tile is None:
            return jax.ShapeDtypeStruct(self.src.shape, BF16)
        n_tiles, width = self._tiles()
        return jax.ShapeDtypeStruct((self.e, n_tiles, self.k, width), BF16)

    def _cast_block(self, src_ref, dst_ref):
        ct, pw = self.col_tile, self.pair_width
        if ct is None:
            dst_ref[...] = src_ref[...].astype(BF16)
        elif pw is not None:
            half = self.f // 2
            for c in range(half // ct):
                for s in range(ct // pw):
                    a0 = c * ct + s * pw
                    dst_ref[c, :, 2 * s * pw:(2 * s + 1) * pw] = src_ref[:, a0:a0 + pw].astype(BF16)
                    dst_ref[c, :, (2 * s + 1) * pw:(2 * s + 2) * pw] = (
                        src_ref[:, half + a0:half + a0 + pw].astype(BF16))
        else:
            for c in range(self.f // ct):
                dst_ref[c] = src_ref[:, c * ct:(c + 1) * ct].astype(BF16)

    def run(self, src_ref, dst_ref):
        if self.always:
            self._cast_block(src_ref, dst_ref)
        else:
            ids = [pl.program_id(a) for a in range(len(self.strides))]

            @pl.when(self._step(ids) < self.n_blocks)
            def _():
                self._cast_block(src_ref, dst_ref)

    def result(self, dst):
        out = dst if self.col_tile is not None else dst.reshape(self.e, self.k, self.f)
        return out if self.stacked else out[0]


def _pallas_with_casts(kernel_fn, *, grid, in_specs, out_specs, out_shape, operands, casts=(),
                       scratch_shapes=(), name):
    jobs = [_CastJob(w, grid, *options) for (w, *options) in casts]
    n_in, n_out, n_jobs = len(in_specs), len(out_specs), len(jobs)

    def body(*refs):
        ins, srcs = refs[:n_in], refs[n_in:n_in + n_jobs]
        outs = refs[n_in + n_jobs:n_in + n_jobs + n_out]
        dsts = refs[n_in + n_jobs + n_out:n_in + 2 * n_jobs + n_out]
        for job, src, dst in zip(jobs, srcs, dsts):
            job.run(src, dst)
        kernel_fn(*ins, *outs, *refs[n_in + 2 * n_jobs + n_out:])

    results = pl.pallas_call(
        body,
        grid=grid,
        in_specs=list(in_specs) + [job.src_spec() for job in jobs],
        out_specs=list(out_specs) + [job.dst_spec() for job in jobs],
        out_shape=list(out_shape) + [job.out_shape() for job in jobs],
        scratch_shapes=list(scratch_shapes),
        compiler_params=_params(*(["arbitrary"] * len(grid))),
        name=name,
    )(*operands, *[job.src for job in jobs])
    return results[:n_out], [job.result(r) for job, r in zip(jobs, results[n_out:])]


def _rms(x, gain):
    ms = jnp.mean(x * x, axis=-1, keepdims=True)
    return x * lax.rsqrt(ms + EPS) * gain


def _qkv_kernel(x_ref, g_ref, w_ref, hg_ref, o_ref, *, n_norm_cols, col_chunk):
    h = _rms(x_ref[...], g_ref[...]).astype(BF16)
    n_out = o_ref.shape[1]
    for c0 in range(0, n_out, col_chunk):
        y = jnp.dot(h, w_ref[:, c0:c0 + col_chunk], preferred_element_type=F32)
        for h0 in range(0, col_chunk, HEAD_DIM):
            col = c0 + h0
            yh = y[:, h0:h0 + HEAD_DIM]
            if col < n_norm_cols:
                yh = _rms(yh, hg_ref[:, col:col + HEAD_DIM])
            o_ref[:, col:col + HEAD_DIM] = yh.astype(BF16)


def _qkv_proj(x, gain, w, head_gain, n_norm_cols, casts, tm=512):
    n, d = x.shape
    nq = w.shape[1]
    kern = functools.partial(_qkv_kernel, n_norm_cols=n_norm_cols, col_chunk=512)
    (qkv,), cast_out = _pallas_with_casts(
        kern,
        grid=(n // tm,),
        in_specs=[
            pl.BlockSpec((tm, d), lambda i: (i, 0)),
            _resident((1, d)),
            _resident((d, nq)),
            _resident((1, nq)),
        ],
        out_specs=[pl.BlockSpec((tm, nq), lambda i: (i, 0))],
        out_shape=[jax.ShapeDtypeStruct((n, nq), BF16)],
        operands=(x, gain, w, head_gain),
        casts=casts,
        name="qkv_proj",
    )
    return qkv, cast_out


def _attn_kernel(sink_ref, q_ref, kp_ref, kc_ref, kn_ref, vp_ref, vc_ref, vn_ref, bias_ref, o_ref):
    kcat = jnp.concatenate([kp_ref[...], kc_ref[...], kn_ref[...]], axis=0)
    vcat = jnp.concatenate([vp_ref[...], vc_ref[...], vn_ref[...]], axis=0)
    group = N_HEADS // N_KV_HEADS
    band = 3 * BLOCK
    scores = []
    for kh in range(N_KV_HEADS):
        k_h = kcat[:, kh * HEAD_DIM:(kh + 1) * HEAD_DIM]
        heads = [kh * group + g for g in range(group)]
        q_g = jnp.concatenate([q_ref[:, hd * HEAD_DIM:(hd + 1) * HEAD_DIM] for hd in heads], axis=0)
        scores.append(lax.dot_general(k_h, q_g, (((1,), (1,)), ((), ())), preferred_element_type=F32))
    for kh in range(N_KV_HEADS):
        v_h = vcat[:, kh * HEAD_DIM:(kh + 1) * HEAD_DIM]
        heads = [kh * group + g for g in range(group)]
        sink = jnp.concatenate([jnp.full((1, BLOCK), sink_ref[hd], F32) for hd in heads], axis=1)
        s = scores[kh] + bias_ref[kh * band:(kh + 1) * band, :]
        m = jnp.maximum(jnp.max(s, axis=0, keepdims=True), sink)
        p = jnp.exp(s - m)
        denom = jnp.sum(p, axis=0, keepdims=True) + jnp.exp(sink - m)
        o_t = lax.dot_general(v_h, p.astype(BF16), (((0,), (0,)), ((), ())),
                              preferred_element_type=F32) * (1.0 / denom)
        for g, hd in enumerate(heads):
            o_ref[:, hd * HEAD_DIM:(hd + 1) * HEAD_DIM] = o_t[:, g * BLOCK:(g + 1) * BLOCK].T.astype(BF16)


def _attention_bias():
    group = N_HEADS // N_KV_HEADS
    qi = jnp.arange(BLOCK)[:, None]
    sj = jnp.arange(3 * BLOCK)[None, :]
    dist = jnp.abs(qi - sj + BLOCK)
    slopes = jnp.exp2(-8.0 * jnp.arange(1, N_HEADS + 1, dtype=F32) / N_HEADS)
    bias = jnp.where(dist <= WINDOW, -slopes[:, None, None] * dist.astype(F32), NEG_INF)
    first = (sj >= BLOCK)[None]
    last = (sj < 2 * BLOCK)[None]
    edge = jnp.stack([jnp.where(first, bias, NEG_INF), bias, jnp.where(last, bias, NEG_INF)])
    edge = edge.reshape(3, N_KV_HEADS, group, BLOCK, 3 * BLOCK).transpose(0, 1, 4, 2, 3)
    return edge.reshape(3, N_KV_HEADS * 3 * BLOCK, group * BLOCK)


def _attention(qkv, sink, batch, seq, casts):
    n = qkv.shape[0]
    nb = seq // BLOCK
    dq = N_HEADS * HEAD_DIM
    dkv = N_KV_HEADS * HEAD_DIM
    k_col = dq // dkv
    v_col = k_col + 1

    def prev_blk(b, j):
        return b * nb + jnp.maximum(j - 1, 0)

    def next_blk(b, j):
        return b * nb + jnp.minimum(j + 1, nb - 1)

    assert nb >= 2, "first and last query block must differ"
    group = N_HEADS // N_KV_HEADS

    def bias_variant(b, j):
        return (jnp.where(j == 0, 0, jnp.where(j == nb - 1, 2, 1)), 0, 0)

    (attn,), cast_out = _pallas_with_casts(
        _attn_kernel,
        grid=(batch, nb),
        in_specs=[
            pl.BlockSpec(memory_space=pltpu.SMEM),
            pl.BlockSpec((BLOCK, dq), lambda b, j: (b * nb + j, 0)),
            pl.BlockSpec((BLOCK, dkv), lambda b, j: (prev_blk(b, j), k_col)),
            pl.BlockSpec((BLOCK, dkv), lambda b, j: (b * nb + j, k_col)),
            pl.BlockSpec((BLOCK, dkv), lambda b, j: (next_blk(b, j), k_col)),
            pl.BlockSpec((BLOCK, dkv), lambda b, j: (prev_blk(b, j), v_col)),
            pl.BlockSpec((BLOCK, dkv), lambda b, j: (b * nb + j, v_col)),
            pl.BlockSpec((BLOCK, dkv), lambda b, j: (next_blk(b, j), v_col)),
            pl.BlockSpec((None, N_KV_HEADS * 3 * BLOCK, group * BLOCK), bias_variant),
        ],
        out_specs=[pl.BlockSpec((BLOCK, dq), lambda b, j: (b * nb + j, 0))],
        out_shape=[jax.ShapeDtypeStruct((n, dq), BF16)],
        operands=(sink, qkv, qkv, qkv, qkv, qkv, qkv, qkv, _attention_bias()),
        casts=casts,
        name="window_attention",
    )
    return attn, cast_out


def _proj_res_kernel(a_ref, w_ref, r_ref, o_ref):
    o_ref[...] = r_ref[...] + jnp.dot(a_ref[...], w_ref[...], preferred_element_type=F32)


def _proj_residual(a, w, res, casts, tm=512, name="proj_residual"):
    n, k = a.shape
    d = w.shape[1]
    (out,), cast_out = _pallas_with_casts(
        _proj_res_kernel,
        grid=(n // tm,),
        in_specs=[
            pl.BlockSpec((tm, k), lambda i: (i, 0)),
            _resident((k, d)),
            pl.BlockSpec((tm, d), lambda i: (i, 0)),
        ],
        out_specs=[pl.BlockSpec((tm, d), lambda i: (i, 0))],
        out_shape=[jax.ShapeDtypeStruct((n, d), F32)],
        operands=(a, w, res),
        casts=casts,
        name=name,
    )
    return out, cast_out


def _swiglu_paired(h, wgu_ref):
    gu = jnp.dot(h, wgu_ref[...], preferred_element_type=F32)
    acts = []
    for c0 in range(0, gu.shape[1], 2 * MXU_COLS):
        gt = gu[:, c0:c0 + MXU_COLS]
        up = gu[:, c0 + MXU_COLS:c0 + 2 * MXU_COLS]
        acts.append((gt * jax.nn.sigmoid(gt) * up).astype(BF16))
    return jnp.concatenate(acts, axis=1)


def _dense_ffn_kernel(x_ref, g_ref, wgu_ref, wd_ref, o_ref, h_ref):
    @pl.when(pl.program_id(1) == 0)
    def _():
        x = x_ref[...]
        h_ref[...] = _rms(x, g_ref[...]).astype(BF16)
        o_ref[...] = x

    act = _swiglu_paired(h_ref[...], wgu_ref)
    o_ref[...] += jnp.dot(act, wd_ref[...], preferred_element_type=F32)


def _dense_ffn(x, gain, w_gate_up, w_down, casts, tm=512):
    n, d = x.shape
    f = w_down.shape[0]
    nf = w_gate_up.shape[0]
    tf = f // nf
    (out,), cast_out = _pallas_with_casts(
        _dense_ffn_kernel,
        grid=(n // tm, nf),
        in_specs=[
            pl.BlockSpec((tm, d), lambda i, j: (i, 0)),
            _resident((1, d)),
            pl.BlockSpec((None, d, 2 * tf), lambda i, j: (j, 0, 0)),
            pl.BlockSpec((tf, d), lambda i, j: (j, 0)),
        ],
        out_specs=[pl.BlockSpec((tm, d), lambda i, j: (i, 0))],
        out_shape=[jax.ShapeDtypeStruct((n, d), F32)],
        operands=(x, gain, w_gate_up, w_down),
        casts=casts,
        scratch_shapes=[pltpu.VMEM((tm, d), BF16)],
        name="dense_ffn",
    )
    return out, cast_out


def _gmlp_in_kernel(x_ref, g_ref, w_ref, vg_ref, u_ref, v_ref, vf_ref, *, col_chunk):
    h = _rms(x_ref[...], g_ref[...]).astype(BF16)
    width = u_ref.shape[1]
    for c0 in range(0, width, col_chunk):
        z = jnp.dot(h, w_ref[:, c0:c0 + col_chunk], preferred_element_type=F32)
        u_ref[:, c0:c0 + col_chunk] = jax.nn.gelu(z).astype(BF16)
    ssq = jnp.zeros((x_ref.shape[0], 1), F32)
    for c0 in range(0, width, col_chunk):
        z = jnp.dot(h, w_ref[:, width + c0:width + c0 + col_chunk], preferred_element_type=F32)
        v = jax.nn.gelu(z)
        ssq = ssq + jnp.sum(v * v, axis=-1, keepdims=True)
        vf_ref[:, c0:c0 + col_chunk] = v
    inv = lax.rsqrt(ssq / width + EPS)
    v_ref[...] = (vf_ref[...] * inv * vg_ref[...]).astype(BF16)


def _gmlp_in(x, gain, w_in, v_gain, casts, tm=256):
    n, d = x.shape
    width = w_in.shape[1] // 2
    kern = functools.partial(_gmlp_in_kernel, col_chunk=2048)
    (u, v), cast_out = _pallas_with_casts(
        kern,
        grid=(n // tm,),
        in_specs=[
            pl.BlockSpec((tm, d), lambda i: (i, 0)),
            _resident((1, d)),
            _resident((d, 2 * width)),
            _resident((1, width)),
        ],
        out_specs=[
            pl.BlockSpec((tm, width), lambda i: (i, 0)),
            pl.BlockSpec((tm, width), lambda i: (i, 0)),
        ],
        out_shape=[
            jax.ShapeDtypeStruct((n, width), BF16),
            jax.ShapeDtypeStruct((n, width), BF16),
        ],
        operands=(x, gain, w_in, v_gain),
        casts=casts,
        scratch_shapes=[pltpu.VMEM((tm, width), F32)],
        name="gmlp_in",
    )
    return u, v, cast_out


def _top2_route(logits):
    lane = lax.broadcasted_iota(jnp.int32, logits.shape, 1)
    lg = jnp.where(lane < N_EXPERTS, logits, -jnp.inf)
    m1 = jnp.max(lg, axis=-1, keepdims=True)
    i1 = jnp.min(jnp.where(lg == m1, lane, LANES), axis=-1, keepdims=True)
    lg2 = jnp.where(lane == i1, -jnp.inf, lg)
    m2 = jnp.max(lg2, axis=-1, keepdims=True)
    i2 = jnp.min(jnp.where(lg2 == m2, lane, LANES), axis=-1, keepdims=True)
    e2 = jnp.exp(m2 - m1)
    w1 = 1.0 / (1.0 + e2)
    w2 = e2 / (1.0 + e2)
    idx = jnp.where(lane == 0, i1, jnp.where(lane == 1, i2, 0))
    gate = jnp.where(lane == 0, w1, jnp.where(lane == 1, w2, 0.0))
    return idx, gate


def _gmlp_out_router_kernel(u_ref, v_ref, ws_ref, bs_ref, wo_ref, r_ref, g_ref, wr_ref,
                            x_ref, h_ref, idx_ref, gate_ref, y_ref):
    tm, width = u_ref.shape
    for c0 in range(0, tm, CHUNK):
        for g in range(width // LANES):
            cols = slice(g * LANES, (g + 1) * LANES)
            vv = v_ref[c0:c0 + CHUNK, cols]
            mixed = jnp.dot(ws_ref[g], vv, preferred_element_type=F32) + bs_ref[:, cols]
            y_ref[c0:c0 + CHUNK, cols] = (u_ref[c0:c0 + CHUNK, cols].astype(F32) * mixed).astype(BF16)
    x = r_ref[...] + jnp.dot(y_ref[...], wo_ref[...], preferred_element_type=F32)
    x_ref[...] = x
    h = _rms(x, g_ref[...])
    h_ref[...] = h
    h_hi = h.astype(BF16)
    h_lo = (h - h_hi.astype(F32)).astype(BF16)
    a = jnp.dot(h_hi, wr_ref[...], preferred_element_type=F32)
    b = jnp.dot(h_lo, wr_ref[:, 0:LANES], preferred_element_type=F32)
    logits = a[:, 0:LANES] + (a[:, LANES:2 * LANES] + b)
    idx_ref[...], gate_ref[...] = _top2_route(logits)


def _gmlp_out_router(u, v, w_s, b_full, w_out, res, ffn_gain, w_router_split, tm=512):
    n, width = u.shape
    d = w_out.shape[1]
    groups = w_s.shape[0]
    row = lambda i: (i, 0)
    return pl.pallas_call(
        _gmlp_out_router_kernel,
        grid=(n // tm,),
        in_specs=[
            pl.BlockSpec((tm, width), row),
            pl.BlockSpec((tm, width), row),
            _resident((groups, CHUNK, CHUNK)),
            _resident((CHUNK, width)),
            _resident((width, d)),
            pl.BlockSpec((tm, d), row),
            _resident((1, d)),
            _resident((d, 2 * LANES)),
        ],
        out_specs=[
            pl.BlockSpec((tm, d), row),
            pl.BlockSpec((tm, d), row),
            pl.BlockSpec((tm, LANES), row),
            pl.BlockSpec((tm, LANES), row),
        ],
        out_shape=[
            jax.ShapeDtypeStruct((n, d), F32),
            jax.ShapeDtypeStruct((n, d), F32),
            jax.ShapeDtypeStruct((n, LANES), jnp.int32),
            jax.ShapeDtypeStruct((n, LANES), F32),
        ],
        scratch_shapes=[pltpu.VMEM((tm, width), BF16)],
        compiler_params=_params("parallel"),
        name="gmlp_out_router",
    )(u, v, w_s, b_full, w_out, res, ffn_gain, w_router_split)


def _moe_ffn_kernel(te_ref, nv_ref, tok_ref, h_hbm, wg_ref, wu_ref, wd_ref, o_ref,
                    xbuf, hb_ref, sem, *, rows_per_step, small_rows):
    i = pl.program_id(0)
    j = pl.program_id(1)
    n_tiles = pl.num_programs(0)
    nf = pl.num_programs(1)
    tm = hb_ref.shape[0]
    rows_buf = xbuf.shape[0]

    def row_copy(tile, r):
        tok = tok_ref[tile * tm + r]
        return pltpu.make_async_copy(h_hbm.at[pl.ds(tok, 1), :], xbuf.at[pl.ds(r, 1), :], sem)

    def wait_rows():
        pltpu.make_async_copy(h_hbm.at[pl.ds(0, rows_buf), :], xbuf, sem).wait()

    def prefetch_next_tile():
        for k in range(rows_per_step):
            row_copy(i + 1, j * rows_per_step + k).start()

    @pl.when((i == 0) & (j == 0))
    def _():
        def issue(r, carry):
            row_copy(0, r).start()
            return carry
        lax.fori_loop(0, rows_buf, issue, 0)

    @pl.when(j == 0)
    def _():
        wait_rows()
        hb_ref[...] = xbuf[0:tm, :].astype(BF16)
        o_ref[...] = jnp.zeros_like(o_ref)

    def swiglu(rows):
        h = hb_ref[0:rows, :]
        gt = jnp.dot(h, wg_ref[...], preferred_element_type=F32)
        up = jnp.dot(h, wu_ref[...], preferred_element_type=F32)
        act = (gt * jax.nn.sigmoid(gt) * up).astype(BF16)
        o_ref[0:rows, :] += jnp.dot(act, wd_ref[...], preferred_element_type=F32)

    n_valid = nv_ref[i]

    @pl.when(n_valid > small_rows)
    def _():
        prefetch_next_tile()
        swiglu(tm)

    @pl.when((n_valid > 0) & (n_valid <= small_rows))
    def _():
        prefetch_next_tile()
        swiglu(small_rows)

    @pl.when(n_valid == 0)
    def _():
        prefetch_next_tile()

    @pl.when((i == n_tiles - 1) & (j == nf - 1))
    def _():
        wait_rows()


def _moe_ffn(h, tile_expert, tile_valid, row_token, we_gate, we_up, we_down, tm, small_rows=128):
    n, d = h.shape
    nf, tf = we_gate.shape[1], we_gate.shape[3]
    n_tiles = tile_expert.shape[0]
    p = n_tiles * tm
    rows_per_step = -(-tm // (nf * SUBLANES)) * SUBLANES
    rows_buf = rows_per_step * nf
    row_token = jnp.pad(row_token, (0, tm + rows_buf))

    def f_tile(i, j, nv):
        return jnp.where(nv[i] > 0, j, nf - 1)

    def w_in_map(i, j, te, nv, tok):
        return (te[i], f_tile(i, j, nv), 0, 0)

    def w_out_map(i, j, te, nv, tok):
        return (te[i], f_tile(i, j, nv), 0)

    grid_spec = pltpu.PrefetchScalarGridSpec(
        num_scalar_prefetch=3,
        grid=(n_tiles, nf),
        in_specs=[
            pl.BlockSpec(memory_space=pl.ANY),
            pl.BlockSpec((None, None, d, tf), w_in_map),
            pl.BlockSpec((None, None, d, tf), w_in_map),
            pl.BlockSpec((None, tf, d), w_out_map),
        ],
        out_specs=pl.BlockSpec((tm, d), lambda i, j, te, nv, tok: (i, 0)),
        scratch_shapes=[
            pltpu.VMEM((rows_buf, d), F32),
            pltpu.VMEM((tm, d), BF16),
            pltpu.SemaphoreType.DMA(()),
        ],
    )
    return pl.pallas_call(
        functools.partial(_moe_ffn_kernel, rows_per_step=rows_per_step, small_rows=small_rows),
        grid_spec=grid_spec,
        out_shape=jax.ShapeDtypeStruct((p, d), F32),
        compiler_params=_params("arbitrary", "arbitrary"),
        name="moe_ffn",
    )(tile_expert, tile_valid, row_token, h, we_gate, we_up, we_down)


def _moe_combine_kernel(pos_ref, x_ref, gate_ref, y_hbm, o_ref, buf, sems):
    i = pl.program_id(0)
    tm = x_ref.shape[0]
    slot = i % 2

    def issue_tile(tile, dst_slot):
        def issue(r, carry):
            for k in range(TOP_K):
                p = pos_ref[(tile * tm + r) * TOP_K + k]
                pltpu.make_async_copy(y_hbm.at[pl.ds(p, 1), :], buf.at[dst_slot, k, pl.ds(r, 1), :],
                                      sems.at[dst_slot]).start()
            return carry
        lax.fori_loop(0, tm, issue, 0, unroll=8)

    @pl.when(i == 0)
    def _():
        issue_tile(0, 0)

    @pl.when(i + 1 < pl.num_programs(0))
    def _():
        issue_tile(i + 1, 1 - slot)

    for k in range(TOP_K):
        pltpu.make_async_copy(y_hbm.at[pl.ds(0, tm), :], buf.at[slot, k], sems.at[slot]).wait()
    gate = gate_ref[...]
    o_ref[...] = x_ref[...] + (gate[:, 0:1] * buf[slot, 0] + gate[:, 1:2] * buf[slot, 1])


def _moe_combine(x, gate, y_sorted, pos, tm=256):
    n, d = x.shape
    grid_spec = pltpu.PrefetchScalarGridSpec(
        num_scalar_prefetch=1,
        grid=(n // tm,),
        in_specs=[
            pl.BlockSpec((tm, d), lambda i, pos: (i, 0)),
            pl.BlockSpec((tm, LANES), lambda i, pos: (i, 0)),
            pl.BlockSpec(memory_space=pl.ANY),
        ],
        out_specs=pl.BlockSpec((tm, d), lambda i, pos: (i, 0)),
        scratch_shapes=[
            pltpu.VMEM((2, TOP_K, tm, d), F32),
            pltpu.SemaphoreType.DMA((2,)),
        ],
    )
    return pl.pallas_call(
        _moe_combine_kernel,
        grid_spec=grid_spec,
        out_shape=jax.ShapeDtypeStruct((n, d), F32),
        compiler_params=_params("arbitrary"),
        name="moe_combine",
    )(pos, x, gate, y_sorted)


def _route_plan(top_idx, tm):
    n = top_idx.shape[0]
    n_pairs = n * TOP_K
    n_tiles = (n_pairs + N_EXPERTS * (tm - 1)) // tm
    e_flat = top_idx.reshape(n_pairs)
    onehot = (e_flat[:, None] == jnp.arange(N_EXPERTS, dtype=jnp.int32)[None, :]).astype(jnp.int32)
    csum = jnp.cumsum(onehot, axis=0)
    rank = jnp.sum(onehot * (csum - 1), axis=1)
    counts = csum[-1]
    tiles_per = (counts + tm - 1) // tm
    tile_end = jnp.cumsum(tiles_per)
    row_start = (tile_end - tiles_per) * tm
    pos = (row_start[e_flat] + rank).astype(jnp.int32)
    tile_ids = jnp.arange(n_tiles, dtype=jnp.int32)
    tile_expert = jnp.sum((tile_end[None, :] <= tile_ids[:, None]).astype(jnp.int32), axis=1)
    tile_expert = jnp.minimum(tile_expert, N_EXPERTS - 1).astype(jnp.int32)
    in_expert = tile_ids - (tile_end - tiles_per)[tile_expert]
    tile_valid = jnp.clip(counts[tile_expert] - in_expert * tm, 0, tm)
    tile_valid = jnp.where(tile_ids < tile_end[-1], tile_valid, 0).astype(jnp.int32)
    row_token = jnp.zeros((n_tiles * tm,), jnp.int32).at[pos].set(
        jnp.arange(n_pairs, dtype=jnp.int32) // TOP_K)
    return tile_expert, tile_valid, row_token, pos


def kernel(x, l0_mix_norm, l0_w_qkv, l0_q_norm, l0_k_norm, l0_sink, l0_w_o, l0_ffn_norm, l0_w_gate_up, l0_w_down, l1_mix_norm, l1_w_in, l1_v_norm, l1_w_s, l1_b_s, l1_w_out, l1_ffn_norm, l1_w_router, l1_we_gate, l1_we_up, l1_we_down):
    batch, seq, d = x.shape
    n = batch * seq
    x0 = x.reshape(n, d)
    dq = N_HEADS * HEAD_DIM
    dkv = N_KV_HEADS * HEAD_DIM

    head_gain = jnp.concatenate([
        jnp.tile(l0_q_norm * (HEAD_DIM ** -0.5), N_HEADS),
        jnp.tile(l0_k_norm, N_KV_HEADS),
        jnp.ones((dkv,), F32),
    ])[None, :]
    qkv, (w_gate_up, w_o, w_down) = _qkv_proj(
        x0, l0_mix_norm[None, :], l0_w_qkv.astype(BF16), head_gain, dq + dkv,
        casts=[(l0_w_gate_up, DENSE_TF, None, MXU_COLS), (l0_w_o, None, None), (l0_w_down, None, None)])
    attn, (we_gate,) = _attention(qkv, l0_sink, batch, seq, casts=[(l1_we_gate, MOE_TF, None)])
    x1, _ = _proj_residual(attn, w_o, x0, casts=[], name="attn_out_proj")

    x2, (we_down, w_in, w_out) = _dense_ffn(
        x1, l0_ffn_norm[None, :], w_gate_up, w_down,
        casts=[(l1_we_down, None, 256), (l1_w_in, None, 16), (l1_w_out, None, 16)])

    u, v, (we_up,) = _gmlp_in(x2, l1_mix_norm[None, :], w_in, l1_v_norm[None, :],
                              casts=[(l1_we_up, MOE_TF, None)])
    b_full = jnp.repeat(l1_b_s.T, LANES, axis=1)
    w_router = jnp.pad(l1_w_router, ((0, 0), (0, LANES - N_EXPERTS)))
    w_router_hi = w_router.astype(BF16)
    w_router_lo = (w_router - w_router_hi.astype(F32)).astype(BF16)
    w_router_split = jnp.concatenate([w_router_hi, w_router_lo], axis=1)
    x3, h, ridx, rgate = _gmlp_out_router(u, v, l1_w_s.astype(BF16), b_full, w_out, x2,
                                          l1_ffn_norm[None, :], w_router_split)

    tm = MOE_TM
    tile_expert, tile_valid, row_token, pos = _route_plan(ridx[:, :TOP_K], tm)
    y_sorted = _moe_ffn(h, tile_expert, tile_valid, row_token, we_gate, we_up, we_down, tm)
    out = _moe_combine(x3, rgate, y_sorted, pos)
    return out.reshape(batch, seq, d)
```

```python
import functools

import jax
import jax.numpy as jnp
from jax import lax
from jax.experimental import pallas as pl
from jax.experimental.pallas import tpu as pltpu

F32 = jnp.float32
BF16 = jnp.bfloat16

EPS = 1e-6
NEG_INF = -1e30
LANES = 128
SUBLANES = 8
HEAD_DIM = 128
N_HEADS = 16
N_KV_HEADS = 4
WINDOW = 128
BLOCK = 128
CHUNK = 128
N_EXPERTS = 8
TOP_K = 2
MOE_TF = 1024
MOE_TM = 512
MXU_COLS = 256
DENSE_TF = 512

MIB = 1024 * 1024
VMEM_LIMIT = 56 * MIB


def _params(*sem):
    return pltpu.CompilerParams(dimension_semantics=sem, vmem_limit_bytes=VMEM_LIMIT)


def _resident(shape):
    zeros = (0,) * len(shape)
    return pl.BlockSpec(shape, lambda *_: zeros, pipeline_mode=pl.Buffered(1))


class _CastJob:
    def __init__(self, w, grid, col_tile=None, block_rows=None, pair_width=None):
        self.pair_width = pair_width
        self.stacked = w.ndim == 3
        self.e, self.k, self.f = w.shape if self.stacked else (1,) + w.shape
        self.src = w.reshape(self.e * self.k, self.f)
        steps = 1
        for g in grid:
            steps *= g
        if block_rows is None:
            assert (self.e * self.k) % steps == 0
            block_rows = self.e * self.k // steps
        assert (self.e * self.k) % block_rows == 0 and self.k % block_rows == 0
        self.block_rows = block_rows
        self.n_blocks = self.e * self.k // block_rows
        assert self.n_blocks <= steps, (self.n_blocks, steps)
        strides = [1] * len(grid)
        for a in range(len(grid) - 2, -1, -1):
            strides[a] = strides[a + 1] * grid[a + 1]
        self.strides = tuple(strides)
        self.col_tile = col_tile
        assert col_tile is None or self.f % col_tile == 0

    def _step(self, ids):
        return sum(i * s for i, s in zip(ids, self.strides))

    def _block(self, ids):
        return jnp.minimum(self._step(ids[:len(self.strides)]), self.n_blocks - 1)

    def src_spec(self):
        return pl.BlockSpec((self.block_rows, self.f), lambda *ids: (self._block(ids), 0))

    def _tiles(self):
        if self.pair_width is not None:
            return self.f // (2 * self.col_tile), 2 * self.col_tile
        return self.f // self.col_tile, self.col_tile

    def dst_spec(self):
        if self.col_tile is None:
            return self.src_spec()
        per_expert = self.k // self.block_rows
        n_tiles, width = self._tiles()
        return pl.BlockSpec(
            (None, n_tiles, self.block_rows, width),
            lambda *ids: (self._block(ids) // per_expert, 0, self._block(ids) % per_expert, 0))

    def out_shape(self):
        if self.col_tile is None:
            return jax.ShapeDtypeStruct(self.src.shape, BF16)
        n_tiles, width = self._tiles()
        return jax.ShapeDtypeStruct((self.e, n_tiles, self.k, width), BF16)

    def _cast_block(self, src_ref, dst_ref):
        ct, pw = self.col_tile, self.pair_width
        if ct is None:
            dst_ref[...] = src_ref[...].astype(BF16)
        elif pw is not None:
            half = self.f // 2
            for c in range(half // ct):
                for s in range(ct // pw):
                    a0 = c * ct + s * pw
                    dst_ref[c, :, 2 * s * pw:(2 * s + 1) * pw] = src_ref[:, a0:a0 + pw].astype(BF16)
                    dst_ref[c, :, (2 * s + 1) * pw:(2 * s + 2) * pw] = (
                        src_ref[:, half + a0:half + a0 + pw].astype(BF16))
        else:
            for c in range(self.f // ct):
                dst_ref[c] = src_ref[:, c * ct:(c + 1) * ct].astype(BF16)

    def run(self, src_ref, dst_ref):
        self._cast_block(src_ref, dst_ref)

    def result(self, dst):
        out = dst if self.col_tile is not None else dst.reshape(self.e, self.k, self.f)
        return out if self.stacked else out[0]


def _pallas_with_casts(kernel_fn, *, grid, in_specs, out_specs, out_shape, operands, casts=(),
                       scratch_shapes=(), name):
    jobs = [_CastJob(w, grid, *options) for (w, *options) in casts]
    n_in, n_out, n_jobs = len(in_specs), len(out_specs), len(jobs)

    def body(*refs):
        ins, srcs = refs[:n_in], refs[n_in:n_in + n_jobs]
        outs = refs[n_in + n_jobs:n_in + n_jobs + n_out]
        dsts = refs[n_in + n_jobs + n_out:n_in + 2 * n_jobs + n_out]
        for job, src, dst in zip(jobs, srcs, dsts):
            job.run(src, dst)
        kernel_fn(*ins, *outs, *refs[n_in + 2 * n_jobs + n_out:])

    results = pl.pallas_call(
        body,
        grid=grid,
        in_specs=list(in_specs) + [job.src_spec() for job in jobs],
        out_specs=list(out_specs) + [job.dst_spec() for job in jobs],
        out_shape=list(out_shape) + [job.out_shape() for job in jobs],
        scratch_shapes=list(scratch_shapes),
        compiler_params=_params(*(["arbitrary"] * len(grid))),
        name=name,
    )(*operands, *[job.src for job in jobs])
    return results[:n_out], [job.result(r) for job, r in zip(jobs, results[n_out:])]


def _rms(x, gain):
    ms = jnp.mean(x * x, axis=-1, keepdims=True)
    return x * lax.rsqrt(ms + EPS) * gain


def _qkv_kernel(x_ref, g_ref, w_ref, hg_ref, o_ref, *, n_norm_cols, col_chunk):
    h = _rms(x_ref[...], g_ref[...]).astype(BF16)
    n_out = o_ref.shape[1]
    for c0 in range(0, n_out, col_chunk):
        y = jnp.dot(h, w_ref[:, c0:c0 + col_chunk], preferred_element_type=F32)
        for h0 in range(0, col_chunk, HEAD_DIM):
            col = c0 + h0
            yh = y[:, h0:h0 + HEAD_DIM]
            if col < n_norm_cols:
                yh = _rms(yh, hg_ref[:, col:col + HEAD_DIM])
            o_ref[:, col:col + HEAD_DIM] = yh.astype(BF16)


def _qkv_proj(x, gain, w, head_gain, n_norm_cols, casts, tm=512):
    n, d = x.shape
    nq = w.shape[1]
    kern = functools.partial(_qkv_kernel, n_norm_cols=n_norm_cols, col_chunk=512)
    (qkv,), cast_out = _pallas_with_casts(
        kern,
        grid=(n // tm,),
        in_specs=[
            pl.BlockSpec((tm, d), lambda i: (i, 0)),
            _resident((1, d)),
            _resident((d, nq)),
            _resident((1, nq)),
        ],
        out_specs=[pl.BlockSpec((tm, nq), lambda i: (i, 0))],
        out_shape=[jax.ShapeDtypeStruct((n, nq), BF16)],
        operands=(x, gain, w, head_gain),
        casts=casts,
        name="qkv_proj",
    )
    return qkv, cast_out


def _attn_kernel(sink_ref, q_ref, k0_ref, k1_ref, k2_ref, k3_ref, v0_ref, v1_ref, v2_ref, v3_ref,
                 bias_ref, o_ref):
    j = pl.program_id(1)
    kcat = jnp.concatenate([k0_ref[...], k1_ref[...], k2_ref[...], k3_ref[...]], axis=0)
    vcat = jnp.concatenate([v0_ref[...], v1_ref[...], v2_ref[...], v3_ref[...]], axis=0)
    group = N_HEADS // N_KV_HEADS
    band = 3 * BLOCK
    variants = (jnp.where(j == 0, 0, 1), jnp.where(j == pl.num_programs(1) - 1, 2, 1))
    for a in range(2):
        rows = slice(a * BLOCK, (a + 1) * BLOCK)
        for kh in range(N_KV_HEADS):
            cols = slice(kh * HEAD_DIM, (kh + 1) * HEAD_DIM)
            k_h = kcat[a * BLOCK:a * BLOCK + band, cols]
            v_h = vcat[a * BLOCK:a * BLOCK + band, cols]
            heads = [kh * group + g for g in range(group)]
            q_g = jnp.concatenate([q_ref[rows, hd * HEAD_DIM:(hd + 1) * HEAD_DIM] for hd in heads], axis=0)
            sink = jnp.concatenate([jnp.full((1, BLOCK), sink_ref[hd], F32) for hd in heads], axis=1)
            s = lax.dot_general(k_h, q_g, (((1,), (1,)), ((), ())), preferred_element_type=F32)
            s = s + bias_ref[variants[a], kh * band:(kh + 1) * band, :]
            m = jnp.maximum(jnp.max(s, axis=0, keepdims=True), sink)
            p = jnp.exp(s - m)
            denom = jnp.sum(p, axis=0, keepdims=True) + jnp.exp(sink - m)
            o_t = lax.dot_general(v_h, p.astype(BF16), (((0,), (0,)), ((), ())),
                                  preferred_element_type=F32) * (1.0 / denom)
            for g, hd in enumerate(heads):
                o_ref[rows, hd * HEAD_DIM:(hd + 1) * HEAD_DIM] = (
                    o_t[:, g * BLOCK:(g + 1) * BLOCK].T.astype(BF16))


def _attention_bias():
    group = N_HEADS // N_KV_HEADS
    qi = jnp.arange(BLOCK)[:, None]
    sj = jnp.arange(3 * BLOCK)[None, :]
    dist = jnp.abs(qi - sj + BLOCK)
    slopes = jnp.exp2(-8.0 * jnp.arange(1, N_HEADS + 1, dtype=F32) / N_HEADS)
    bias = jnp.where(dist <= WINDOW, -slopes[:, None, None] * dist.astype(F32), NEG_INF)
    first = (sj >= BLOCK)[None]
    last = (sj < 2 * BLOCK)[None]
    edge = jnp.stack([jnp.where(first, bias, NEG_INF), bias, jnp.where(last, bias, NEG_INF)])
    edge = edge.reshape(3, N_KV_HEADS, group, BLOCK, 3 * BLOCK).transpose(0, 1, 4, 2, 3)
    return edge.reshape(3, N_KV_HEADS * 3 * BLOCK, group * BLOCK)


def _attention(qkv, sink, batch, seq, casts):
    n = qkv.shape[0]
    nb = seq // BLOCK
    assert nb % 2 == 0
    steps = nb // 2
    dq = N_HEADS * HEAD_DIM
    dkv = N_KV_HEADS * HEAD_DIM
    k_col = dq // dkv
    v_col = k_col + 1
    group = N_HEADS // N_KV_HEADS

    def kv_spec(offset, col):
        return pl.BlockSpec(
            (BLOCK, dkv), lambda b, j: (b * nb + jnp.clip(2 * j + offset, 0, nb - 1), col))

    (attn,), cast_out = _pallas_with_casts(
        _attn_kernel,
        grid=(batch, steps),
        in_specs=[
            pl.BlockSpec(memory_space=pltpu.SMEM),
            pl.BlockSpec((2 * BLOCK, dq), lambda b, j: (b * steps + j, 0)),
            kv_spec(-1, k_col), kv_spec(0, k_col), kv_spec(1, k_col), kv_spec(2, k_col),
            kv_spec(-1, v_col), kv_spec(0, v_col), kv_spec(1, v_col), kv_spec(2, v_col),
            _resident((3, N_KV_HEADS * 3 * BLOCK, group * BLOCK)),
        ],
        out_specs=[pl.BlockSpec((2 * BLOCK, dq), lambda b, j: (b * steps + j, 0))],
        out_shape=[jax.ShapeDtypeStruct((n, dq), BF16)],
        operands=(sink, qkv, qkv, qkv, qkv, qkv, qkv, qkv, qkv, qkv, _attention_bias()),
        casts=casts,
        name="window_attention",
    )
    return attn, cast_out


def _proj_res_kernel(a_ref, w_ref, r_ref, o_ref):
    o_ref[...] = r_ref[...] + jnp.dot(a_ref[...], w_ref[...], preferred_element_type=F32)


def _proj_residual(a, w, res, casts, tm=512, name="proj_residual"):
    n, k = a.shape
    d = w.shape[1]
    (out,), cast_out = _pallas_with_casts(
        _proj_res_kernel,
        grid=(n // tm,),
        in_specs=[
            pl.BlockSpec((tm, k), lambda i: (i, 0)),
            _resident((k, d)),
            pl.BlockSpec((tm, d), lambda i: (i, 0)),
        ],
        out_specs=[pl.BlockSpec((tm, d), lambda i: (i, 0))],
        out_shape=[jax.ShapeDtypeStruct((n, d), F32)],
        operands=(a, w, res),
        casts=casts,
        name=name,
    )
    return out, cast_out


def _swiglu_paired(h, wgu_ref):
    gu = jnp.dot(h, wgu_ref[...], preferred_element_type=F32)
    acts = []
    for c0 in range(0, gu.shape[1], 2 * MXU_COLS):
        gt = gu[:, c0:c0 + MXU_COLS]
        up = gu[:, c0 + MXU_COLS:c0 + 2 * MXU_COLS]
        acts.append((gt * jax.nn.sigmoid(gt) * up).astype(BF16))
    return jnp.concatenate(acts, axis=1)


def _dense_ffn_kernel(x_ref, g_ref, wgu_ref, wd_ref, o_ref, h_ref):
    @pl.when(pl.program_id(1) == 0)
    def _():
        x = x_ref[...]
        h_ref[...] = _rms(x, g_ref[...]).astype(BF16)
        o_ref[...] = x

    act = _swiglu_paired(h_ref[...], wgu_ref)
    o_ref[...] += jnp.dot(act, wd_ref[...], preferred_element_type=F32)


def _dense_ffn(x, gain, w_gate_up, w_down, casts, tm=512):
    n, d = x.shape
    f = w_down.shape[0]
    nf = w_gate_up.shape[0]
    tf = f // nf
    (out,), cast_out = _pallas_with_casts(
        _dense_ffn_kernel,
        grid=(n // tm, nf),
        in_specs=[
            pl.BlockSpec((tm, d), lambda i, j: (i, 0)),
            _resident((1, d)),
            pl.BlockSpec((None, d, 2 * tf), lambda i, j: (j, 0, 0)),
            pl.BlockSpec((tf, d), lambda i, j: (j, 0)),
        ],
        out_specs=[pl.BlockSpec((tm, d), lambda i, j: (i, 0))],
        out_shape=[jax.ShapeDtypeStruct((n, d), F32)],
        operands=(x, gain, w_gate_up, w_down),
        casts=casts,
        scratch_shapes=[pltpu.VMEM((tm, d), BF16)],
        name="dense_ffn",
    )
    return out, cast_out


def _gmlp_in_kernel(x_ref, g_ref, w_ref, vg_ref, u_ref, v_ref, vf_ref, *, col_chunk):
    h = _rms(x_ref[...], g_ref[...]).astype(BF16)
    width = u_ref.shape[1]
    for c0 in range(0, width, col_chunk):
        z = jnp.dot(h, w_ref[:, c0:c0 + col_chunk], preferred_element_type=F32)
        u_ref[:, c0:c0 + col_chunk] = jax.nn.gelu(z).astype(BF16)
    ssq = jnp.zeros((x_ref.shape[0], 1), F32)
    for c0 in range(0, width, col_chunk):
        z = jnp.dot(h, w_ref[:, width + c0:width + c0 + col_chunk], preferred_element_type=F32)
        v = jax.nn.gelu(z)
        ssq = ssq + jnp.sum(v * v, axis=-1, keepdims=True)
        vf_ref[:, c0:c0 + col_chunk] = v
    inv = lax.rsqrt(ssq / width + EPS)
    v_ref[...] = (vf_ref[...] * inv * vg_ref[...]).astype(BF16)


def _gmlp_in(x, gain, w_in, v_gain, casts, tm=256):
    n, d = x.shape
    width = w_in.shape[1] // 2
    kern = functools.partial(_gmlp_in_kernel, col_chunk=2048)
    (u, v), cast_out = _pallas_with_casts(
        kern,
        grid=(n // tm,),
        in_specs=[
            pl.BlockSpec((tm, d), lambda i: (i, 0)),
            _resident((1, d)),
            _resident((d, 2 * width)),
            _resident((1, width)),
        ],
        out_specs=[
            pl.BlockSpec((tm, width), lambda i: (i, 0)),
            pl.BlockSpec((tm, width), lambda i: (i, 0)),
        ],
        out_shape=[
            jax.ShapeDtypeStruct((n, width), BF16),
            jax.ShapeDtypeStruct((n, width), BF16),
        ],
        operands=(x, gain, w_in, v_gain),
        casts=casts,
        scratch_shapes=[pltpu.VMEM((tm, width), F32)],
        name="gmlp_in",
    )
    return u, v, cast_out


def _top2_route(logits):
    lane = lax.broadcasted_iota(jnp.int32, logits.shape, 1)
    lg = jnp.where(lane < N_EXPERTS, logits, -jnp.inf)
    m1 = jnp.max(lg, axis=-1, keepdims=True)
    i1 = jnp.min(jnp.where(lg == m1, lane, LANES), axis=-1, keepdims=True)
    lg2 = jnp.where(lane == i1, -jnp.inf, lg)
    m2 = jnp.max(lg2, axis=-1, keepdims=True)
    i2 = jnp.min(jnp.where(lg2 == m2, lane, LANES), axis=-1, keepdims=True)
    e2 = jnp.exp(m2 - m1)
    w1 = 1.0 / (1.0 + e2)
    w2 = e2 / (1.0 + e2)
    idx = jnp.where(lane == 0, i1, jnp.where(lane == 1, i2, 0))
    gate = jnp.where(lane == 0, w1, jnp.where(lane == 1, w2, 0.0))
    return idx, gate


def _gmlp_out_router_kernel(u_ref, v_ref, ws_ref, bs_ref, wo_ref, r_ref, g_ref, wr_ref,
                            x_ref, h_ref, idx_ref, gate_ref, y_ref):
    tm, width = u_ref.shape
    for c0 in range(0, tm, CHUNK):
        for g in range(width // LANES):
            cols = slice(g * LANES, (g + 1) * LANES)
            vv = v_ref[c0:c0 + CHUNK, cols]
            mixed = jnp.dot(ws_ref[g], vv, preferred_element_type=F32) + bs_ref[:, cols]
            y_ref[c0:c0 + CHUNK, cols] = (u_ref[c0:c0 + CHUNK, cols].astype(F32) * mixed).astype(BF16)
    x = r_ref[...] + jnp.dot(y_ref[...], wo_ref[...], preferred_element_type=F32)
    x_ref[...] = x
    h = _rms(x, g_ref[...])
    h_ref[...] = h
    h_hi = h.astype(BF16)
    h_lo = (h - h_hi.astype(F32)).astype(BF16)
    a = jnp.dot(h_hi, wr_ref[...], preferred_element_type=F32)
    b = jnp.dot(h_lo, wr_ref[:, 0:LANES], preferred_element_type=F32)
    logits = a[:, 0:LANES] + (a[:, LANES:2 * LANES] + b)
    idx_ref[...], gate_ref[...] = _top2_route(logits)


def _gmlp_out_router(u, v, w_s, b_full, w_out, res, ffn_gain, w_router_split, tm=512):
    n, width = u.shape
    d = w_out.shape[1]
    groups = w_s.shape[0]
    row = lambda i: (i, 0)
    return pl.pallas_call(
        _gmlp_out_router_kernel,
        grid=(n // tm,),
        in_specs=[
            pl.BlockSpec((tm, width), row),
            pl.BlockSpec((tm, width), row),
            _resident((groups, CHUNK, CHUNK)),
            _resident((CHUNK, width)),
            _resident((width, d)),
            pl.BlockSpec((tm, d), row),
            _resident((1, d)),
            _resident((d, 2 * LANES)),
        ],
        out_specs=[
            pl.BlockSpec((tm, d), row),
            pl.BlockSpec((tm, d), row),
            pl.BlockSpec((tm, LANES), row),
            pl.BlockSpec((tm, LANES), row),
        ],
        out_shape=[
            jax.ShapeDtypeStruct((n, d), F32),
            jax.ShapeDtypeStruct((n, d), F32),
            jax.ShapeDtypeStruct((n, LANES), jnp.int32),
            jax.ShapeDtypeStruct((n, LANES), F32),
        ],
        scratch_shapes=[pltpu.VMEM((tm, width), BF16)],
        compiler_params=_params("parallel"),
        name="gmlp_out_router",
    )(u, v, w_s, b_full, w_out, res, ffn_gain, w_router_split)


def _moe_ffn_kernel(te_ref, nv_ref, tok_ref, h_hbm, wg_ref, wu_ref, wd_ref, o_ref,
                    xbuf, hb_ref, sem, *, rows_per_step, small_rows):
    i = pl.program_id(0)
    j = pl.program_id(1)
    n_tiles = pl.num_programs(0)
    nf = pl.num_programs(1)
    tm = hb_ref.shape[0]
    rows_buf = xbuf.shape[0]

    def row_copy(tile, r):
        tok = tok_ref[tile * tm + r]
        return pltpu.make_async_copy(h_hbm.at[pl.ds(tok, 1), :], xbuf.at[pl.ds(r, 1), :], sem)

    def wait_rows():
        pltpu.make_async_copy(h_hbm.at[pl.ds(0, rows_buf), :], xbuf, sem).wait()

    def prefetch_next_tile():
        for k in range(rows_per_step):
            row_copy(i + 1, j * rows_per_step + k).start()

    @pl.when((i == 0) & (j == 0))
    def _():
        def issue(r, carry):
            row_copy(0, r).start()
            return carry
        lax.fori_loop(0, rows_buf, issue, 0)

    @pl.when(j == 0)
    def _():
        wait_rows()
        hb_ref[...] = xbuf[0:tm, :].astype(BF16)
        o_ref[...] = jnp.zeros_like(o_ref)

    def swiglu(rows):
        h = hb_ref[0:rows, :]
        gt = jnp.dot(h, wg_ref[...], preferred_element_type=F32)
        up = jnp.dot(h, wu_ref[...], preferred_element_type=F32)
        act = (gt * jax.nn.sigmoid(gt) * up).astype(BF16)
        o_ref[0:rows, :] += jnp.dot(act, wd_ref[...], preferred_element_type=F32)

    n_valid = nv_ref[i]

    @pl.when(n_valid > small_rows)
    def _():
        prefetch_next_tile()
        swiglu(tm)

    @pl.when((n_valid > 0) & (n_valid <= small_rows))
    def _():
        prefetch_next_tile()
        swiglu(small_rows)

    @pl.when(n_valid == 0)
    def _():
        prefetch_next_tile()

    @pl.when((i == n_tiles - 1) & (j == nf - 1))
    def _():
        wait_rows()


def _moe_ffn(h, tile_expert, tile_valid, row_token, we_gate, we_up, we_down, tm, small_rows=128):
    n, d = h.shape
    nf, tf = we_gate.shape[1], we_gate.shape[3]
    n_tiles = tile_expert.shape[0]
    p = n_tiles * tm
    rows_per_step = -(-tm // (nf * SUBLANES)) * SUBLANES
    rows_buf = rows_per_step * nf
    row_token = jnp.pad(row_token, (0, tm + rows_buf))

    def f_tile(i, j, nv):
        return jnp.where(nv[i] > 0, j, nf - 1)

    def w_in_map(i, j, te, nv, tok):
        return (te[i], f_tile(i, j, nv), 0, 0)

    def w_out_map(i, j, te, nv, tok):
        return (te[i], f_tile(i, j, nv), 0)

    grid_spec = pltpu.PrefetchScalarGridSpec(
        num_scalar_prefetch=3,
        grid=(n_tiles, nf),
        in_specs=[
            pl.BlockSpec(memory_space=pl.ANY),
            pl.BlockSpec((None, None, d, tf), w_in_map),
            pl.BlockSpec((None, None, d, tf), w_in_map),
            pl.BlockSpec((None, tf, d), w_out_map),
        ],
        out_specs=pl.BlockSpec((tm, d), lambda i, j, te, nv, tok: (i, 0)),
        scratch_shapes=[
            pltpu.VMEM((rows_buf, d), F32),
            pltpu.VMEM((tm, d), BF16),
            pltpu.SemaphoreType.DMA(()),
        ],
    )
    return pl.pallas_call(
        functools.partial(_moe_ffn_kernel, rows_per_step=rows_per_step, small_rows=small_rows),
        grid_spec=grid_spec,
        out_shape=jax.ShapeDtypeStruct((p, d), F32),
        compiler_params=_params("arbitrary", "arbitrary"),
        name="moe_ffn",
    )(tile_expert, tile_valid, row_token, h, we_gate, we_up, we_down)


def _moe_combine_kernel(pos_ref, x_ref, gate_ref, y_hbm, o_ref, buf, sems):
    i = pl.program_id(0)
    tm = x_ref.shape[0]
    slot = i % 2

    def issue_tile(tile, dst_slot):
        def issue(r, carry):
            for k in range(TOP_K):
                p = pos_ref[(tile * tm + r) * TOP_K + k]
                pltpu.make_async_copy(y_hbm.at[pl.ds(p, 1), :], buf.at[dst_slot, k, pl.ds(r, 1), :],
                                      sems.at[dst_slot]).start()
            return carry
        lax.fori_loop(0, tm, issue, 0, unroll=8)

    @pl.when(i == 0)
    def _():
        issue_tile(0, 0)

    @pl.when(i + 1 < pl.num_programs(0))
    def _():
        issue_tile(i + 1, 1 - slot)

    for k in range(TOP_K):
        pltpu.make_async_copy(y_hbm.at[pl.ds(0, tm), :], buf.at[slot, k], sems.at[slot]).wait()
    gate = gate_ref[...]
    o_ref[...] = x_ref[...] + (gate[:, 0:1] * buf[slot, 0] + gate[:, 1:2] * buf[slot, 1])


def _moe_combine(x, gate, y_sorted, pos, tm=256):
    n, d = x.shape
    grid_spec = pltpu.PrefetchScalarGridSpec(
        num_scalar_prefetch=1,
        grid=(n // tm,),
        in_specs=[
            pl.BlockSpec((tm, d), lambda i, pos: (i, 0)),
            pl.BlockSpec((tm, LANES), lambda i, pos: (i, 0)),
            pl.BlockSpec(memory_space=pl.ANY),
        ],
        out_specs=pl.BlockSpec((tm, d), lambda i, pos: (i, 0)),
        scratch_shapes=[
            pltpu.VMEM((2, TOP_K, tm, d), F32),
            pltpu.SemaphoreType.DMA((2,)),
        ],
    )
    return pl.pallas_call(
        _moe_combine_kernel,
        grid_spec=grid_spec,
        out_shape=jax.ShapeDtypeStruct((n, d), F32),
        compiler_params=_params("arbitrary"),
        name="moe_combine",
    )(pos, x, gate, y_sorted)


def _route_plan(top_idx, tm):
    n = top_idx.shape[0]
    n_pairs = n * TOP_K
    n_tiles = (n_pairs + N_EXPERTS * (tm - 1)) // tm
    e_flat = top_idx.reshape(n_pairs)
    onehot = (e_flat[:, None] == jnp.arange(N_EXPERTS, dtype=jnp.int32)[None, :]).astype(jnp.int32)
    csum = jnp.cumsum(onehot, axis=0)
    rank = jnp.sum(onehot * (csum - 1), axis=1)
    counts = csum[-1]
    tiles_per = (counts + tm - 1) // tm
    tile_end = jnp.cumsum(tiles_per)
    row_start = (tile_end - tiles_per) * tm
    pos = (row_start[e_flat] + rank).astype(jnp.int32)
    tile_ids = jnp.arange(n_tiles, dtype=jnp.int32)
    tile_expert = jnp.sum((tile_end[None, :] <= tile_ids[:, None]).astype(jnp.int32), axis=1)
    tile_expert = jnp.minimum(tile_expert, N_EXPERTS - 1).astype(jnp.int32)
    in_expert = tile_ids - (tile_end - tiles_per)[tile_expert]
    tile_valid = jnp.clip(counts[tile_expert] - in_expert * tm, 0, tm)
    tile_valid = jnp.where(tile_ids < tile_end[-1], tile_valid, 0).astype(jnp.int32)
    row_token = jnp.zeros((n_tiles * tm,), jnp.int32).at[pos].set(
        jnp.arange(n_pairs, dtype=jnp.int32) // TOP_K)
    return tile_expert, tile_valid, row_token, pos


def kernel(x, l0_mix_norm, l0_w_qkv, l0_q_norm, l0_k_norm, l0_sink, l0_w_o, l0_ffn_norm, l0_w_gate_up, l0_w_down, l1_mix_norm, l1_w_in, l1_v_norm, l1_w_s, l1_b_s, l1_w_out, l1_ffn_norm, l1_w_router, l1_we_gate, l1_we_up, l1_we_down):
    batch, seq, d = x.shape
    n = batch * seq
    x0 = x.reshape(n, d)
    dq = N_HEADS * HEAD_DIM
    dkv = N_KV_HEADS * HEAD_DIM

    head_gain = jnp.concatenate([
        jnp.tile(l0_q_norm * (HEAD_DIM ** -0.5), N_HEADS),
        jnp.tile(l0_k_norm, N_KV_HEADS),
        jnp.ones((dkv,), F32),
    ])[None, :]
    qkv, (w_gate_up, w_o, w_down) = _qkv_proj(
        x0, l0_mix_norm[None, :], l0_w_qkv.astype(BF16), head_gain, dq + dkv,
        casts=[(l0_w_gate_up, DENSE_TF, None, MXU_COLS), (l0_w_o, None, None), (l0_w_down, None, None)])
    attn, (we_gate,) = _attention(qkv, l0_sink, batch, seq, casts=[(l1_we_gate, MOE_TF, None)])
    x1, _ = _proj_residual(attn, w_o, x0, casts=[], name="attn_out_proj")

    x2, (we_down, w_in, w_out) = _dense_ffn(
        x1, l0_ffn_norm[None, :], w_gate_up, w_down,
        casts=[(l1_we_down, None, 256), (l1_w_in, None, 16), (l1_w_out, None, 16)])

    u, v, (we_up,) = _gmlp_in(x2, l1_mix_norm[None, :], w_in, l1_v_norm[None, :],
                              casts=[(l1_we_up, MOE_TF, None)])
    b_full = jnp.repeat(l1_b_s.T, LANES, axis=1)
    w_router = jnp.pad(l1_w_router, ((0, 0), (0, LANES - N_EXPERTS)))
    w_router_hi = w_router.astype(BF16)
    w_router_lo = (w_router - w_router_hi.astype(F32)).astype(BF16)
    w_router_split = jnp.concatenate([w_router_hi, w_router_lo], axis=1)
    x3, h, ridx, rgate = _gmlp_out_router(u, v, l1_w_s.astype(BF16), b_full, w_out, x2,
                                          l1_ffn_norm[None, :], w_router_split)

    tm = MOE_TM
    tile_expert, tile_valid, row_token, pos = _route_plan(ridx[:, :TOP_K], tm)
    y_sorted = _moe_ffn(h, tile_expert, tile_valid, row_token, we_gate, we_up, we_down, tm)
    out = _moe_combine(x3, rgate, y_sorted, pos)
    return out.reshape(batch, seq, d)
```

```python
import functools

import jax
import jax.numpy as jnp
from jax import lax
from jax.experimental import pallas as pl
from jax.experimental.pallas import tpu as pltpu

F32 = jnp.float32
BF16 = jnp.bfloat16

EPS = 1e-6
NEG_INF = -1e30
LANES = 128
SUBLANES = 8
HEAD_DIM = 128
N_HEADS = 16
N_KV_HEADS = 4
WINDOW = 128
BLOCK = 128
CHUNK = 128
N_EXPERTS = 8
TOP_K = 2
MOE_TF = 1024
MOE_TM = 512
MXU_COLS = 256
DENSE_TF = 512

MIB = 1024 * 1024
VMEM_LIMIT = 56 * MIB


def _params(*sem):
    return pltpu.CompilerParams(dimension_semantics=sem, vmem_limit_bytes=VMEM_LIMIT)


def _resident(shape):
    zeros = (0,) * len(shape)
    return pl.BlockSpec(shape, lambda *_: zeros, pipeline_mode=pl.Buffered(1))


class _CastJob:
    def __init__(self, w, grid, col_tile=None, block_rows=None, pair_width=None):
        self.pair_width = pair_width
        self.stacked = w.ndim == 3
        self.e, self.k, self.f = w.shape if self.stacked else (1,) + w.shape
        self.src = w.reshape(self.e * self.k, self.f)
        steps = 1
        for g in grid:
            steps *= g
        if block_rows is None:
            assert (self.e * self.k) % steps == 0
            block_rows = self.e * self.k // steps
        assert (self.e * self.k) % block_rows == 0 and self.k % block_rows == 0
        self.block_rows = block_rows
        self.n_blocks = self.e * self.k // block_rows
        assert self.n_blocks <= steps, (self.n_blocks, steps)
        strides = [1] * len(grid)
        for a in range(len(grid) - 2, -1, -1):
            strides[a] = strides[a + 1] * grid[a + 1]
        self.strides = tuple(strides)
        self.col_tile = col_tile
        assert col_tile is None or self.f % col_tile == 0

    def _step(self, ids):
        return sum(i * s for i, s in zip(ids, self.strides))

    def _block(self, ids):
        return jnp.minimum(self._step(ids[:len(self.strides)]), self.n_blocks - 1)

    def src_spec(self):
        return pl.BlockSpec((self.block_rows, self.f), lambda *ids: (self._block(ids), 0))

    def _tiles(self):
        if self.pair_width is not None:
            return self.f // (2 * self.col_tile), 2 * self.col_tile
        return self.f // self.col_tile, self.col_tile

    def dst_spec(self):
        if self.col_tile is None:
            return self.src_spec()
        per_expert = self.k // self.block_rows
        n_tiles, width = self._tiles()
        return pl.BlockSpec(
            (None, n_tiles, self.block_rows, width),
            lambda *ids: (self._block(ids) // per_expert, 0, self._block(ids) % per_expert, 0))

    def out_shape(self):
        if self.col_tile is None:
            return jax.ShapeDtypeStruct(self.src.shape, BF16)
        n_tiles, width = self._tiles()
        return jax.ShapeDtypeStruct((self.e, n_tiles, self.k, width), BF16)

    def _cast_block(self, src_ref, dst_ref):
        ct, pw = self.col_tile, self.pair_width
        if ct is None:
            dst_ref[...] = src_ref[...].astype(BF16)
        elif pw is not None:
            half = self.f // 2
            for c in range(half // ct):
                for s in range(ct // pw):
                    a0 = c * ct + s * pw
                    dst_ref[c, :, 2 * s * pw:(2 * s + 1) * pw] = src_ref[:, a0:a0 + pw].astype(BF16)
                    dst_ref[c, :, (2 * s + 1) * pw:(2 * s + 2) * pw] = (
                        src_ref[:, half + a0:half + a0 + pw].astype(BF16))
        else:
            for c in range(self.f // ct):
                dst_ref[c] = src_ref[:, c * ct:(c + 1) * ct].astype(BF16)

    def run(self, src_ref, dst_ref):
        self._cast_block(src_ref, dst_ref)

    def result(self, dst):
        out = dst if self.col_tile is not None else dst.reshape(self.e, self.k, self.f)
        return out if self.stacked else out[0]


def _pallas_with_casts(kernel_fn, *, grid, in_specs, out_specs, out_shape, operands, casts=(),
                       scratch_shapes=(), name):
    jobs = [_CastJob(w, grid, *options) for (w, *options) in casts]
    n_in, n_out, n_jobs = len(in_specs), len(out_specs), len(jobs)

    def body(*refs):
        ins, srcs = refs[:n_in], refs[n_in:n_in + n_jobs]
        outs = refs[n_in + n_jobs:n_in + n_jobs + n_out]
        dsts = refs[n_in + n_jobs + n_out:n_in + 2 * n_jobs + n_out]
        for job, src, dst in zip(jobs, srcs, dsts):
            job.run(src, dst)
        kernel_fn(*ins, *outs, *refs[n_in + 2 * n_jobs + n_out:])

    results = pl.pallas_call(
        body,
        grid=grid,
        in_specs=list(in_specs) + [job.src_spec() for job in jobs],
        out_specs=list(out_specs) + [job.dst_spec() for job in jobs],
        out_shape=list(out_shape) + [job.out_shape() for job in jobs],
        scratch_shapes=list(scratch_shapes),
        compiler_params=_params(*(["arbitrary"] * len(grid))),
        name=name,
    )(*operands, *[job.src for job in jobs])
    return results[:n_out], [job.result(r) for job, r in zip(jobs, results[n_out:])]


def _rms(x, gain):
    ms = jnp.mean(x * x, axis=-1, keepdims=True)
    return x * lax.rsqrt(ms + EPS) * gain


def _qkv_kernel(x_ref, g_ref, w_ref, hg_ref, o_ref, *, n_norm_cols, col_chunk):
    h = _rms(x_ref[...], g_ref[...]).astype(BF16)
    n_out = o_ref.shape[1]
    for c0 in range(0, n_out, col_chunk):
        y = jnp.dot(h, w_ref[:, c0:c0 + col_chunk], preferred_element_type=F32)
        for h0 in range(0, col_chunk, HEAD_DIM):
            col = c0 + h0
            yh = y[:, h0:h0 + HEAD_DIM]
            if col < n_norm_cols:
                yh = _rms(yh, hg_ref[:, col:col + HEAD_DIM])
            o_ref[:, col:col + HEAD_DIM] = yh.astype(BF16)


def _qkv_proj(x, gain, w, head_gain, n_norm_cols, casts, tm=512):
    n, d = x.shape
    nq = w.shape[1]
    kern = functools.partial(_qkv_kernel, n_norm_cols=n_norm_cols, col_chunk=512)
    (qkv,), cast_out = _pallas_with_casts(
        kern,
        grid=(n // tm,),
        in_specs=[
            pl.BlockSpec((tm, d), lambda i: (i, 0)),
            _resident((1, d)),
            _resident((d, nq)),
            _resident((1, nq)),
        ],
        out_specs=[pl.BlockSpec((tm, nq), lambda i: (i, 0))],
        out_shape=[jax.ShapeDtypeStruct((n, nq), BF16)],
        operands=(x, gain, w, head_gain),
        casts=casts,
        name="qkv_proj",
    )
    return qkv, cast_out


def _attn_kernel(sink_ref, q_ref, k0_ref, k1_ref, k2_ref, k3_ref, v0_ref, v1_ref, v2_ref, v3_ref,
                 bias_ref, o_ref):
    j = pl.program_id(1)
    kcat = jnp.concatenate([k0_ref[...], k1_ref[...], k2_ref[...], k3_ref[...]], axis=0)
    vcat = jnp.concatenate([v0_ref[...], v1_ref[...], v2_ref[...], v3_ref[...]], axis=0)
    group = N_HEADS // N_KV_HEADS
    band = 3 * BLOCK
    variants = (jnp.where(j == 0, 0, 1), jnp.where(j == pl.num_programs(1) - 1, 2, 1))
    for a in range(2):
        rows = slice(a * BLOCK, (a + 1) * BLOCK)
        for kh in range(N_KV_HEADS):
            cols = slice(kh * HEAD_DIM, (kh + 1) * HEAD_DIM)
            k_h = kcat[a * BLOCK:a * BLOCK + band, cols]
            v_h = vcat[a * BLOCK:a * BLOCK + band, cols]
            heads = [kh * group + g for g in range(group)]
            q_g = jnp.concatenate([q_ref[rows, hd * HEAD_DIM:(hd + 1) * HEAD_DIM] for hd in heads], axis=0)
            sink = jnp.concatenate([jnp.full((1, BLOCK), sink_ref[hd], F32) for hd in heads], axis=1)
            s = lax.dot_general(k_h, q_g, (((1,), (1,)), ((), ())), preferred_element_type=F32)
            s = s + bias_ref[variants[a], kh * band:(kh + 1) * band, :]
            m = jnp.maximum(jnp.max(s, axis=0, keepdims=True), sink)
            p = jnp.exp(s - m)
            denom = jnp.sum(p, axis=0, keepdims=True) + jnp.exp(sink - m)
            o_t = lax.dot_general(v_h, p.astype(BF16), (((0,), (0,)), ((), ())),
                                  preferred_element_type=F32) * (1.0 / denom)
            for g, hd in enumerate(heads):
                o_ref[rows, hd * HEAD_DIM:(hd + 1) * HEAD_DIM] = (
                    o_t[:, g * BLOCK:(g + 1) * BLOCK].T.astype(BF16))


def _attention_bias():
    group = N_HEADS // N_KV_HEADS
    qi = jnp.arange(BLOCK)[:, None]
    sj = jnp.arange(3 * BLOCK)[None, :]
    dist = jnp.abs(qi - sj + BLOCK)
    slopes = jnp.exp2(-8.0 * jnp.arange(1, N_HEADS + 1, dtype=F32) / N_HEADS)
    bias = jnp.where(dist <= WINDOW, -slopes[:, None, None] * dist.astype(F32), NEG_INF)
    first = (sj >= BLOCK)[None]
    last = (sj < 2 * BLOCK)[None]
    edge = jnp.stack([jnp.where(first, bias, NEG_INF), bias, jnp.where(last, bias, NEG_INF)])
    edge = edge.reshape(3, N_KV_HEADS, group, BLOCK, 3 * BLOCK).transpose(0, 1, 4, 2, 3)
    return edge.reshape(3, N_KV_HEADS * 3 * BLOCK, group * BLOCK)


def _attention(qkv, sink, batch, seq, casts):
    n = qkv.shape[0]
    nb = seq // BLOCK
    assert nb % 2 == 0
    steps = nb // 2
    dq = N_HEADS * HEAD_DIM
    dkv = N_KV_HEADS * HEAD_DIM
    k_col = dq // dkv
    v_col = k_col + 1
    group = N_HEADS // N_KV_HEADS

    def kv_spec(offset, col):
        return pl.BlockSpec(
            (BLOCK, dkv), lambda b, j: (b * nb + jnp.clip(2 * j + offset, 0, nb - 1), col))

    (attn,), cast_out = _pallas_with_casts(
        _attn_kernel,
        grid=(batch, steps),
        in_specs=[
            pl.BlockSpec(memory_space=pltpu.SMEM),
            pl.BlockSpec((2 * BLOCK, dq), lambda b, j: (b * steps + j, 0)),
            kv_spec(-1, k_col), kv_spec(0, k_col), kv_spec(1, k_col), kv_spec(2, k_col),
            kv_spec(-1, v_col), kv_spec(0, v_col), kv_spec(1, v_col), kv_spec(2, v_col),
            _resident((3, N_KV_HEADS * 3 * BLOCK, group * BLOCK)),
        ],
        out_specs=[pl.BlockSpec((2 * BLOCK, dq), lambda b, j: (b * steps + j, 0))],
        out_shape=[jax.ShapeDtypeStruct((n, dq), BF16)],
        operands=(sink, qkv, qkv, qkv, qkv, qkv, qkv, qkv, qkv, qkv, _attention_bias()),
        casts=casts,
        name="window_attention",
    )
    return attn, cast_out


def _proj_res_kernel(a_ref, w_ref, r_ref, o_ref):
    o_ref[...] = r_ref[...] + jnp.dot(a_ref[...], w_ref[...], preferred_element_type=F32)


def _proj_residual(a, w, res, casts, tm=512, name="proj_residual"):
    n, k = a.shape
    d = w.shape[1]
    (out,), cast_out = _pallas_with_casts(
        _proj_res_kernel,
        grid=(n // tm,),
        in_specs=[
            pl.BlockSpec((tm, k), lambda i: (i, 0)),
            _resident((k, d)),
            pl.BlockSpec((tm, d), lambda i: (i, 0)),
        ],
        out_specs=[pl.BlockSpec((tm, d), lambda i: (i, 0))],
        out_shape=[jax.ShapeDtypeStruct((n, d), F32)],
        operands=(a, w, res),
        casts=casts,
        name=name,
    )
    return out, cast_out


def _swiglu_paired(h, wgu_ref):
    gu = jnp.dot(h, wgu_ref[...], preferred_element_type=F32)
    acts = []
    for c0 in range(0, gu.shape[1], 2 * MXU_COLS):
        gt = gu[:, c0:c0 + MXU_COLS]
        up = gu[:, c0 + MXU_COLS:c0 + 2 * MXU_COLS]
        acts.append((gt * jax.nn.sigmoid(gt) * up).astype(BF16))
    return jnp.concatenate(acts, axis=1)


def _dense_ffn_kernel(x_ref, g_ref, wgu_ref, wd_ref, o_ref, h_ref):
    @pl.when(pl.program_id(1) == 0)
    def _():
        x = x_ref[...]
        h_ref[...] = _rms(x, g_ref[...]).astype(BF16)
        o_ref[...] = x

    act = _swiglu_paired(h_ref[...], wgu_ref)
    o_ref[...] += jnp.dot(act, wd_ref[...], preferred_element_type=F32)


def _dense_ffn(x, gain, w_gate_up, w_down, casts, tm=512):
    n, d = x.shape
    f = w_down.shape[0]
    nf = w_gate_up.shape[0]
    tf = f // nf
    (out,), cast_out = _pallas_with_casts(
        _dense_ffn_kernel,
        grid=(n // tm, nf),
        in_specs=[
            pl.BlockSpec((tm, d), lambda i, j: (i, 0)),
            _resident((1, d)),
            pl.BlockSpec((None, d, 2 * tf), lambda i, j: (j, 0, 0)),
            pl.BlockSpec((tf, d), lambda i, j: (j, 0)),
        ],
        out_specs=[pl.BlockSpec((tm, d), lambda i, j: (i, 0))],
        out_shape=[jax.ShapeDtypeStruct((n, d), F32)],
        operands=(x, gain, w_gate_up, w_down),
        casts=casts,
        scratch_shapes=[pltpu.VMEM((tm, d), BF16)],
        name="dense_ffn",
    )
    return out, cast_out


def _gmlp_in_kernel(x_ref, g_ref, w_ref, vg_ref, u_ref, v_ref, vf_ref, *, col_chunk):
    h = _rms(x_ref[...], g_ref[...]).astype(BF16)
    width = u_ref.shape[1]
    ssq = jnp.zeros((x_ref.shape[0], 1), F32)
    for c0 in range(0, width, col_chunk):
        z = jnp.dot(h, w_ref[:, width + c0:width + c0 + col_chunk], preferred_element_type=F32)
        v = jax.nn.gelu(z)
        ssq = ssq + jnp.sum(v * v, axis=-1, keepdims=True)
        vf_ref[:, c0:c0 + col_chunk] = v
    inv = lax.rsqrt(ssq / width + EPS)
    v_ref[...] = (vf_ref[...] * inv * vg_ref[...]).astype(BF16)
    for c0 in range(0, width, col_chunk):
        z = jnp.dot(h, w_ref[:, c0:c0 + col_chunk], preferred_element_type=F32)
        u_ref[:, c0:c0 + col_chunk] = jax.nn.gelu(z).astype(BF16)


def _gmlp_in(x, gain, w_in, v_gain, casts, tm=256):
    n, d = x.shape
    width = w_in.shape[1] // 2
    kern = functools.partial(_gmlp_in_kernel, col_chunk=2048)
    (u, v), cast_out = _pallas_with_casts(
        kern,
        grid=(n // tm,),
        in_specs=[
            pl.BlockSpec((tm, d), lambda i: (i, 0)),
            _resident((1, d)),
            _resident((d, 2 * width)),
            _resident((1, width)),
        ],
        out_specs=[
            pl.BlockSpec((tm, width), lambda i: (i, 0)),
            pl.BlockSpec((tm, width), lambda i: (i, 0)),
        ],
        out_shape=[
            jax.ShapeDtypeStruct((n, width), BF16),
            jax.ShapeDtypeStruct((n, width), BF16),
        ],
        operands=(x, gain, w_in, v_gain),
        casts=casts,
        scratch_shapes=[pltpu.VMEM((tm, width), F32)],
        name="gmlp_in",
    )
    return u, v, cast_out


def _top2_route(logits):
    lane = lax.broadcasted_iota(jnp.int32, logits.shape, 1)
    lg = jnp.where(lane < N_EXPERTS, logits, -jnp.inf)
    m1 = jnp.max(lg, axis=-1, keepdims=True)
    i1 = jnp.min(jnp.where(lg == m1, lane, LANES), axis=-1, keepdims=True)
    lg2 = jnp.where(lane == i1, -jnp.inf, lg)
    m2 = jnp.max(lg2, axis=-1, keepdims=True)
    i2 = jnp.min(jnp.where(lg2 == m2, lane, LANES), axis=-1, keepdims=True)
    e2 = jnp.exp(m2 - m1)
    w1 = 1.0 / (1.0 + e2)
    w2 = e2 / (1.0 + e2)
    idx = jnp.where(lane == 0, i1, jnp.where(lane == 1, i2, 0))
    gate = jnp.where(lane == 0, w1, jnp.where(lane == 1, w2, 0.0))
    return idx, gate


def _gmlp_out_router_kernel(u_ref, v_ref, ws_ref, bs_ref, wo_ref, r_ref, g_ref, wr_ref,
                            x_ref, h_ref, idx_ref, gate_ref, y_ref):
    tm, width = u_ref.shape
    for c0 in range(0, tm, CHUNK):
        for g in range(width // LANES):
            cols = slice(g * LANES, (g + 1) * LANES)
            vv = v_ref[c0:c0 + CHUNK, cols]
            mixed = jnp.dot(ws_ref[g], vv, preferred_element_type=F32) + bs_ref[:, cols]
            y_ref[c0:c0 + CHUNK, cols] = (u_ref[c0:c0 + CHUNK, cols].astype(F32) * mixed).astype(BF16)
    x = r_ref[...] + jnp.dot(y_ref[...], wo_ref[...], preferred_element_type=F32)
    x_ref[...] = x
    h = _rms(x, g_ref[...])
    h_ref[...] = h
    h_hi = h.astype(BF16)
    h_lo = (h - h_hi.astype(F32)).astype(BF16)
    a = jnp.dot(h_hi, wr_ref[...], preferred_element_type=F32)
    b = jnp.dot(h_lo, wr_ref[:, 0:LANES], preferred_element_type=F32)
    logits = a[:, 0:LANES] + (a[:, LANES:2 * LANES] + b)
    idx_ref[...], gate_ref[...] = _top2_route(logits)


def _gmlp_out_router(u, v, w_s, b_full, w_out, res, ffn_gain, w_router_split, tm=512):
    n, width = u.shape
    d = w_out.shape[1]
    groups = w_s.shape[0]
    row = lambda i: (i, 0)
    return pl.pallas_call(
        _gmlp_out_router_kernel,
        grid=(n // tm,),
        in_specs=[
            pl.BlockSpec((tm, width), row),
            pl.BlockSpec((tm, width), row),
            _resident((groups, CHUNK, CHUNK)),
            _resident((CHUNK, width)),
            _resident((width, d)),
            pl.BlockSpec((tm, d), row),
            _resident((1, d)),
            _resident((d, 2 * LANES)),
        ],
        out_specs=[
            pl.BlockSpec((tm, d), row),
            pl.BlockSpec((tm, d), row),
            pl.BlockSpec((tm, LANES), row),
            pl.BlockSpec((tm, LANES), row),
        ],
        out_shape=[
            jax.ShapeDtypeStruct((n, d), F32),
            jax.ShapeDtypeStruct((n, d), F32),
            jax.ShapeDtypeStruct((n, LANES), jnp.int32),
            jax.ShapeDtypeStruct((n, LANES), F32),
        ],
        scratch_shapes=[pltpu.VMEM((tm, width), BF16)],
        compiler_params=_params("parallel"),
        name="gmlp_out_router",
    )(u, v, w_s, b_full, w_out, res, ffn_gain, w_router_split)


def _moe_ffn_kernel(te_ref, nv_ref, tok_ref, h_hbm, wg_ref, wu_ref, wd_ref, o_ref,
                    xbuf, hb_ref, sem, *, rows_per_step, small_rows):
    i = pl.program_id(0)
    j = pl.program_id(1)
    n_tiles = pl.num_programs(0)
    nf = pl.num_programs(1)
    tm = hb_ref.shape[0]
    rows_buf = xbuf.shape[0]

    def row_copy(tile, r):
        tok = tok_ref[tile * tm + r]
        return pltpu.make_async_copy(h_hbm.at[pl.ds(tok, 1), :], xbuf.at[pl.ds(r, 1), :], sem)

    def wait_rows():
        pltpu.make_async_copy(h_hbm.at[pl.ds(0, rows_buf), :], xbuf, sem).wait()

    def prefetch_next_tile():
        for k in range(rows_per_step):
            row_copy(i + 1, j * rows_per_step + k).start()

    @pl.when((i == 0) & (j == 0))
    def _():
        def issue(r, carry):
            row_copy(0, r).start()
            return carry
        lax.fori_loop(0, rows_buf, issue, 0)

    @pl.when(j == 0)
    def _():
        wait_rows()
        hb_ref[...] = xbuf[0:tm, :].astype(BF16)
        o_ref[...] = jnp.zeros_like(o_ref)

    def swiglu(rows):
        h = hb_ref[0:rows, :]
        gt = jnp.dot(h, wg_ref[...], preferred_element_type=F32)
        up = jnp.dot(h, wu_ref[...], preferred_element_type=F32)
        act = (gt * jax.nn.sigmoid(gt) * up).astype(BF16)
        o_ref[0:rows, :] += jnp.dot(act, wd_ref[...], preferred_element_type=F32)

    n_valid = nv_ref[i]

    @pl.when(n_valid > small_rows)
    def _():
        prefetch_next_tile()
        swiglu(tm)

    @pl.when((n_valid > 0) & (n_valid <= small_rows))
    def _():
        prefetch_next_tile()
        swiglu(small_rows)

    @pl.when(n_valid == 0)
    def _():
        prefetch_next_tile()

    @pl.when((i == n_tiles - 1) & (j == nf - 1))
    def _():
        wait_rows()


def _moe_ffn(h, tile_expert, tile_valid, row_token, we_gate, we_up, we_down, tm, small_rows=128):
    n, d = h.shape
    nf, tf = we_gate.shape[1], we_gate.shape[3]
    n_tiles = tile_expert.shape[0]
    p = n_tiles * tm
    rows_per_step = -(-tm // (nf * SUBLANES)) * SUBLANES
    rows_buf = rows_per_step * nf
    row_token = jnp.pad(row_token, (0, tm + rows_buf))

    def f_tile(i, j, nv):
        return jnp.where(nv[i] > 0, j, nf - 1)

    def w_in_map(i, j, te, nv, tok):
        return (te[i], f_tile(i, j, nv), 0, 0)

    def w_out_map(i, j, te, nv, tok):
        return (te[i], f_tile(i, j, nv), 0)

    grid_spec = pltpu.PrefetchScalarGridSpec(
        num_scalar_prefetch=3,
        grid=(n_tiles, nf),
        in_specs=[
            pl.BlockSpec(memory_space=pl.ANY),
            pl.BlockSpec((None, None, d, tf), w_in_map),
            pl.BlockSpec((None, None, d, tf), w_in_map),
            pl.BlockSpec((None, tf, d), w_out_map),
        ],
        out_specs=pl.BlockSpec((tm, d), lambda i, j, te, nv, tok: (i, 0)),
        scratch_shapes=[
            pltpu.VMEM((rows_buf, d), F32),
            pltpu.VMEM((tm, d), BF16),
            pltpu.SemaphoreType.DMA(()),
        ],
    )
    return pl.pallas_call(
        functools.partial(_moe_ffn_kernel, rows_per_step=rows_per_step, small_rows=small_rows),
        grid_spec=grid_spec,
        out_shape=jax.ShapeDtypeStruct((p, d), F32),
        compiler_params=_params("arbitrary", "arbitrary"),
        name="moe_ffn",
    )(tile_expert, tile_valid, row_token, h, we_gate, we_up, we_down)


def _moe_combine_kernel(pos_ref, x_ref, gate_ref, y_hbm, o_ref, buf, sems):
    i = pl.program_id(0)
    tm = x_ref.shape[0]
    slot = i % 2

    def row_copies(tile, r, dst_slot):
        for k in range(TOP_K):
            p = pos_ref[(tile * tm + r) * TOP_K + k]
            pltpu.make_async_copy(y_hbm.at[pl.ds(p, 1), :], buf.at[dst_slot, k, pl.ds(r, 1), :],
                                  sems.at[dst_slot]).start()

    def wait_slot(s):
        for k in range(TOP_K):
            pltpu.make_async_copy(y_hbm.at[pl.ds(0, tm), :], buf.at[s, k], sems.at[s]).wait()

    @pl.when(i == 0)
    def _():
        def issue(r, carry):
            row_copies(0, r, 0)
            return carry
        lax.fori_loop(0, tm, issue, 0, unroll=8)

    wait_slot(slot)
    chunk = 4 * SUBLANES
    for r0 in range(0, tm, chunk):
        for r in range(r0, r0 + chunk):
            row_copies(i + 1, r, 1 - slot)
        rows = slice(r0, r0 + chunk)
        gate = gate_ref[rows, :]
        o_ref[rows, :] = x_ref[rows, :] + (gate[:, 0:1] * buf[slot, 0, rows, :]
                                           + gate[:, 1:2] * buf[slot, 1, rows, :])

    @pl.when(i == pl.num_programs(0) - 1)
    def _():
        wait_slot(1 - slot)


def _moe_combine(x, gate, y_sorted, pos, tm=256):
    n, d = x.shape
    pos = jnp.pad(pos, (0, TOP_K * tm))
    grid_spec = pltpu.PrefetchScalarGridSpec(
        num_scalar_prefetch=1,
        grid=(n // tm,),
        in_specs=[
            pl.BlockSpec((tm, d), lambda i, pos: (i, 0)),
            pl.BlockSpec((tm, LANES), lambda i, pos: (i, 0)),
            pl.BlockSpec(memory_space=pl.ANY),
        ],
        out_specs=pl.BlockSpec((tm, d), lambda i, pos: (i, 0)),
        scratch_shapes=[
            pltpu.VMEM((2, TOP_K, tm, d), F32),
            pltpu.SemaphoreType.DMA((2,)),
        ],
    )
    return pl.pallas_call(
        _moe_combine_kernel,
        grid_spec=grid_spec,
        out_shape=jax.ShapeDtypeStruct((n, d), F32),
        compiler_params=_params("arbitrary"),
        name="moe_combine",
    )(pos, x, gate, y_sorted)


def _route_plan(top_idx, tm):
    n = top_idx.shape[0]
    n_pairs = n * TOP_K
    n_tiles = (n_pairs + N_EXPERTS * (tm - 1)) // tm
    e_flat = top_idx.reshape(n_pairs)
    onehot = (e_flat[:, None] == jnp.arange(N_EXPERTS, dtype=jnp.int32)[None, :]).astype(jnp.int32)
    csum = jnp.cumsum(onehot, axis=0)
    rank = jnp.sum(onehot * (csum - 1), axis=1)
    counts = csum[-1]
    tiles_per = (counts + tm - 1) // tm
    tile_end = jnp.cumsum(tiles_per)
    row_start = (tile_end - tiles_per) * tm
    pos = (row_start[e_flat] + rank).astype(jnp.int32)
    tile_ids = jnp.arange(n_tiles, dtype=jnp.int32)
    tile_expert = jnp.sum((tile_end[None, :] <= tile_ids[:, None]).astype(jnp.int32), axis=1)
    tile_expert = jnp.minimum(tile_expert, N_EXPERTS - 1).astype(jnp.int32)
    in_expert = tile_ids - (tile_end - tiles_per)[tile_expert]
    tile_valid = jnp.clip(counts[tile_expert] - in_expert * tm, 0, tm)
    tile_valid = jnp.where(tile_ids < tile_end[-1], tile_valid, 0).astype(jnp.int32)
    row_token = jnp.zeros((n_tiles * tm,), jnp.int32).at[pos].set(
        jnp.arange(n_pairs, dtype=jnp.int32) // TOP_K)
    return tile_expert, tile_valid, row_token, pos


def kernel(x, l0_mix_norm, l0_w_qkv, l0_q_norm, l0_k_norm, l0_sink, l0_w_o, l0_ffn_norm, l0_w_gate_up, l0_w_down, l1_mix_norm, l1_w_in, l1_v_norm, l1_w_s, l1_b_s, l1_w_out, l1_ffn_norm, l1_w_router, l1_we_gate, l1_we_up, l1_we_down):
    batch, seq, d = x.shape
    n = batch * seq
    x0 = x.reshape(n, d)
    dq = N_HEADS * HEAD_DIM
    dkv = N_KV_HEADS * HEAD_DIM

    head_gain = jnp.concatenate([
        jnp.tile(l0_q_norm * (HEAD_DIM ** -0.5), N_HEADS),
        jnp.tile(l0_k_norm, N_KV_HEADS),
        jnp.ones((dkv,), F32),
    ])[None, :]
    qkv, (w_gate_up, w_o, w_down) = _qkv_proj(
        x0, l0_mix_norm[None, :], l0_w_qkv.astype(BF16), head_gain, dq + dkv,
        casts=[(l0_w_gate_up, DENSE_TF, None, MXU_COLS), (l0_w_o, None, None), (l0_w_down, None, None)])
    attn, (we_gate,) = _attention(qkv, l0_sink, batch, seq, casts=[(l1_we_gate, MOE_TF, None)])
    x1, _ = _proj_residual(attn, w_o, x0, casts=[], name="attn_out_proj")

    x2, (we_down, w_in, w_out) = _dense_ffn(
        x1, l0_ffn_norm[None, :], w_gate_up, w_down,
        casts=[(l1_we_down, None, 256), (l1_w_in, None, 16), (l1_w_out, None, 16)])

    u, v, (we_up,) = _gmlp_in(x2, l1_mix_norm[None, :], w_in, l1_v_norm[None, :],
                              casts=[(l1_we_up, MOE_TF, None)])
    b_full = jnp.repeat(l1_b_s.T, LANES, axis=1)
    w_router = jnp.pad(l1_w_router, ((0, 0), (0, LANES - N_EXPERTS)))
    w_router_hi = w_router.astype(BF16)
    w_router_lo = (w_router - w_router_hi.astype(F32)).astype(BF16)
    w_router_split = jnp.concatenate([w_router_hi, w_router_lo], axis=1)
    x3, h, ridx, rgate = _gmlp_out_router(u, v, l1_w_s.astype(BF16), b_full, w_out, x2,
                                          l1_ffn_norm[None, :], w_router_split)

    tm = MOE_TM
    tile_expert, tile_valid, row_token, pos = _route_plan(ridx[:, :TOP_K], tm)
    y_sorted = _moe_ffn(h, tile_expert, tile_valid, row_token, we_gate, we_up, we_down, tm)
    out = _moe_combine(x3, rgate, y_sorted, pos)
    return out.reshape(batch, seq, d)
```

```python
import functools

import jax
import jax.numpy as jnp
import numpy as np
from jax import lax
from jax.experimental import pallas as pl
from jax.experimental.pallas import tpu as pltpu

F32 = jnp.float32
BF16 = jnp.bfloat16

EPS = 1e-6
NEG_INF = -1e30
LANES = 128
SUBLANES = 8
HEAD_DIM = 128
N_HEADS = 16
N_KV_HEADS = 4
WINDOW = 128
BLOCK = 128
CHUNK = 128
N_EXPERTS = 8
TOP_K = 2
MOE_TF = 1024
MOE_TM = 512
MXU_COLS = 256
DENSE_TF = 512

MIB = 1024 * 1024
VMEM_LIMIT = 56 * MIB


def _params(*sem):
    return pltpu.CompilerParams(dimension_semantics=sem, vmem_limit_bytes=VMEM_LIMIT)


def _resident(shape):
    zeros = (0,) * len(shape)
    return pl.BlockSpec(shape, lambda *_: zeros, pipeline_mode=pl.Buffered(1))


class _CastJob:
    def __init__(self, w, grid, col_tile=None, block_rows=None, pair_width=None):
        self.pair_width = pair_width
        self.stacked = w.ndim == 3
        self.e, self.k, self.f = w.shape if self.stacked else (1,) + w.shape
        self.src = w.reshape(self.e * self.k, self.f)
        steps = 1
        for g in grid:
            steps *= g
        if block_rows is None:
            assert (self.e * self.k) % steps == 0
            block_rows = self.e * self.k // steps
        assert (self.e * self.k) % block_rows == 0 and self.k % block_rows == 0
        self.block_rows = block_rows
        self.n_blocks = self.e * self.k // block_rows
        assert self.n_blocks <= steps, (self.n_blocks, steps)
        strides = [1] * len(grid)
        for a in range(len(grid) - 2, -1, -1):
            strides[a] = strides[a + 1] * grid[a + 1]
        self.strides = tuple(strides)
        self.col_tile = col_tile
        assert col_tile is None or self.f % col_tile == 0

    def _step(self, ids):
        return sum(i * s for i, s in zip(ids, self.strides))

    def _block(self, ids):
        return jnp.minimum(self._step(ids[:len(self.strides)]), self.n_blocks - 1)

    def src_spec(self):
        return pl.BlockSpec((self.block_rows, self.f), lambda *ids: (self._block(ids), 0))

    def _tiles(self):
        if self.pair_width is not None:
            return self.f // (2 * self.col_tile), 2 * self.col_tile
        return self.f // self.col_tile, self.col_tile

    def dst_spec(self):
        if self.col_tile is None:
            return self.src_spec()
        per_expert = self.k // self.block_rows
        n_tiles, width = self._tiles()
        return pl.BlockSpec(
            (None, n_tiles, self.block_rows, width),
            lambda *ids: (self._block(ids) // per_expert, 0, self._block(ids) % per_expert, 0))

    def out_shape(self):
        if self.col_tile is None:
            return jax.ShapeDtypeStruct(self.src.shape, BF16)
        n_tiles, width = self._tiles()
        return jax.ShapeDtypeStruct((self.e, n_tiles, self.k, width), BF16)

    def _cast_block(self, src_ref, dst_ref):
        ct, pw = self.col_tile, self.pair_width
        if ct is None:
            dst_ref[...] = src_ref[...].astype(BF16)
        elif pw is not None:
            half = self.f // 2
            for c in range(half // ct):
                for s in range(ct // pw):
                    a0 = c * ct + s * pw
                    dst_ref[c, :, 2 * s * pw:(2 * s + 1) * pw] = src_ref[:, a0:a0 + pw].astype(BF16)
                    dst_ref[c, :, (2 * s + 1) * pw:(2 * s + 2) * pw] = (
                        src_ref[:, half + a0:half + a0 + pw].astype(BF16))
        else:
            for c in range(self.f // ct):
                dst_ref[c] = src_ref[:, c * ct:(c + 1) * ct].astype(BF16)

    def run(self, src_ref, dst_ref):
        self._cast_block(src_ref, dst_ref)

    def result(self, dst):
        out = dst if self.col_tile is not None else dst.reshape(self.e, self.k, self.f)
        return out if self.stacked else out[0]


def _pallas_with_casts(kernel_fn, *, grid, in_specs, out_specs, out_shape, operands, casts=(),
                       scratch_shapes=(), name):
    jobs = [_CastJob(w, grid, *options) for (w, *options) in casts]
    n_in, n_out, n_jobs = len(in_specs), len(out_specs), len(jobs)

    def body(*refs):
        ins, srcs = refs[:n_in], refs[n_in:n_in + n_jobs]
        outs = refs[n_in + n_jobs:n_in + n_jobs + n_out]
        dsts = refs[n_in + n_jobs + n_out:n_in + 2 * n_jobs + n_out]
        for job, src, dst in zip(jobs, srcs, dsts):
            job.run(src, dst)
        kernel_fn(*ins, *outs, *refs[n_in + 2 * n_jobs + n_out:])

    results = pl.pallas_call(
        body,
        grid=grid,
        in_specs=list(in_specs) + [job.src_spec() for job in jobs],
        out_specs=list(out_specs) + [job.dst_spec() for job in jobs],
        out_shape=list(out_shape) + [job.out_shape() for job in jobs],
        scratch_shapes=list(scratch_shapes),
        compiler_params=_params(*(["arbitrary"] * len(grid))),
        name=name,
    )(*operands, *[job.src for job in jobs])
    return results[:n_out], [job.result(r) for job, r in zip(jobs, results[n_out:])]


def _rms(x, gain):
    ms = jnp.mean(x * x, axis=-1, keepdims=True)
    return x * lax.rsqrt(ms + EPS) * gain


def _qkv_kernel(x_ref, g_ref, w_ref, hg_ref, o_ref, *, n_norm_cols, col_chunk):
    h = _rms(x_ref[...], g_ref[...]).astype(BF16)
    n_out = o_ref.shape[1]
    for c0 in range(0, n_out, col_chunk):
        y = jnp.dot(h, w_ref[:, c0:c0 + col_chunk], preferred_element_type=F32)
        for h0 in range(0, col_chunk, HEAD_DIM):
            col = c0 + h0
            yh = y[:, h0:h0 + HEAD_DIM]
            if col < n_norm_cols:
                yh = _rms(yh, hg_ref[:, col:col + HEAD_DIM])
            o_ref[:, col:col + HEAD_DIM] = yh.astype(BF16)


def _qkv_proj(x, gain, w, head_gain, n_norm_cols, casts, tm=512):
    n, d = x.shape
    nq = w.shape[1]
    kern = functools.partial(_qkv_kernel, n_norm_cols=n_norm_cols, col_chunk=512)
    (qkv,), cast_out = _pallas_with_casts(
        kern,
        grid=(n // tm,),
        in_specs=[
            pl.BlockSpec((tm, d), lambda i: (i, 0)),
            _resident((1, d)),
            _resident((d, nq)),
            _resident((1, nq)),
        ],
        out_specs=[pl.BlockSpec((tm, nq), lambda i: (i, 0))],
        out_shape=[jax.ShapeDtypeStruct((n, nq), BF16)],
        operands=(x, gain, w, head_gain),
        casts=casts,
        name="qkv_proj",
    )
    return qkv, cast_out


def _attn_kernel(sink_ref, q_ref, k0_ref, k1_ref, k2_ref, k3_ref, v0_ref, v1_ref, v2_ref, v3_ref,
                 bias_ref, o_ref):
    j = pl.program_id(1)
    kcat = jnp.concatenate([k0_ref[...], k1_ref[...], k2_ref[...], k3_ref[...]], axis=0)
    vcat = jnp.concatenate([v0_ref[...], v1_ref[...], v2_ref[...], v3_ref[...]], axis=0)
    group = N_HEADS // N_KV_HEADS
    band = 3 * BLOCK
    variants = (jnp.where(j == 0, 0, 1), jnp.where(j == pl.num_programs(1) - 1, 2, 1))
    for a in range(2):
        rows = slice(a * BLOCK, (a + 1) * BLOCK)
        for kh in range(N_KV_HEADS):
            cols = slice(kh * HEAD_DIM, (kh + 1) * HEAD_DIM)
            k_h = kcat[a * BLOCK:a * BLOCK + band, cols]
            v_h = vcat[a * BLOCK:a * BLOCK + band, cols]
            heads = [kh * group + g for g in range(group)]
            q_g = jnp.concatenate([q_ref[rows, hd * HEAD_DIM:(hd + 1) * HEAD_DIM] for hd in heads], axis=0)
            sink = jnp.concatenate([jnp.full((1, BLOCK), sink_ref[hd], F32) for hd in heads], axis=1)
            s = lax.dot_general(k_h, q_g, (((1,), (1,)), ((), ())), preferred_element_type=F32)
            s = s + bias_ref[variants[a], kh * band:(kh + 1) * band, :]
            m = jnp.maximum(jnp.max(s, axis=0, keepdims=True), sink)
            p = jnp.exp(s - m)
            denom = jnp.sum(p, axis=0, keepdims=True) + jnp.exp(sink - m)
            o_t = lax.dot_general(v_h, p.astype(BF16), (((0,), (0,)), ((), ())),
                                  preferred_element_type=F32) * (1.0 / denom)
            for g, hd in enumerate(heads):
                o_ref[rows, hd * HEAD_DIM:(hd + 1) * HEAD_DIM] = (
                    o_t[:, g * BLOCK:(g + 1) * BLOCK].T.astype(BF16))


def _attention_bias():
    group = N_HEADS // N_KV_HEADS
    qi = np.arange(BLOCK)[:, None]
    sj = np.arange(3 * BLOCK)[None, :]
    dist = np.abs(qi - sj + BLOCK)
    slopes = np.exp2(-8.0 * np.arange(1, N_HEADS + 1, dtype=np.float64) / N_HEADS)
    bias = np.where(dist <= WINDOW, -slopes[:, None, None] * dist, NEG_INF)
    first = (sj >= BLOCK)[None]
    last = (sj < 2 * BLOCK)[None]
    edge = np.stack([np.where(first, bias, NEG_INF), bias, np.where(last, bias, NEG_INF)])
    edge = edge.reshape(3, N_KV_HEADS, group, BLOCK, 3 * BLOCK).transpose(0, 1, 4, 2, 3)
    return np.ascontiguousarray(edge.reshape(3, N_KV_HEADS * 3 * BLOCK, group * BLOCK), dtype=np.float32)


def _attention(qkv, sink, batch, seq, casts):
    n = qkv.shape[0]
    nb = seq // BLOCK
    assert nb % 2 == 0
    steps = nb // 2
    dq = N_HEADS * HEAD_DIM
    dkv = N_KV_HEADS * HEAD_DIM
    k_col = dq // dkv
    v_col = k_col + 1
    group = N_HEADS // N_KV_HEADS

    def kv_spec(offset, col):
        return pl.BlockSpec(
            (BLOCK, dkv), lambda b, j: (b * nb + jnp.clip(2 * j + offset, 0, nb - 1), col))

    (attn,), cast_out = _pallas_with_casts(
        _attn_kernel,
        grid=(batch, steps),
        in_specs=[
            pl.BlockSpec(memory_space=pltpu.SMEM),
            pl.BlockSpec((2 * BLOCK, dq), lambda b, j: (b * steps + j, 0)),
            kv_spec(-1, k_col), kv_spec(0, k_col), kv_spec(1, k_col), kv_spec(2, k_col),
            kv_spec(-1, v_col), kv_spec(0, v_col), kv_spec(1, v_col), kv_spec(2, v_col),
            _resident((3, N_KV_HEADS * 3 * BLOCK, group * BLOCK)),
        ],
        out_specs=[pl.BlockSpec((2 * BLOCK, dq), lambda b, j: (b * steps + j, 0))],
        out_shape=[jax.ShapeDtypeStruct((n, dq), BF16)],
        operands=(sink, qkv, qkv, qkv, qkv, qkv, qkv, qkv, qkv, qkv, _attention_bias()),
        casts=casts,
        name="window_attention",
    )
    return attn, cast_out


def _proj_res_kernel(a_ref, w_ref, r_ref, o_ref):
    o_ref[...] = r_ref[...] + jnp.dot(a_ref[...], w_ref[...], preferred_element_type=F32)


def _proj_residual(a, w, res, casts, tm=512, name="proj_residual"):
    n, k = a.shape
    d = w.shape[1]
    (out,), cast_out = _pallas_with_casts(
        _proj_res_kernel,
        grid=(n // tm,),
        in_specs=[
            pl.BlockSpec((tm, k), lambda i: (i, 0)),
            _resident((k, d)),
            pl.BlockSpec((tm, d), lambda i: (i, 0)),
        ],
        out_specs=[pl.BlockSpec((tm, d), lambda i: (i, 0))],
        out_shape=[jax.ShapeDtypeStruct((n, d), F32)],
        operands=(a, w, res),
        casts=casts,
        name=name,
    )
    return out, cast_out


def _swiglu_paired(h, wgu_ref):
    gu = jnp.dot(h, wgu_ref[...], preferred_element_type=F32)
    acts = []
    for c0 in range(0, gu.shape[1], 2 * MXU_COLS):
        gt = gu[:, c0:c0 + MXU_COLS]
        up = gu[:, c0 + MXU_COLS:c0 + 2 * MXU_COLS]
        acts.append((gt * jax.nn.sigmoid(gt) * up).astype(BF16))
    return jnp.concatenate(acts, axis=1)


def _dense_ffn_kernel(x_ref, g_ref, wgu_ref, wd_ref, o_ref, h_ref):
    @pl.when(pl.program_id(1) == 0)
    def _():
        x = x_ref[...]
        h_ref[...] = _rms(x, g_ref[...]).astype(BF16)
        o_ref[...] = x

    act = _swiglu_paired(h_ref[...], wgu_ref)
    o_ref[...] += jnp.dot(act, wd_ref[...], preferred_element_type=F32)


def _dense_ffn(x, gain, w_gate_up, w_down, casts, tm=512):
    n, d = x.shape
    f = w_down.shape[0]
    nf = w_gate_up.shape[0]
    tf = f // nf
    (out,), cast_out = _pallas_with_casts(
        _dense_ffn_kernel,
        grid=(n // tm, nf),
        in_specs=[
            pl.BlockSpec((tm, d), lambda i, j: (i, 0)),
            _resident((1, d)),
            pl.BlockSpec((None, d, 2 * tf), lambda i, j: (j, 0, 0)),
            pl.BlockSpec((tf, d), lambda i, j: (j, 0)),
        ],
        out_specs=[pl.BlockSpec((tm, d), lambda i, j: (i, 0))],
        out_shape=[jax.ShapeDtypeStruct((n, d), F32)],
        operands=(x, gain, w_gate_up, w_down),
        casts=casts,
        scratch_shapes=[pltpu.VMEM((tm, d), BF16)],
        name="dense_ffn",
    )
    return out, cast_out


def _gmlp_in_kernel(x_ref, g_ref, w_ref, vg_ref, u_ref, v_ref, vf_ref, *, col_chunk):
    h = _rms(x_ref[...], g_ref[...]).astype(BF16)
    width = u_ref.shape[1]
    ssq = jnp.zeros((x_ref.shape[0], 1), F32)
    for c0 in range(0, width, col_chunk):
        z = jnp.dot(h, w_ref[:, width + c0:width + c0 + col_chunk], preferred_element_type=F32)
        v = jax.nn.gelu(z)
        ssq = ssq + jnp.sum(v * v, axis=-1, keepdims=True)
        vf_ref[:, c0:c0 + col_chunk] = v
    inv = lax.rsqrt(ssq / width + EPS)
    v_ref[...] = (vf_ref[...] * inv * vg_ref[...]).astype(BF16)
    for c0 in range(0, width, col_chunk):
        z = jnp.dot(h, w_ref[:, c0:c0 + col_chunk], preferred_element_type=F32)
        u_ref[:, c0:c0 + col_chunk] = jax.nn.gelu(z).astype(BF16)


def _gmlp_in(x, gain, w_in, v_gain, casts, tm=256):
    n, d = x.shape
    width = w_in.shape[1] // 2
    kern = functools.partial(_gmlp_in_kernel, col_chunk=2048)
    (u, v), cast_out = _pallas_with_casts(
        kern,
        grid=(n // tm,),
        in_specs=[
            pl.BlockSpec((tm, d), lambda i: (i, 0)),
            _resident((1, d)),
            _resident((d, 2 * width)),
            _resident((1, width)),
        ],
        out_specs=[
            pl.BlockSpec((tm, width), lambda i: (i, 0)),
            pl.BlockSpec((tm, width), lambda i: (i, 0)),
        ],
        out_shape=[
            jax.ShapeDtypeStruct((n, width), BF16),
            jax.ShapeDtypeStruct((n, width), BF16),
        ],
        operands=(x, gain, w_in, v_gain),
        casts=casts,
        scratch_shapes=[pltpu.VMEM((tm, width), F32)],
        name="gmlp_in",
    )
    return u, v, cast_out


def _top2_route(logits):
    lane = lax.broadcasted_iota(jnp.int32, logits.shape, 1)
    lg = jnp.where(lane < N_EXPERTS, logits, -jnp.inf)
    m1 = jnp.max(lg, axis=-1, keepdims=True)
    i1 = jnp.min(jnp.where(lg == m1, lane, LANES), axis=-1, keepdims=True)
    lg2 = jnp.where(lane == i1, -jnp.inf, lg)
    m2 = jnp.max(lg2, axis=-1, keepdims=True)
    i2 = jnp.min(jnp.where(lg2 == m2, lane, LANES), axis=-1, keepdims=True)
    e2 = jnp.exp(m2 - m1)
    w1 = 1.0 / (1.0 + e2)
    w2 = e2 / (1.0 + e2)
    idx = jnp.where(lane == 0, i1, jnp.where(lane == 1, i2, 0))
    gate = jnp.where(lane == 0, w1, jnp.where(lane == 1, w2, 0.0))
    return idx, gate


def _gmlp_out_router_kernel(u_ref, v_ref, ws_ref, bs_ref, wo_ref, r_ref, g_ref, wr_ref,
                            x_ref, h_ref, idx_ref, gate_ref, y_ref):
    tm, width = u_ref.shape
    for c0 in range(0, tm, CHUNK):
        for g in range(width // LANES):
            cols = slice(g * LANES, (g + 1) * LANES)
            vv = v_ref[c0:c0 + CHUNK, cols]
            mixed = jnp.dot(ws_ref[g], vv, preferred_element_type=F32) + bs_ref[:, cols]
            y_ref[c0:c0 + CHUNK, cols] = (u_ref[c0:c0 + CHUNK, cols].astype(F32) * mixed).astype(BF16)
    x = r_ref[...] + jnp.dot(y_ref[...], wo_ref[...], preferred_element_type=F32)
    x_ref[...] = x
    h = _rms(x, g_ref[...])
    h_ref[...] = h
    h_hi = h.astype(BF16)
    h_lo = (h - h_hi.astype(F32)).astype(BF16)
    a = jnp.dot(h_hi, wr_ref[...], preferred_element_type=F32)
    b = jnp.dot(h_lo, wr_ref[:, 0:LANES], preferred_element_type=F32)
    logits = a[:, 0:LANES] + (a[:, LANES:2 * LANES] + b)
    idx_ref[...], gate_ref[...] = _top2_route(logits)


def _gmlp_out_router(u, v, w_s, b_full, w_out, res, ffn_gain, w_router_split, tm=512):
    n, width = u.shape
    d = w_out.shape[1]
    groups = w_s.shape[0]
    row = lambda i: (i, 0)
    return pl.pallas_call(
        _gmlp_out_router_kernel,
        grid=(n // tm,),
        in_specs=[
            pl.BlockSpec((tm, width), row),
            pl.BlockSpec((tm, width), row),
            _resident((groups, CHUNK, CHUNK)),
            _resident((CHUNK, width)),
            _resident((width, d)),
            pl.BlockSpec((tm, d), row),
            _resident((1, d)),
            _resident((d, 2 * LANES)),
        ],
        out_specs=[
            pl.BlockSpec((tm, d), row),
            pl.BlockSpec((tm, d), row),
            pl.BlockSpec((tm, LANES), row),
            pl.BlockSpec((tm, LANES), row),
        ],
        out_shape=[
            jax.ShapeDtypeStruct((n, d), F32),
            jax.ShapeDtypeStruct((n, d), F32),
            jax.ShapeDtypeStruct((n, LANES), jnp.int32),
            jax.ShapeDtypeStruct((n, LANES), F32),
        ],
        scratch_shapes=[pltpu.VMEM((tm, width), BF16)],
        compiler_params=_params("parallel"),
        name="gmlp_out_router",
    )(u, v, w_s, b_full, w_out, res, ffn_gain, w_router_split)


def _moe_ffn_kernel(te_ref, nv_ref, tok_ref, h_hbm, wg_ref, wu_ref, wd_ref, o_ref,
                    xbuf, hb_ref, sem, *, rows_per_step, small_rows):
    i = pl.program_id(0)
    j = pl.program_id(1)
    n_tiles = pl.num_programs(0)
    nf = pl.num_programs(1)
    tm = hb_ref.shape[0]
    rows_buf = xbuf.shape[0]

    def row_copy(tile, r):
        tok = tok_ref[tile * tm + r]
        return pltpu.make_async_copy(h_hbm.at[pl.ds(tok, 1), :], xbuf.at[pl.ds(r, 1), :], sem)

    def wait_rows():
        pltpu.make_async_copy(h_hbm.at[pl.ds(0, rows_buf), :], xbuf, sem).wait()

    def prefetch_next_tile():
        for k in range(rows_per_step):
            row_copy(i + 1, j * rows_per_step + k).start(priority=1)

    @pl.when((i == 0) & (j == 0))
    def _():
        def issue(r, carry):
            row_copy(0, r).start()
            return carry
        lax.fori_loop(0, rows_buf, issue, 0)

    @pl.when(j == 0)
    def _():
        wait_rows()
        hb_ref[...] = xbuf[0:tm, :].astype(BF16)
        o_ref[...] = jnp.zeros_like(o_ref)

    def swiglu(rows):
        h = hb_ref[0:rows, :]
        gt = jnp.dot(h, wg_ref[...], preferred_element_type=F32)
        up = jnp.dot(h, wu_ref[...], preferred_element_type=F32)
        act = (gt * jax.nn.sigmoid(gt) * up).astype(BF16)
        o_ref[0:rows, :] += jnp.dot(act, wd_ref[...], preferred_element_type=F32)

    n_valid = nv_ref[i]

    @pl.when(n_valid > small_rows)
    def _():
        prefetch_next_tile()
        swiglu(tm)

    @pl.when((n_valid > 0) & (n_valid <= small_rows))
    def _():
        prefetch_next_tile()
        swiglu(small_rows)

    @pl.when(n_valid == 0)
    def _():
        prefetch_next_tile()

    @pl.when((i == n_tiles - 1) & (j == nf - 1))
    def _():
        wait_rows()


def _moe_ffn(h, tile_expert, tile_valid, row_token, we_gate, we_up, we_down, tm, small_rows=128):
    n, d = h.shape
    nf, tf = we_gate.shape[1], we_gate.shape[3]
    n_tiles = tile_expert.shape[0]
    p = n_tiles * tm
    rows_per_step = -(-tm // (nf * SUBLANES)) * SUBLANES
    rows_buf = rows_per_step * nf
    row_token = jnp.pad(row_token, (0, tm + rows_buf))

    def f_tile(i, j, nv):
        return jnp.where(nv[i] > 0, j, nf - 1)

    def w_in_map(i, j, te, nv, tok):
        return (te[i], f_tile(i, j, nv), 0, 0)

    def w_out_map(i, j, te, nv, tok):
        return (te[i], f_tile(i, j, nv), 0)

    grid_spec = pltpu.PrefetchScalarGridSpec(
        num_scalar_prefetch=3,
        grid=(n_tiles, nf),
        in_specs=[
            pl.BlockSpec(memory_space=pl.ANY),
            pl.BlockSpec((None, None, d, tf), w_in_map),
            pl.BlockSpec((None, None, d, tf), w_in_map),
            pl.BlockSpec((None, tf, d), w_out_map),
        ],
        out_specs=pl.BlockSpec((tm, d), lambda i, j, te, nv, tok: (i, 0)),
        scratch_shapes=[
            pltpu.VMEM((rows_buf, d), F32),
            pltpu.VMEM((tm, d), BF16),
            pltpu.SemaphoreType.DMA(()),
        ],
    )
    return pl.pallas_call(
        functools.partial(_moe_ffn_kernel, rows_per_step=rows_per_step, small_rows=small_rows),
        grid_spec=grid_spec,
        out_shape=jax.ShapeDtypeStruct((p, d), F32),
        compiler_params=_params("arbitrary", "arbitrary"),
        name="moe_ffn",
    )(tile_expert, tile_valid, row_token, h, we_gate, we_up, we_down)


def _moe_combine_kernel(pos_ref, x_ref, gate_ref, y_hbm, o_ref, buf, sems):
    i = pl.program_id(0)
    tm = x_ref.shape[0]
    slot = i % 2

    def row_copies(tile, r, dst_slot):
        for k in range(TOP_K):
            p = pos_ref[(tile * tm + r) * TOP_K + k]
            pltpu.make_async_copy(y_hbm.at[pl.ds(p, 1), :], buf.at[dst_slot, k, pl.ds(r, 1), :],
                                  sems.at[dst_slot]).start(priority=k)

    def wait_slot(s):
        for k in range(TOP_K):
            pltpu.make_async_copy(y_hbm.at[pl.ds(0, tm), :], buf.at[s, k], sems.at[s]).wait()

    @pl.when(i == 0)
    def _():
        def issue(r, carry):
            row_copies(0, r, 0)
            return carry
        lax.fori_loop(0, tm, issue, 0, unroll=8)

    wait_slot(slot)
    chunk = 4 * SUBLANES
    for r0 in range(0, tm, chunk):
        for r in range(r0, r0 + chunk):
            row_copies(i + 1, r, 1 - slot)
        rows = slice(r0, r0 + chunk)
        gate = gate_ref[rows, :]
        o_ref[rows, :] = x_ref[rows, :] + (gate[:, 0:1] * buf[slot, 0, rows, :]
                                           + gate[:, 1:2] * buf[slot, 1, rows, :])

    @pl.when(i == pl.num_programs(0) - 1)
    def _():
        wait_slot(1 - slot)


def _moe_combine(x, gate, y_sorted, pos, tm=256):
    n, d = x.shape
    pos = jnp.pad(pos, (0, TOP_K * tm))
    grid_spec = pltpu.PrefetchScalarGridSpec(
        num_scalar_prefetch=1,
        grid=(n // tm,),
        in_specs=[
            pl.BlockSpec((tm, d), lambda i, pos: (i, 0)),
            pl.BlockSpec((tm, LANES), lambda i, pos: (i, 0)),
            pl.BlockSpec(memory_space=pl.ANY),
        ],
        out_specs=pl.BlockSpec((tm, d), lambda i, pos: (i, 0)),
        scratch_shapes=[
            pltpu.VMEM((2, TOP_K, tm, d), F32),
            pltpu.SemaphoreType.DMA((2,)),
        ],
    )
    return pl.pallas_call(
        _moe_combine_kernel,
        grid_spec=grid_spec,
        out_shape=jax.ShapeDtypeStruct((n, d), F32),
        compiler_params=_params("arbitrary"),
        name="moe_combine",
    )(pos, x, gate, y_sorted)


def _route_plan(top_idx, tm):
    n = top_idx.shape[0]
    n_pairs = n * TOP_K
    n_tiles = (n_pairs + N_EXPERTS * (tm - 1)) // tm
    hot = top_idx[:, :, None] == jnp.arange(N_EXPERTS, dtype=jnp.int32)[None, None, :]
    per_token = jnp.sum(hot, axis=1, dtype=jnp.int32)
    csum = jnp.cumsum(per_token, axis=0)
    counts = csum[-1]
    tiles_per = (counts + tm - 1) // tm
    tile_end = jnp.cumsum(tiles_per)
    row_start = (tile_end - tiles_per) * tm
    slot = (csum - per_token) + row_start[None, :]
    pos = jnp.sum(jnp.where(hot, slot[:, None, :], 0), axis=2).reshape(n_pairs).astype(jnp.int32)
    tile_ids = jnp.arange(n_tiles, dtype=jnp.int32)
    tile_expert = jnp.sum((tile_end[None, :] <= tile_ids[:, None]).astype(jnp.int32), axis=1)
    tile_expert = jnp.minimum(tile_expert, N_EXPERTS - 1).astype(jnp.int32)
    in_expert = tile_ids - (tile_end - tiles_per)[tile_expert]
    tile_valid = jnp.clip(counts[tile_expert] - in_expert * tm, 0, tm)
    tile_valid = jnp.where(tile_ids < tile_end[-1], tile_valid, 0).astype(jnp.int32)
    row_token = jnp.zeros((n_tiles * tm,), jnp.int32).at[pos].set(
        jnp.arange(n_pairs, dtype=jnp.int32) // TOP_K)
    return tile_expert, tile_valid, row_token, pos


def kernel(x, l0_mix_norm, l0_w_qkv, l0_q_norm, l0_k_norm, l0_sink, l0_w_o, l0_ffn_norm, l0_w_gate_up, l0_w_down, l1_mix_norm, l1_w_in, l1_v_norm, l1_w_s, l1_b_s, l1_w_out, l1_ffn_norm, l1_w_router, l1_we_gate, l1_we_up, l1_we_down):
    batch, seq, d = x.shape
    n = batch * seq
    x0 = x.reshape(n, d)
    dq = N_HEADS * HEAD_DIM
    dkv = N_KV_HEADS * HEAD_DIM

    head_gain = jnp.concatenate([
        jnp.tile(l0_q_norm * (HEAD_DIM ** -0.5), N_HEADS),
        jnp.tile(l0_k_norm, N_KV_HEADS),
        jnp.ones((dkv,), F32),
    ])[None, :]
    qkv, (w_gate_up, w_o, w_down) = _qkv_proj(
        x0, l0_mix_norm[None, :], l0_w_qkv.astype(BF16), head_gain, dq + dkv,
        casts=[(l0_w_gate_up, DENSE_TF, None, MXU_COLS), (l0_w_o, None, None), (l0_w_down, None, None)])
    attn, (we_gate,) = _attention(qkv, l0_sink, batch, seq, casts=[(l1_we_gate, MOE_TF, None)])
    x1, _ = _proj_residual(attn, w_o, x0, casts=[], name="attn_out_proj")

    x2, (we_down, w_in, w_out) = _dense_ffn(
        x1, l0_ffn_norm[None, :], w_gate_up, w_down,
        casts=[(l1_we_down, None, 256), (l1_w_in, None, 16), (l1_w_out, None, 16)])

    u, v, (we_up,) = _gmlp_in(x2, l1_mix_norm[None, :], w_in, l1_v_norm[None, :],
                              casts=[(l1_we_up, MOE_TF, None)])
    b_full = jnp.repeat(l1_b_s.T, LANES, axis=1)
    w_router = jnp.pad(l1_w_router, ((0, 0), (0, LANES - N_EXPERTS)))
    w_router_hi = w_router.astype(BF16)
    w_router_lo = (w_router - w_router_hi.astype(F32)).astype(BF16)
    w_router_split = jnp.concatenate([w_router_hi, w_router_lo], axis=1)
    x3, h, ridx, rgate = _gmlp_out_router(u, v, l1_w_s.astype(BF16), b_full, w_out, x2,
                                          l1_ffn_norm[None, :], w_router_split)

    tm = MOE_TM
    tile_expert, tile_valid, row_token, pos = _route_plan(ridx[:, :TOP_K], tm)
    y_sorted = _moe_ffn(h, tile_expert, tile_valid, row_token, we_gate, we_up, we_down, tm)
    out = _moe_combine(x3, rgate, y_sorted, pos)
    return out.reshape(batch, seq, d)
```

```python
import functools

import jax
import jax.numpy as jnp
import numpy as np
from jax import lax
from jax.experimental import pallas as pl
from jax.experimental.pallas import tpu as pltpu

F32 = jnp.float32
BF16 = jnp.bfloat16

EPS = 1e-6
NEG_INF = -1e30
LANES = 128
SUBLANES = 8
HEAD_DIM = 128
N_HEADS = 16
N_KV_HEADS = 4
WINDOW = 128
BLOCK = 128
CHUNK = 128
N_EXPERTS = 8
TOP_K = 2
MOE_TF = 1024
MOE_TM = 512
MOE_TM_EXT = 128
MXU_COLS = 256
DENSE_TF = 512

MIB = 1024 * 1024
VMEM_LIMIT = 56 * MIB


def _params(*sem):
    return pltpu.CompilerParams(dimension_semantics=sem, vmem_limit_bytes=VMEM_LIMIT)


def _resident(shape):
    zeros = (0,) * len(shape)
    return pl.BlockSpec(shape, lambda *_: zeros, pipeline_mode=pl.Buffered(1))


class _CastJob:
    def __init__(self, w, grid, col_tile=None, block_rows=None, pair_width=None):
        self.pair_width = pair_width
        self.stacked = w.ndim == 3
        self.e, self.k, self.f = w.shape if self.stacked else (1,) + w.shape
        self.src = w.reshape(self.e * self.k, self.f)
        steps = 1
        for g in grid:
            steps *= g
        if block_rows is None:
            assert (self.e * self.k) % steps == 0
            block_rows = self.e * self.k // steps
        assert (self.e * self.k) % block_rows == 0 and self.k % block_rows == 0
        self.block_rows = block_rows
        self.n_blocks = self.e * self.k // block_rows
        assert self.n_blocks <= steps, (self.n_blocks, steps)
        strides = [1] * len(grid)
        for a in range(len(grid) - 2, -1, -1):
            strides[a] = strides[a + 1] * grid[a + 1]
        self.strides = tuple(strides)
        self.col_tile = col_tile
        assert col_tile is None or self.f % col_tile == 0

    def _step(self, ids):
        return sum(i * s for i, s in zip(ids, self.strides))

    def _block(self, ids):
        return jnp.minimum(self._step(ids[:len(self.strides)]), self.n_blocks - 1)

    def src_spec(self):
        return pl.BlockSpec((self.block_rows, self.f), lambda *ids: (self._block(ids), 0))

    def _tiles(self):
        if self.pair_width is not None:
            return self.f // (2 * self.col_tile), 2 * self.col_tile
        return self.f // self.col_tile, self.col_tile

    def dst_spec(self):
        if self.col_tile is None:
            return self.src_spec()
        per_expert = self.k // self.block_rows
        n_tiles, width = self._tiles()
        return pl.BlockSpec(
            (None, n_tiles, self.block_rows, width),
            lambda *ids: (self._block(ids) // per_expert, 0, self._block(ids) % per_expert, 0))

    def out_shape(self):
        if self.col_tile is None:
            return jax.ShapeDtypeStruct(self.src.shape, BF16)
        n_tiles, width = self._tiles()
        return jax.ShapeDtypeStruct((self.e, n_tiles, self.k, width), BF16)

    def _cast_block(self, src_ref, dst_ref):
        ct, pw = self.col_tile, self.pair_width
        if ct is None:
            dst_ref[...] = src_ref[...].astype(BF16)
        elif pw is not None:
            half = self.f // 2
            for c in range(half // ct):
                for s in range(ct // pw):
                    a0 = c * ct + s * pw
                    dst_ref[c, :, 2 * s * pw:(2 * s + 1) * pw] = src_ref[:, a0:a0 + pw].astype(BF16)
                    dst_ref[c, :, (2 * s + 1) * pw:(2 * s + 2) * pw] = (
                        src_ref[:, half + a0:half + a0 + pw].astype(BF16))
        else:
            for c in range(self.f // ct):
                dst_ref[c] = src_ref[:, c * ct:(c + 1) * ct].astype(BF16)

    def run(self, src_ref, dst_ref):
        self._cast_block(src_ref, dst_ref)

    def result(self, dst):
        out = dst if self.col_tile is not None else dst.reshape(self.e, self.k, self.f)
        return out if self.stacked else out[0]


def _pallas_with_casts(kernel_fn, *, grid, in_specs, out_specs, out_shape, operands, casts=(),
                       scratch_shapes=(), name):
    jobs = [_CastJob(w, grid, *options) for (w, *options) in casts]
    n_in, n_out, n_jobs = len(in_specs), len(out_specs), len(jobs)

    def body(*refs):
        ins, srcs = refs[:n_in], refs[n_in:n_in + n_jobs]
        outs = refs[n_in + n_jobs:n_in + n_jobs + n_out]
        dsts = refs[n_in + n_jobs + n_out:n_in + 2 * n_jobs + n_out]
        for job, src, dst in zip(jobs, srcs, dsts):
            job.run(src, dst)
        kernel_fn(*ins, *outs, *refs[n_in + 2 * n_jobs + n_out:])

    results = pl.pallas_call(
        body,
        grid=grid,
        in_specs=list(in_specs) + [job.src_spec() for job in jobs],
        out_specs=list(out_specs) + [job.dst_spec() for job in jobs],
        out_shape=list(out_shape) + [job.out_shape() for job in jobs],
        scratch_shapes=list(scratch_shapes),
        compiler_params=_params(*(["arbitrary"] * len(grid))),
        name=name,
    )(*operands, *[job.src for job in jobs])
    return results[:n_out], [job.result(r) for job, r in zip(jobs, results[n_out:])]


def _rms(x, gain):
    ms = jnp.mean(x * x, axis=-1, keepdims=True)
    return x * lax.rsqrt(ms + EPS) * gain


def _qkv_kernel(x_ref, g_ref, w_ref, hg_ref, o_ref, *, n_norm_cols, col_chunk):
    h = _rms(x_ref[...], g_ref[...]).astype(BF16)
    n_out = o_ref.shape[1]
    for c0 in range(0, n_out, col_chunk):
        y = jnp.dot(h, w_ref[:, c0:c0 + col_chunk], preferred_element_type=F32)
        for h0 in range(0, col_chunk, HEAD_DIM):
            col = c0 + h0
            yh = y[:, h0:h0 + HEAD_DIM]
            if col < n_norm_cols:
                yh = _rms(yh, hg_ref[:, col:col + HEAD_DIM])
            o_ref[:, col:col + HEAD_DIM] = yh.astype(BF16)


def _qkv_proj(x, gain, w, head_gain, n_norm_cols, casts, tm=512):
    n, d = x.shape
    nq = w.shape[1]
    kern = functools.partial(_qkv_kernel, n_norm_cols=n_norm_cols, col_chunk=512)
    (qkv,), cast_out = _pallas_with_casts(
        kern,
        grid=(n // tm,),
        in_specs=[
            pl.BlockSpec((tm, d), lambda i: (i, 0)),
            _resident((1, d)),
            _resident((d, nq)),
            _resident((1, nq)),
        ],
        out_specs=[pl.BlockSpec((tm, nq), lambda i: (i, 0))],
        out_shape=[jax.ShapeDtypeStruct((n, nq), BF16)],
        operands=(x, gain, w, head_gain),
        casts=casts,
        name="qkv_proj",
    )
    return qkv, cast_out


def _attn_kernel(sink_ref, q_ref, k0_ref, k1_ref, k2_ref, k3_ref, v0_ref, v1_ref, v2_ref, v3_ref,
                 bias_ref, o_ref):
    j = pl.program_id(1)
    kcat = jnp.concatenate([k0_ref[...], k1_ref[...], k2_ref[...], k3_ref[...]], axis=0)
    vcat = jnp.concatenate([v0_ref[...], v1_ref[...], v2_ref[...], v3_ref[...]], axis=0)
    group = N_HEADS // N_KV_HEADS
    band = 3 * BLOCK
    variants = (jnp.where(j == 0, 0, 1), jnp.where(j == pl.num_programs(1) - 1, 2, 1))
    for a in range(2):
        rows = slice(a * BLOCK, (a + 1) * BLOCK)
        for kh in range(N_KV_HEADS):
            cols = slice(kh * HEAD_DIM, (kh + 1) * HEAD_DIM)
            k_h = kcat[a * BLOCK:a * BLOCK + band, cols]
            v_h = vcat[a * BLOCK:a * BLOCK + band, cols]
            heads = [kh * group + g for g in range(group)]
            q_g = jnp.concatenate([q_ref[rows, hd * HEAD_DIM:(hd + 1) * HEAD_DIM] for hd in heads], axis=0)
            sink = jnp.concatenate([jnp.full((1, BLOCK), sink_ref[hd], F32) for hd in heads], axis=1)
            s = lax.dot_general(k_h, q_g, (((1,), (1,)), ((), ())), preferred_element_type=F32)
            s = s + bias_ref[variants[a], kh * band:(kh + 1) * band, :]
            m = jnp.maximum(jnp.max(s, axis=0, keepdims=True), sink)
            p = jnp.exp(s - m)
            denom = jnp.sum(p, axis=0, keepdims=True) + jnp.exp(sink - m)
            o_t = lax.dot_general(v_h, p.astype(BF16), (((0,), (0,)), ((), ())),
                                  preferred_element_type=F32) * (1.0 / denom)
            for g, hd in enumerate(heads):
                o_ref[rows, hd * HEAD_DIM:(hd + 1) * HEAD_DIM] = (
                    o_t[:, g * BLOCK:(g + 1) * BLOCK].T.astype(BF16))


def _attention_bias():
    group = N_HEADS // N_KV_HEADS
    qi = np.arange(BLOCK)[:, None]
    sj = np.arange(3 * BLOCK)[None, :]
    dist = np.abs(qi - sj + BLOCK)
    slopes = np.exp2(-8.0 * np.arange(1, N_HEADS + 1, dtype=np.float64) / N_HEADS)
    bias = np.where(dist <= WINDOW, -slopes[:, None, None] * dist, NEG_INF)
    first = (sj >= BLOCK)[None]
    last = (sj < 2 * BLOCK)[None]
    edge = np.stack([np.where(first, bias, NEG_INF), bias, np.where(last, bias, NEG_INF)])
    edge = edge.reshape(3, N_KV_HEADS, group, BLOCK, 3 * BLOCK).transpose(0, 1, 4, 2, 3)
    return np.ascontiguousarray(edge.reshape(3, N_KV_HEADS * 3 * BLOCK, group * BLOCK), dtype=np.float32)


def _attention(qkv, sink, batch, seq, casts):
    n = qkv.shape[0]
    nb = seq // BLOCK
    assert nb % 2 == 0
    steps = nb // 2
    dq = N_HEADS * HEAD_DIM
    dkv = N_KV_HEADS * HEAD_DIM
    k_col = dq // dkv
    v_col = k_col + 1
    group = N_HEADS // N_KV_HEADS

    def kv_spec(offset, col):
        return pl.BlockSpec(
            (BLOCK, dkv), lambda b, j: (b * nb + jnp.clip(2 * j + offset, 0, nb - 1), col))

    (attn,), cast_out = _pallas_with_casts(
        _attn_kernel,
        grid=(batch, steps),
        in_specs=[
            pl.BlockSpec(memory_space=pltpu.SMEM),
            pl.BlockSpec((2 * BLOCK, dq), lambda b, j: (b * steps + j, 0)),
            kv_spec(-1, k_col), kv_spec(0, k_col), kv_spec(1, k_col), kv_spec(2, k_col),
            kv_spec(-1, v_col), kv_spec(0, v_col), kv_spec(1, v_col), kv_spec(2, v_col),
            _resident((3, N_KV_HEADS * 3 * BLOCK, group * BLOCK)),
        ],
        out_specs=[pl.BlockSpec((2 * BLOCK, dq), lambda b, j: (b * steps + j, 0))],
        out_shape=[jax.ShapeDtypeStruct((n, dq), BF16)],
        operands=(sink, qkv, qkv, qkv, qkv, qkv, qkv, qkv, qkv, qkv, _attention_bias()),
        casts=casts,
        name="window_attention",
    )
    return attn, cast_out


def _proj_res_kernel(a_ref, w_ref, r_ref, o_ref):
    o_ref[...] = r_ref[...] + jnp.dot(a_ref[...], w_ref[...], preferred_element_type=F32)


def _proj_residual(a, w, res, casts, tm=512, name="proj_residual"):
    n, k = a.shape
    d = w.shape[1]
    (out,), cast_out = _pallas_with_casts(
        _proj_res_kernel,
        grid=(n // tm,),
        in_specs=[
            pl.BlockSpec((tm, k), lambda i: (i, 0)),
            _resident((k, d)),
            pl.BlockSpec((tm, d), lambda i: (i, 0)),
        ],
        out_specs=[pl.BlockSpec((tm, d), lambda i: (i, 0))],
        out_shape=[jax.ShapeDtypeStruct((n, d), F32)],
        operands=(a, w, res),
        casts=casts,
        name=name,
    )
    return out, cast_out


def _swiglu_paired(h, wgu_ref):
    gu = jnp.dot(h, wgu_ref[...], preferred_element_type=F32)
    acts = []
    for c0 in range(0, gu.shape[1], 2 * MXU_COLS):
        gt = gu[:, c0:c0 + MXU_COLS]
        up = gu[:, c0 + MXU_COLS:c0 + 2 * MXU_COLS]
        acts.append((gt * jax.nn.sigmoid(gt) * up).astype(BF16))
    return jnp.concatenate(acts, axis=1)


def _dense_ffn_kernel(x_ref, g_ref, wgu_ref, wd_ref, o_ref, h_ref):
    @pl.when(pl.program_id(1) == 0)
    def _():
        x = x_ref[...]
        h_ref[...] = _rms(x, g_ref[...]).astype(BF16)
        o_ref[...] = x

    act = _swiglu_paired(h_ref[...], wgu_ref)
    o_ref[...] += jnp.dot(act, wd_ref[...], preferred_element_type=F32)


def _dense_ffn(x, gain, w_gate_up, w_down, casts, tm=512):
    n, d = x.shape
    f = w_down.shape[0]
    nf = w_gate_up.shape[0]
    tf = f // nf
    (out,), cast_out = _pallas_with_casts(
        _dense_ffn_kernel,
        grid=(n // tm, nf),
        in_specs=[
            pl.BlockSpec((tm, d), lambda i, j: (i, 0)),
            _resident((1, d)),
            pl.BlockSpec((None, d, 2 * tf), lambda i, j: (j, 0, 0)),
            pl.BlockSpec((tf, d), lambda i, j: (j, 0)),
        ],
        out_specs=[pl.BlockSpec((tm, d), lambda i, j: (i, 0))],
        out_shape=[jax.ShapeDtypeStruct((n, d), F32)],
        operands=(x, gain, w_gate_up, w_down),
        casts=casts,
        scratch_shapes=[pltpu.VMEM((tm, d), BF16)],
        name="dense_ffn",
    )
    return out, cast_out


def _gmlp_in_kernel(x_ref, g_ref, w_ref, vg_ref, u_ref, v_ref, vf_ref, *, col_chunk):
    h = _rms(x_ref[...], g_ref[...]).astype(BF16)
    width = u_ref.shape[1]
    ssq = jnp.zeros((x_ref.shape[0], 1), F32)
    for c0 in range(0, width, col_chunk):
        z = jnp.dot(h, w_ref[:, width + c0:width + c0 + col_chunk], preferred_element_type=F32)
        v = jax.nn.gelu(z)
        ssq = ssq + jnp.sum(v * v, axis=-1, keepdims=True)
        vf_ref[:, c0:c0 + col_chunk] = v
    inv = lax.rsqrt(ssq / width + EPS)
    v_ref[...] = (vf_ref[...] * inv * vg_ref[...]).astype(BF16)
    for c0 in range(0, width, col_chunk):
        z = jnp.dot(h, w_ref[:, c0:c0 + col_chunk], preferred_element_type=F32)
        u_ref[:, c0:c0 + col_chunk] = jax.nn.gelu(z).astype(BF16)


def _gmlp_in(x, gain, w_in, v_gain, casts, tm=256):
    n, d = x.shape
    width = w_in.shape[1] // 2
    kern = functools.partial(_gmlp_in_kernel, col_chunk=2048)
    (u, v), cast_out = _pallas_with_casts(
        kern,
        grid=(n // tm,),
        in_specs=[
            pl.BlockSpec((tm, d), lambda i: (i, 0)),
            _resident((1, d)),
            _resident((d, 2 * width)),
            _resident((1, width)),
        ],
        out_specs=[
            pl.BlockSpec((tm, width), lambda i: (i, 0)),
            pl.BlockSpec((tm, width), lambda i: (i, 0)),
        ],
        out_shape=[
            jax.ShapeDtypeStruct((n, width), BF16),
            jax.ShapeDtypeStruct((n, width), BF16),
        ],
        operands=(x, gain, w_in, v_gain),
        casts=casts,
        scratch_shapes=[pltpu.VMEM((tm, width), F32)],
        name="gmlp_in",
    )
    return u, v, cast_out


def _top2_route(logits):
    lane = lax.broadcasted_iota(jnp.int32, logits.shape, 1)
    lg = jnp.where(lane < N_EXPERTS, logits, -jnp.inf)
    m1 = jnp.max(lg, axis=-1, keepdims=True)
    i1 = jnp.min(jnp.where(lg == m1, lane, LANES), axis=-1, keepdims=True)
    lg2 = jnp.where(lane == i1, -jnp.inf, lg)
    m2 = jnp.max(lg2, axis=-1, keepdims=True)
    i2 = jnp.min(jnp.where(lg2 == m2, lane, LANES), axis=-1, keepdims=True)
    e2 = jnp.exp(m2 - m1)
    w1 = 1.0 / (1.0 + e2)
    w2 = e2 / (1.0 + e2)
    idx = jnp.where(lane == 0, i1, jnp.where(lane == 1, i2, 0))
    gate = jnp.where(lane == 0, w1, jnp.where(lane == 1, w2, 0.0))
    return idx, gate


def _gmlp_out_router_kernel(u_ref, v_ref, ws_ref, bs_ref, wo_ref, r_ref, g_ref, wr_ref,
                            x_ref, h_ref, idx_ref, gate_ref, y_ref):
    tm, width = u_ref.shape
    for c0 in range(0, tm, CHUNK):
        for g in range(width // LANES):
            cols = slice(g * LANES, (g + 1) * LANES)
            vv = v_ref[c0:c0 + CHUNK, cols]
            mixed = jnp.dot(ws_ref[g], vv, preferred_element_type=F32) + bs_ref[:, cols]
            y_ref[c0:c0 + CHUNK, cols] = (u_ref[c0:c0 + CHUNK, cols].astype(F32) * mixed).astype(BF16)
    x = r_ref[...] + jnp.dot(y_ref[...], wo_ref[...], preferred_element_type=F32)
    x_ref[...] = x
    h = _rms(x, g_ref[...])
    h_ref[...] = h
    h_hi = h.astype(BF16)
    h_lo = (h - h_hi.astype(F32)).astype(BF16)
    a = jnp.dot(h_hi, wr_ref[...], preferred_element_type=F32)
    b = jnp.dot(h_lo, wr_ref[:, 0:LANES], preferred_element_type=F32)
    logits = a[:, 0:LANES] + (a[:, LANES:2 * LANES] + b)
    idx_ref[...], gate_ref[...] = _top2_route(logits)


def _gmlp_out_router(u, v, w_s, b_full, w_out, res, ffn_gain, w_router_split, tm=512):
    n, width = u.shape
    d = w_out.shape[1]
    groups = w_s.shape[0]
    row = lambda i: (i, 0)
    return pl.pallas_call(
        _gmlp_out_router_kernel,
        grid=(n // tm,),
        in_specs=[
            pl.BlockSpec((tm, width), row),
            pl.BlockSpec((tm, width), row),
            _resident((groups, CHUNK, CHUNK)),
            _resident((CHUNK, width)),
            _resident((width, d)),
            pl.BlockSpec((tm, d), row),
            _resident((1, d)),
            _resident((d, 2 * LANES)),
        ],
        out_specs=[
            pl.BlockSpec((tm, d), row),
            pl.BlockSpec((tm, d), row),
            pl.BlockSpec((tm, LANES), row),
            pl.BlockSpec((tm, LANES), row),
        ],
        out_shape=[
            jax.ShapeDtypeStruct((n, d), F32),
            jax.ShapeDtypeStruct((n, d), F32),
            jax.ShapeDtypeStruct((n, LANES), jnp.int32),
            jax.ShapeDtypeStruct((n, LANES), F32),
        ],
        scratch_shapes=[pltpu.VMEM((tm, width), BF16)],
        compiler_params=_params("parallel"),
        name="gmlp_out_router",
    )(u, v, w_s, b_full, w_out, res, ffn_gain, w_router_split)


def _moe_ffn_kernel(te_ref, nv_ref, tok_ref, h_hbm, wg_ref, wu_ref, wd_ref, o_ref,
                    xbuf, hb_ref, sem, *, rows_per_step, row_paths):
    i = pl.program_id(0)
    j = pl.program_id(1)
    n_tiles = pl.num_programs(0)
    nf = pl.num_programs(1)
    tm = hb_ref.shape[0]
    rows_buf = xbuf.shape[0]

    def row_copy(tile, r):
        tok = tok_ref[tile * tm + r]
        return pltpu.make_async_copy(h_hbm.at[pl.ds(tok, 1), :], xbuf.at[pl.ds(r, 1), :], sem)

    def wait_rows():
        pltpu.make_async_copy(h_hbm.at[pl.ds(0, rows_buf), :], xbuf, sem).wait()

    def prefetch_next_tile():
        for k in range(rows_per_step):
            row_copy(i + 1, j * rows_per_step + k).start(priority=1)

    @pl.when((i == 0) & (j == 0))
    def _():
        def issue(r, carry):
            row_copy(0, r).start()
            return carry
        lax.fori_loop(0, rows_buf, issue, 0)

    @pl.when(j == 0)
    def _():
        wait_rows()
        hb_ref[...] = xbuf[0:tm, :].astype(BF16)
        o_ref[...] = jnp.zeros_like(o_ref)

    def swiglu(rows):
        h = hb_ref[0:rows, :]
        gt = jnp.dot(h, wg_ref[...], preferred_element_type=F32)
        up = jnp.dot(h, wu_ref[...], preferred_element_type=F32)
        act = (gt * jax.nn.sigmoid(gt) * up).astype(BF16)
        o_ref[0:rows, :] += jnp.dot(act, wd_ref[...], preferred_element_type=F32)

    n_valid = nv_ref[i]

    for rows, below in zip(row_paths, row_paths[1:] + (0,)):
        @pl.when((n_valid > below) & (n_valid <= rows))
        def _(rows=rows):
            prefetch_next_tile()
            swiglu(rows)

    @pl.when(n_valid == 0)
    def _():
        prefetch_next_tile()

    @pl.when((i == n_tiles - 1) & (j == nf - 1))
    def _():
        wait_rows()


def _moe_ffn(h, tile_expert, tile_valid, row_token, we_gate, we_up, we_down, row_paths):
    tm = row_paths[0]
    n, d = h.shape
    nf, tf = we_gate.shape[1], we_gate.shape[3]
    n_tiles = tile_expert.shape[0]
    p = n_tiles * tm
    rows_per_step = -(-tm // (nf * SUBLANES)) * SUBLANES
    rows_buf = rows_per_step * nf
    row_token = jnp.pad(row_token, (0, tm + rows_buf))

    def f_tile(i, j, nv):
        return jnp.where(nv[i] > 0, j, nf - 1)

    def w_in_map(i, j, te, nv, tok):
        return (te[i], f_tile(i, j, nv), 0, 0)

    def w_out_map(i, j, te, nv, tok):
        return (te[i], f_tile(i, j, nv), 0)

    grid_spec = pltpu.PrefetchScalarGridSpec(
        num_scalar_prefetch=3,
        grid=(n_tiles, nf),
        in_specs=[
            pl.BlockSpec(memory_space=pl.ANY),
            pl.BlockSpec((None, None, d, tf), w_in_map),
            pl.BlockSpec((None, None, d, tf), w_in_map),
            pl.BlockSpec((None, tf, d), w_out_map),
        ],
        out_specs=pl.BlockSpec((tm, d), lambda i, j, te, nv, tok: (i, 0)),
        scratch_shapes=[
            pltpu.VMEM((rows_buf, d), F32),
            pltpu.VMEM((tm, d), BF16),
            pltpu.SemaphoreType.DMA(()),
        ],
    )
    return pl.pallas_call(
        functools.partial(_moe_ffn_kernel, rows_per_step=rows_per_step, row_paths=tuple(row_paths)),
        grid_spec=grid_spec,
        out_shape=jax.ShapeDtypeStruct((p, d), F32),
        compiler_params=_params("arbitrary", "arbitrary"),
        name="moe_ffn",
    )(tile_expert, tile_valid, row_token, h, we_gate, we_up, we_down)


def _moe_combine_kernel(pos_ref, x_ref, gate_ref, y_hbm, o_ref, buf, sems):
    i = pl.program_id(0)
    tm = x_ref.shape[0]
    slot = i % 2

    def row_copies(tile, r, dst_slot):
        for k in range(TOP_K):
            p = pos_ref[(tile * tm + r) * TOP_K + k]
            pltpu.make_async_copy(y_hbm.at[pl.ds(p, 1), :], buf.at[dst_slot, k, pl.ds(r, 1), :],
                                  sems.at[dst_slot]).start(priority=k)

    def wait_slot(s):
        for k in range(TOP_K):
            pltpu.make_async_copy(y_hbm.at[pl.ds(0, tm), :], buf.at[s, k], sems.at[s]).wait()

    @pl.when(i == 0)
    def _():
        def issue(r, carry):
            row_copies(0, r, 0)
            return carry
        lax.fori_loop(0, tm, issue, 0, unroll=8)

    wait_slot(slot)
    chunk = 4 * SUBLANES
    for r0 in range(0, tm, chunk):
        for r in range(r0, r0 + chunk):
            row_copies(i + 1, r, 1 - slot)
        rows = slice(r0, r0 + chunk)
        gate = gate_ref[rows, :]
        o_ref[rows, :] = x_ref[rows, :] + (gate[:, 0:1] * buf[slot, 0, rows, :]
                                           + gate[:, 1:2] * buf[slot, 1, rows, :])

    @pl.when(i == pl.num_programs(0) - 1)
    def _():
        wait_slot(1 - slot)


def _moe_combine(x, gate, y_sorted, pos, tm=256):
    n, d = x.shape
    pos = jnp.pad(pos, (0, TOP_K * tm))
    grid_spec = pltpu.PrefetchScalarGridSpec(
        num_scalar_prefetch=1,
        grid=(n // tm,),
        in_specs=[
            pl.BlockSpec((tm, d), lambda i, pos: (i, 0)),
            pl.BlockSpec((tm, LANES), lambda i, pos: (i, 0)),
            pl.BlockSpec(memory_space=pl.ANY),
        ],
        out_specs=pl.BlockSpec((tm, d), lambda i, pos: (i, 0)),
        scratch_shapes=[
            pltpu.VMEM((2, TOP_K, tm, d), F32),
            pltpu.SemaphoreType.DMA((2,)),
        ],
    )
    return pl.pallas_call(
        _moe_combine_kernel,
        grid_spec=grid_spec,
        out_shape=jax.ShapeDtypeStruct((n, d), F32),
        compiler_params=_params("arbitrary"),
        name="moe_combine",
    )(pos, x, gate, y_sorted)


def _route_plan(top_idx, tm, ext):
    n = top_idx.shape[0]
    n_pairs = n * TOP_K
    stride = tm + ext
    n_tiles = (n_pairs + N_EXPERTS * (tm - 1)) // tm
    hot = top_idx[:, :, None] == jnp.arange(N_EXPERTS, dtype=jnp.int32)[None, None, :]
    per_token = jnp.sum(hot, axis=1, dtype=jnp.int32)
    csum = jnp.cumsum(per_token, axis=0)
    counts = csum[-1]
    full, rem = counts // tm, counts % tm
    own_tile = (rem > ext) | ((rem > 0) & (full == 0))
    tiles_per = full + own_tile.astype(jnp.int32)
    tile_end = jnp.cumsum(tiles_per)
    tile_first = tile_end - tiles_per
    rank = csum - per_token
    in_tile = jnp.minimum(rank // tm, tiles_per[None, :] - 1)
    slot = (tile_first[None, :] + in_tile) * stride + (rank - in_tile * tm)
    pos = jnp.sum(jnp.where(hot, slot[:, None, :], 0), axis=2).reshape(n_pairs).astype(jnp.int32)
    tile_ids = jnp.arange(n_tiles, dtype=jnp.int32)
    tile_expert = jnp.sum((tile_end[None, :] <= tile_ids[:, None]).astype(jnp.int32), axis=1)
    tile_expert = jnp.minimum(tile_expert, N_EXPERTS - 1).astype(jnp.int32)
    in_expert = tile_ids - tile_first[tile_expert]
    is_last = in_expert == tiles_per[tile_expert] - 1
    tile_valid = jnp.where(is_last, counts[tile_expert] - in_expert * tm, tm)
    tile_valid = jnp.where(tile_ids < tile_end[-1], tile_valid, 0).astype(jnp.int32)
    row_token = jnp.zeros((n_tiles * stride,), jnp.int32).at[pos].set(
        jnp.arange(n_pairs, dtype=jnp.int32) // TOP_K)
    return tile_expert, tile_valid, row_token, pos


def kernel(x, l0_mix_norm, l0_w_qkv, l0_q_norm, l0_k_norm, l0_sink, l0_w_o, l0_ffn_norm, l0_w_gate_up, l0_w_down, l1_mix_norm, l1_w_in, l1_v_norm, l1_w_s, l1_b_s, l1_w_out, l1_ffn_norm, l1_w_router, l1_we_gate, l1_we_up, l1_we_down):
    batch, seq, d = x.shape
    n = batch * seq
    x0 = x.reshape(n, d)
    dq = N_HEADS * HEAD_DIM
    dkv = N_KV_HEADS * HEAD_DIM

    head_gain = jnp.concatenate([
        jnp.tile(l0_q_norm * (HEAD_DIM ** -0.5), N_HEADS),
        jnp.tile(l0_k_norm, N_KV_HEADS),
        jnp.ones((dkv,), F32),
    ])[None, :]
    qkv, (w_gate_up, w_o, w_down) = _qkv_proj(
        x0, l0_mix_norm[None, :], l0_w_qkv.astype(BF16), head_gain, dq + dkv,
        casts=[(l0_w_gate_up, DENSE_TF, None, MXU_COLS), (l0_w_o, None, None), (l0_w_down, None, None)])
    attn, (we_gate,) = _attention(qkv, l0_sink, batch, seq, casts=[(l1_we_gate, MOE_TF, None)])
    x1, _ = _proj_residual(attn, w_o, x0, casts=[], name="attn_out_proj")

    x2, (we_down, w_in, w_out) = _dense_ffn(
        x1, l0_ffn_norm[None, :], w_gate_up, w_down,
        casts=[(l1_we_down, None, 256), (l1_w_in, None, 16), (l1_w_out, None, 16)])

    u, v, (we_up,) = _gmlp_in(x2, l1_mix_norm[None, :], w_in, l1_v_norm[None, :],
                              casts=[(l1_we_up, MOE_TF, None)])
    b_full = jnp.repeat(l1_b_s.T, LANES, axis=1)
    w_router = jnp.pad(l1_w_router, ((0, 0), (0, LANES - N_EXPERTS)))
    w_router_hi = w_router.astype(BF16)
    w_router_lo = (w_router - w_router_hi.astype(F32)).astype(BF16)
    w_router_split = jnp.concatenate([w_router_hi, w_router_lo], axis=1)
    x3, h, ridx, rgate = _gmlp_out_router(u, v, l1_w_s.astype(BF16), b_full, w_out, x2,
                                          l1_ffn_norm[None, :], w_router_split)

    tile_expert, tile_valid, row_token, pos = _route_plan(ridx[:, :TOP_K], MOE_TM, MOE_TM_EXT)
    y_sorted = _moe_ffn(h, tile_expert, tile_valid, row_token, we_gate, we_up, we_down,
                        row_paths=(MOE_TM + MOE_TM_EXT, MOE_TM, MOE_TM_EXT))
    out = _moe_combine(x3, rgate, y_sorted, pos)
    return out.reshape(batch, seq, d)
```

```python
import functools

import jax
import jax.numpy as jnp
import numpy as np
from jax import lax
from jax.experimental import pallas as pl
from jax.experimental.pallas import tpu as pltpu

F32 = jnp.float32
BF16 = jnp.bfloat16

EPS = 1e-6
NEG_INF = -1e30
LANES = 128
SUBLANES = 8
HEAD_DIM = 128
N_HEADS = 16
N_KV_HEADS = 4
WINDOW = 128
BLOCK = 128
CHUNK = 128
N_EXPERTS = 8
TOP_K = 2
MOE_TF = 512
MOE_TM = 512
MXU_COLS = 256
DENSE_TF = 512

MIB = 1024 * 1024
VMEM_LIMIT = 56 * MIB


def _params(*sem):
    return pltpu.CompilerParams(dimension_semantics=sem, vmem_limit_bytes=VMEM_LIMIT)


def _resident(shape):
    zeros = (0,) * len(shape)
    return pl.BlockSpec(shape, lambda *_: zeros, pipeline_mode=pl.Buffered(1))


class _CastJob:
    def __init__(self, w, grid, col_tile=None, block_rows=None, pair_width=None):
        self.pair_width = pair_width
        self.stacked = w.ndim == 3
        self.e, self.k, self.f = w.shape if self.stacked else (1,) + w.shape
        self.src = w.reshape(self.e * self.k, self.f)
        steps = 1
        for g in grid:
            steps *= g
        if block_rows is None:
            assert (self.e * self.k) % steps == 0
            block_rows = self.e * self.k // steps
        assert (self.e * self.k) % block_rows == 0 and self.k % block_rows == 0
        self.block_rows = block_rows
        self.n_blocks = self.e * self.k // block_rows
        assert self.n_blocks <= steps, (self.n_blocks, steps)
        strides = [1] * len(grid)
        for a in range(len(grid) - 2, -1, -1):
            strides[a] = strides[a + 1] * grid[a + 1]
        self.strides = tuple(strides)
        self.col_tile = col_tile
        assert col_tile is None or self.f % col_tile == 0

    def _step(self, ids):
        return sum(i * s for i, s in zip(ids, self.strides))

    def _block(self, ids):
        return jnp.minimum(self._step(ids[:len(self.strides)]), self.n_blocks - 1)

    def src_spec(self):
        return pl.BlockSpec((self.block_rows, self.f), lambda *ids: (self._block(ids), 0))

    def _tiles(self):
        if self.pair_width is not None:
            return self.f // (2 * self.col_tile), 2 * self.col_tile
        return self.f // self.col_tile, self.col_tile

    def dst_spec(self):
        if self.col_tile is None:
            return self.src_spec()
        per_expert = self.k // self.block_rows
        n_tiles, width = self._tiles()
        return pl.BlockSpec(
            (None, n_tiles, self.block_rows, width),
            lambda *ids: (self._block(ids) // per_expert, 0, self._block(ids) % per_expert, 0))

    def out_shape(self):
        if self.col_tile is None:
            return jax.ShapeDtypeStruct(self.src.shape, BF16)
        n_tiles, width = self._tiles()
        return jax.ShapeDtypeStruct((self.e, n_tiles, self.k, width), BF16)

    def _cast_block(self, src_ref, dst_ref):
        ct, pw = self.col_tile, self.pair_width
        if ct is None:
            dst_ref[...] = src_ref[...].astype(BF16)
        elif pw is not None:
            half = self.f // 2
            for c in range(half // ct):
                for s in range(ct // pw):
                    a0 = c * ct + s * pw
                    dst_ref[c, :, 2 * s * pw:(2 * s + 1) * pw] = src_ref[:, a0:a0 + pw].astype(BF16)
                    dst_ref[c, :, (2 * s + 1) * pw:(2 * s + 2) * pw] = (
                        src_ref[:, half + a0:half + a0 + pw].astype(BF16))
        else:
            for c in range(self.f // ct):
                dst_ref[c] = src_ref[:, c * ct:(c + 1) * ct].astype(BF16)

    def run(self, src_ref, dst_ref):
        self._cast_block(src_ref, dst_ref)

    def result(self, dst):
        out = dst if self.col_tile is not None else dst.reshape(self.e, self.k, self.f)
        return out if self.stacked else out[0]


def _pallas_with_casts(kernel_fn, *, grid, in_specs, out_specs, out_shape, operands, casts=(),
                       scratch_shapes=(), name):
    jobs = [_CastJob(w, grid, *options) for (w, *options) in casts]
    n_in, n_out, n_jobs = len(in_specs), len(out_specs), len(jobs)

    def body(*refs):
        ins, srcs = refs[:n_in], refs[n_in:n_in + n_jobs]
        outs = refs[n_in + n_jobs:n_in + n_jobs + n_out]
        dsts = refs[n_in + n_jobs + n_out:n_in + 2 * n_jobs + n_out]
        for job, src, dst in zip(jobs, srcs, dsts):
            job.run(src, dst)
        kernel_fn(*ins, *outs, *refs[n_in + 2 * n_jobs + n_out:])

    results = pl.pallas_call(
        body,
        grid=grid,
        in_specs=list(in_specs) + [job.src_spec() for job in jobs],
        out_specs=list(out_specs) + [job.dst_spec() for job in jobs],
        out_shape=list(out_shape) + [job.out_shape() for job in jobs],
        scratch_shapes=list(scratch_shapes),
        compiler_params=_params(*(["arbitrary"] * len(grid))),
        name=name,
    )(*operands, *[job.src for job in jobs])
    return results[:n_out], [job.result(r) for job, r in zip(jobs, results[n_out:])]


def _rms(x, gain):
    ms = jnp.mean(x * x, axis=-1, keepdims=True)
    return x * lax.rsqrt(ms + EPS) * gain


def _qkv_kernel(x_ref, g_ref, w_ref, hg_ref, o_ref, *, n_norm_cols, col_chunk):
    h = _rms(x_ref[...], g_ref[...]).astype(BF16)
    n_out = o_ref.shape[1]
    for c0 in range(0, n_out, col_chunk):
        y = jnp.dot(h, w_ref[:, c0:c0 + col_chunk], preferred_element_type=F32)
        for h0 in range(0, col_chunk, HEAD_DIM):
            col = c0 + h0
            yh = y[:, h0:h0 + HEAD_DIM]
            if col < n_norm_cols:
                yh = _rms(yh, hg_ref[:, col:col + HEAD_DIM])
            o_ref[:, col:col + HEAD_DIM] = yh.astype(BF16)


def _qkv_proj(x, gain, w, head_gain, n_norm_cols, casts, tm=512):
    n, d = x.shape
    nq = w.shape[1]
    kern = functools.partial(_qkv_kernel, n_norm_cols=n_norm_cols, col_chunk=512)
    (qkv,), cast_out = _pallas_with_casts(
        kern,
        grid=(n // tm,),
        in_specs=[
            pl.BlockSpec((tm, d), lambda i: (i, 0)),
            _resident((1, d)),
            _resident((d, nq)),
            _resident((1, nq)),
        ],
        out_specs=[pl.BlockSpec((tm, nq), lambda i: (i, 0))],
        out_shape=[jax.ShapeDtypeStruct((n, nq), BF16)],
        operands=(x, gain, w, head_gain),
        casts=casts,
        name="qkv_proj",
    )
    return qkv, cast_out


def _attn_kernel(sink_ref, q_ref, k0_ref, k1_ref, k2_ref, k3_ref, v0_ref, v1_ref, v2_ref, v3_ref,
                 bias_ref, o_ref):
    j = pl.program_id(1)
    kcat = jnp.concatenate([k0_ref[...], k1_ref[...], k2_ref[...], k3_ref[...]], axis=0)
    vcat = jnp.concatenate([v0_ref[...], v1_ref[...], v2_ref[...], v3_ref[...]], axis=0)
    group = N_HEADS // N_KV_HEADS
    band = 3 * BLOCK
    variants = (jnp.where(j == 0, 0, 1), jnp.where(j == pl.num_programs(1) - 1, 2, 1))
    for a in range(2):
        rows = slice(a * BLOCK, (a + 1) * BLOCK)
        for kh in range(N_KV_HEADS):
            cols = slice(kh * HEAD_DIM, (kh + 1) * HEAD_DIM)
            k_h = kcat[a * BLOCK:a * BLOCK + band, cols]
            v_h = vcat[a * BLOCK:a * BLOCK + band, cols]
            heads = [kh * group + g for g in range(group)]
            q_g = jnp.concatenate([q_ref[rows, hd * HEAD_DIM:(hd + 1) * HEAD_DIM] for hd in heads], axis=0)
            sink = jnp.concatenate([jnp.full((1, BLOCK), sink_ref[hd], F32) for hd in heads], axis=1)
            s = lax.dot_general(k_h, q_g, (((1,), (1,)), ((), ())), preferred_element_type=F32)
            s = s + bias_ref[variants[a], kh * band:(kh + 1) * band, :]
            m = jnp.maximum(jnp.max(s, axis=0, keepdims=True), sink)
            p = jnp.exp(s - m)
            denom = jnp.sum(p, axis=0, keepdims=True) + jnp.exp(sink - m)
            o_t = lax.dot_general(v_h, p.astype(BF16), (((0,), (0,)), ((), ())),
                                  preferred_element_type=F32) * (1.0 / denom)
            for g, hd in enumerate(heads):
                o_ref[rows, hd * HEAD_DIM:(hd + 1) * HEAD_DIM] = (
                    o_t[:, g * BLOCK:(g + 1) * BLOCK].T.astype(BF16))


def _attention_bias():
    group = N_HEADS // N_KV_HEADS
    qi = np.arange(BLOCK)[:, None]
    sj = np.arange(3 * BLOCK)[None, :]
    dist = np.abs(qi - sj + BLOCK)
    slopes = np.exp2(-8.0 * np.arange(1, N_HEADS + 1, dtype=np.float64) / N_HEADS)
    bias = np.where(dist <= WINDOW, -slopes[:, None, None] * dist, NEG_INF)
    first = (sj >= BLOCK)[None]
    last = (sj < 2 * BLOCK)[None]
    edge = np.stack([np.where(first, bias, NEG_INF), bias, np.where(last, bias, NEG_INF)])
    edge = edge.reshape(3, N_KV_HEADS, group, BLOCK, 3 * BLOCK).transpose(0, 1, 4, 2, 3)
    return np.ascontiguousarray(edge.reshape(3, N_KV_HEADS * 3 * BLOCK, group * BLOCK), dtype=np.float32)


def _attention(qkv, sink, batch, seq, casts):
    n = qkv.shape[0]
    nb = seq // BLOCK
    assert nb % 2 == 0
    steps = nb // 2
    dq = N_HEADS * HEAD_DIM
    dkv = N_KV_HEADS * HEAD_DIM
    k_col = dq // dkv
    v_col = k_col + 1
    group = N_HEADS // N_KV_HEADS

    def kv_spec(offset, col):
        return pl.BlockSpec(
            (BLOCK, dkv), lambda b, j: (b * nb + jnp.clip(2 * j + offset, 0, nb - 1), col))

    (attn,), cast_out = _pallas_with_casts(
        _attn_kernel,
        grid=(batch, steps),
        in_specs=[
            pl.BlockSpec(memory_space=pltpu.SMEM),
            pl.BlockSpec((2 * BLOCK, dq), lambda b, j: (b * steps + j, 0)),
            kv_spec(-1, k_col), kv_spec(0, k_col), kv_spec(1, k_col), kv_spec(2, k_col),
            kv_spec(-1, v_col), kv_spec(0, v_col), kv_spec(1, v_col), kv_spec(2, v_col),
            _resident((3, N_KV_HEADS * 3 * BLOCK, group * BLOCK)),
        ],
        out_specs=[pl.BlockSpec((2 * BLOCK, dq), lambda b, j: (b * steps + j, 0))],
        out_shape=[jax.ShapeDtypeStruct((n, dq), BF16)],
        operands=(sink, qkv, qkv, qkv, qkv, qkv, qkv, qkv, qkv, qkv, _attention_bias()),
        casts=casts,
        name="window_attention",
    )
    return attn, cast_out


def _proj_res_kernel(a_ref, w_ref, r_ref, o_ref):
    o_ref[...] = r_ref[...] + jnp.dot(a_ref[...], w_ref[...], preferred_element_type=F32)


def _proj_residual(a, w, res, casts, tm=512, name="proj_residual"):
    n, k = a.shape
    d = w.shape[1]
    (out,), cast_out = _pallas_with_casts(
        _proj_res_kernel,
        grid=(n // tm,),
        in_specs=[
            pl.BlockSpec((tm, k), lambda i: (i, 0)),
            _resident((k, d)),
            pl.BlockSpec((tm, d), lambda i: (i, 0)),
        ],
        out_specs=[pl.BlockSpec((tm, d), lambda i: (i, 0))],
        out_shape=[jax.ShapeDtypeStruct((n, d), F32)],
        operands=(a, w, res),
        casts=casts,
        name=name,
    )
    return out, cast_out


def _swiglu_paired(h, wgu_ref):
    gu = jnp.dot(h, wgu_ref[...], preferred_element_type=F32)
    acts = []
    for c0 in range(0, gu.shape[1], 2 * MXU_COLS):
        gt = gu[:, c0:c0 + MXU_COLS]
        up = gu[:, c0 + MXU_COLS:c0 + 2 * MXU_COLS]
        acts.append((gt * jax.nn.sigmoid(gt) * up).astype(BF16))
    return jnp.concatenate(acts, axis=1)


def _dense_ffn_kernel(x_ref, g_ref, wgu_ref, wd_ref, o_ref, h_ref):
    @pl.when(pl.program_id(1) == 0)
    def _():
        x = x_ref[...]
        h_ref[...] = _rms(x, g_ref[...]).astype(BF16)
        o_ref[...] = x

    act = _swiglu_paired(h_ref[...], wgu_ref)
    o_ref[...] += jnp.dot(act, wd_ref[...], preferred_element_type=F32)


def _dense_ffn(x, gain, w_gate_up, w_down, casts, tm=512):
    n, d = x.shape
    f = w_down.shape[0]
    nf = w_gate_up.shape[0]
    tf = f // nf
    (out,), cast_out = _pallas_with_casts(
        _dense_ffn_kernel,
        grid=(n // tm, nf),
        in_specs=[
            pl.BlockSpec((tm, d), lambda i, j: (i, 0)),
            _resident((1, d)),
            pl.BlockSpec((None, d, 2 * tf), lambda i, j: (j, 0, 0)),
            pl.BlockSpec((tf, d), lambda i, j: (j, 0)),
        ],
        out_specs=[pl.BlockSpec((tm, d), lambda i, j: (i, 0))],
        out_shape=[jax.ShapeDtypeStruct((n, d), F32)],
        operands=(x, gain, w_gate_up, w_down),
        casts=casts,
        scratch_shapes=[pltpu.VMEM((tm, d), BF16)],
        name="dense_ffn",
    )
    return out, cast_out


def _gmlp_in_kernel(x_ref, g_ref, w_ref, vg_ref, u_ref, v_ref, vf_ref, *, col_chunk):
    h = _rms(x_ref[...], g_ref[...]).astype(BF16)
    width = u_ref.shape[1]
    ssq = jnp.zeros((x_ref.shape[0], 1), F32)
    for c0 in range(0, width, col_chunk):
        z = jnp.dot(h, w_ref[:, width + c0:width + c0 + col_chunk], preferred_element_type=F32)
        v = jax.nn.gelu(z)
        ssq = ssq + jnp.sum(v * v, axis=-1, keepdims=True)
        vf_ref[:, c0:c0 + col_chunk] = v
    inv = lax.rsqrt(ssq / width + EPS)
    v_ref[...] = (vf_ref[...] * inv * vg_ref[...]).astype(BF16)
    for c0 in range(0, width, col_chunk):
        z = jnp.dot(h, w_ref[:, c0:c0 + col_chunk], preferred_element_type=F32)
        u_ref[:, c0:c0 + col_chunk] = jax.nn.gelu(z).astype(BF16)


def _gmlp_in(x, gain, w_in, v_gain, casts, tm=256):
    n, d = x.shape
    width = w_in.shape[1] // 2
    kern = functools.partial(_gmlp_in_kernel, col_chunk=2048)
    (u, v), cast_out = _pallas_with_casts(
        kern,
        grid=(n // tm,),
        in_specs=[
            pl.BlockSpec((tm, d), lambda i: (i, 0)),
            _resident((1, d)),
            _resident((d, 2 * width)),
            _resident((1, width)),
        ],
        out_specs=[
            pl.BlockSpec((tm, width), lambda i: (i, 0)),
            pl.BlockSpec((tm, width), lambda i: (i, 0)),
        ],
        out_shape=[
            jax.ShapeDtypeStruct((n, width), BF16),
            jax.ShapeDtypeStruct((n, width), BF16),
        ],
        operands=(x, gain, w_in, v_gain),
        casts=casts,
        scratch_shapes=[pltpu.VMEM((tm, width), F32)],
        name="gmlp_in",
    )
    return u, v, cast_out


def _top2_route(logits):
    lane = lax.broadcasted_iota(jnp.int32, logits.shape, 1)
    lg = jnp.where(lane < N_EXPERTS, logits, -jnp.inf)
    m1 = jnp.max(lg, axis=-1, keepdims=True)
    i1 = jnp.min(jnp.where(lg == m1, lane, LANES), axis=-1, keepdims=True)
    lg2 = jnp.where(lane == i1, -jnp.inf, lg)
    m2 = jnp.max(lg2, axis=-1, keepdims=True)
    i2 = jnp.min(jnp.where(lg2 == m2, lane, LANES), axis=-1, keepdims=True)
    e2 = jnp.exp(m2 - m1)
    w1 = 1.0 / (1.0 + e2)
    w2 = e2 / (1.0 + e2)
    idx = jnp.where(lane == 0, i1, jnp.where(lane == 1, i2, 0))
    gate = jnp.where(lane == 0, w1, jnp.where(lane == 1, w2, 0.0))
    return idx, gate


def _gmlp_out_router_kernel(u_ref, v_ref, ws_ref, bs_ref, wo_ref, r_ref, g_ref, wr_ref,
                            x_ref, h_ref, idx_ref, gate_ref, y_ref):
    tm, width = u_ref.shape
    for c0 in range(0, tm, CHUNK):
        for g in range(width // LANES):
            cols = slice(g * LANES, (g + 1) * LANES)
            vv = v_ref[c0:c0 + CHUNK, cols]
            mixed = jnp.dot(ws_ref[g], vv, preferred_element_type=F32) + bs_ref[:, cols]
            y_ref[c0:c0 + CHUNK, cols] = (u_ref[c0:c0 + CHUNK, cols].astype(F32) * mixed).astype(BF16)
    x = r_ref[...] + jnp.dot(y_ref[...], wo_ref[...], preferred_element_type=F32)
    x_ref[...] = x
    h = _rms(x, g_ref[...])
    h_ref[...] = h
    h_hi = h.astype(BF16)
    h_lo = (h - h_hi.astype(F32)).astype(BF16)
    a = jnp.dot(h_hi, wr_ref[...], preferred_element_type=F32)
    b = jnp.dot(h_lo, wr_ref[:, 0:LANES], preferred_element_type=F32)
    logits = a[:, 0:LANES] + (a[:, LANES:2 * LANES] + b)
    idx_ref[...], gate_ref[...] = _top2_route(logits)


def _gmlp_out_router(u, v, w_s, b_full, w_out, res, ffn_gain, w_router_split, tm=512):
    n, width = u.shape
    d = w_out.shape[1]
    groups = w_s.shape[0]
    row = lambda i: (i, 0)
    return pl.pallas_call(
        _gmlp_out_router_kernel,
        grid=(n // tm,),
        in_specs=[
            pl.BlockSpec((tm, width), row),
            pl.BlockSpec((tm, width), row),
            _resident((groups, CHUNK, CHUNK)),
            _resident((CHUNK, width)),
            _resident((width, d)),
            pl.BlockSpec((tm, d), row),
            _resident((1, d)),
            _resident((d, 2 * LANES)),
        ],
        out_specs=[
            pl.BlockSpec((tm, d), row),
            pl.BlockSpec((tm, d), row),
            pl.BlockSpec((tm, LANES), row),
            pl.BlockSpec((tm, LANES), row),
        ],
        out_shape=[
            jax.ShapeDtypeStruct((n, d), F32),
            jax.ShapeDtypeStruct((n, d), F32),
            jax.ShapeDtypeStruct((n, LANES), jnp.int32),
            jax.ShapeDtypeStruct((n, LANES), F32),
        ],
        scratch_shapes=[pltpu.VMEM((tm, width), BF16)],
        compiler_params=_params("parallel"),
        name="gmlp_out_router",
    )(u, v, w_s, b_full, w_out, res, ffn_gain, w_router_split)


def _moe_ffn_kernel(te_ref, nv_ref, tok_ref, h_hbm, wg_ref, wu_ref, wd_ref, o_ref,
                    xbuf, hb_ref, sem, *, rows_per_step, small_rows):
    i = pl.program_id(0)
    j = pl.program_id(1)
    n_tiles = pl.num_programs(0)
    nf = pl.num_programs(1)
    tm = hb_ref.shape[0]
    rows_buf = xbuf.shape[0]

    def row_copy(tile, r):
        tok = tok_ref[tile * tm + r]
        return pltpu.make_async_copy(h_hbm.at[pl.ds(tok, 1), :], xbuf.at[pl.ds(r, 1), :], sem)

    def wait_rows():
        pltpu.make_async_copy(h_hbm.at[pl.ds(0, rows_buf), :], xbuf, sem).wait()

    def prefetch_next_tile():
        for k in range(rows_per_step):
            row_copy(i + 1, j * rows_per_step + k).start(priority=1)

    @pl.when((i == 0) & (j == 0))
    def _():
        def issue(r, carry):
            row_copy(0, r).start()
            return carry
        lax.fori_loop(0, rows_buf, issue, 0)

    @pl.when(j == 0)
    def _():
        wait_rows()
        hb_ref[...] = xbuf[0:tm, :].astype(BF16)
        o_ref[...] = jnp.zeros_like(o_ref)

    def swiglu(rows):
        h = hb_ref[0:rows, :]
        gt = jnp.dot(h, wg_ref[...], preferred_element_type=F32)
        up = jnp.dot(h, wu_ref[...], preferred_element_type=F32)
        act = (gt * jax.nn.sigmoid(gt) * up).astype(BF16)
        o_ref[0:rows, :] += jnp.dot(act, wd_ref[...], preferred_element_type=F32)

    n_valid = nv_ref[i]

    @pl.when(n_valid > small_rows)
    def _():
        prefetch_next_tile()
        swiglu(tm)

    @pl.when((n_valid > 0) & (n_valid <= small_rows))
    def _():
        prefetch_next_tile()
        swiglu(small_rows)

    @pl.when(n_valid == 0)
    def _():
        prefetch_next_tile()

    @pl.when((i == n_tiles - 1) & (j == nf - 1))
    def _():
        wait_rows()


def _moe_ffn(h, tile_expert, tile_valid, row_token, we_gate, we_up, we_down, tm, small_rows=128):
    n, d = h.shape
    nf, tf = we_gate.shape[1], we_gate.shape[3]
    n_tiles = tile_expert.shape[0]
    p = n_tiles * tm
    rows_per_step = -(-tm // (nf * SUBLANES)) * SUBLANES
    rows_buf = rows_per_step * nf
    row_token = jnp.pad(row_token, (0, tm + rows_buf))

    def f_tile(i, j, nv):
        return jnp.where(nv[i] > 0, j, nf - 1)

    def w_in_map(i, j, te, nv, tok):
        return (te[i], f_tile(i, j, nv), 0, 0)

    def w_out_map(i, j, te, nv, tok):
        return (te[i], f_tile(i, j, nv), 0)

    grid_spec = pltpu.PrefetchScalarGridSpec(
        num_scalar_prefetch=3,
        grid=(n_tiles, nf),
        in_specs=[
            pl.BlockSpec(memory_space=pl.ANY),
            pl.BlockSpec((None, None, d, tf), w_in_map),
            pl.BlockSpec((None, None, d, tf), w_in_map),
            pl.BlockSpec((None, tf, d), w_out_map),
        ],
        out_specs=pl.BlockSpec((tm, d), lambda i, j, te, nv, tok: (i, 0)),
        scratch_shapes=[
            pltpu.VMEM((rows_buf, d), F32),
            pltpu.VMEM((tm, d), BF16),
            pltpu.SemaphoreType.DMA(()),
        ],
    )
    return pl.pallas_call(
        functools.partial(_moe_ffn_kernel, rows_per_step=rows_per_step, small_rows=small_rows),
        grid_spec=grid_spec,
        out_shape=jax.ShapeDtypeStruct((p, d), F32),
        compiler_params=_params("arbitrary", "arbitrary"),
        name="moe_ffn",
    )(tile_expert, tile_valid, row_token, h, we_gate, we_up, we_down)


def _moe_combine_kernel(pos_ref, x_ref, gate_ref, y_hbm, o_ref, buf, sems):
    i = pl.program_id(0)
    tm = x_ref.shape[0]
    slot = i % 2

    def row_copies(tile, r, dst_slot):
        for k in range(TOP_K):
            p = pos_ref[(tile * tm + r) * TOP_K + k]
            pltpu.make_async_copy(y_hbm.at[pl.ds(p, 1), :], buf.at[dst_slot, k, pl.ds(r, 1), :],
                                  sems.at[dst_slot]).start(priority=k)

    def wait_slot(s):
        for k in range(TOP_K):
            pltpu.make_async_copy(y_hbm.at[pl.ds(0, tm), :], buf.at[s, k], sems.at[s]).wait()

    @pl.when(i == 0)
    def _():
        def issue(r, carry):
            row_copies(0, r, 0)
            return carry
        lax.fori_loop(0, tm, issue, 0, unroll=8)

    wait_slot(slot)
    chunk = 4 * SUBLANES
    for r0 in range(0, tm, chunk):
        for r in range(r0, r0 + chunk):
            row_copies(i + 1, r, 1 - slot)
        rows = slice(r0, r0 + chunk)
        gate = gate_ref[rows, :]
        o_ref[rows, :] = x_ref[rows, :] + (gate[:, 0:1] * buf[slot, 0, rows, :]
                                           + gate[:, 1:2] * buf[slot, 1, rows, :])

    @pl.when(i == pl.num_programs(0) - 1)
    def _():
        wait_slot(1 - slot)


def _moe_combine(x, gate, y_sorted, pos, tm=256):
    n, d = x.shape
    pos = jnp.pad(pos, (0, TOP_K * tm))
    grid_spec = pltpu.PrefetchScalarGridSpec(
        num_scalar_prefetch=1,
        grid=(n // tm,),
        in_specs=[
            pl.BlockSpec((tm, d), lambda i, pos: (i, 0)),
            pl.BlockSpec((tm, LANES), lambda i, pos: (i, 0)),
            pl.BlockSpec(memory_space=pl.ANY),
        ],
        out_specs=pl.BlockSpec((tm, d), lambda i, pos: (i, 0)),
        scratch_shapes=[
            pltpu.VMEM((2, TOP_K, tm, d), F32),
            pltpu.SemaphoreType.DMA((2,)),
        ],
    )
    return pl.pallas_call(
        _moe_combine_kernel,
        grid_spec=grid_spec,
        out_shape=jax.ShapeDtypeStruct((n, d), F32),
        compiler_params=_params("arbitrary"),
        name="moe_combine",
    )(pos, x, gate, y_sorted)


def _route_plan(top_idx, tm):
    n = top_idx.shape[0]
    n_pairs = n * TOP_K
    n_tiles = (n_pairs + N_EXPERTS * (tm - 1)) // tm
    hot = top_idx[:, :, None] == jnp.arange(N_EXPERTS, dtype=jnp.int32)[None, None, :]
    per_token = jnp.sum(hot, axis=1, dtype=jnp.int32)
    csum = jnp.cumsum(per_token, axis=0)
    counts = csum[-1]
    tiles_per = (counts + tm - 1) // tm
    tile_end = jnp.cumsum(tiles_per)
    row_start = (tile_end - tiles_per) * tm
    slot = (csum - per_token) + row_start[None, :]
    pos = jnp.sum(jnp.where(hot, slot[:, None, :], 0), axis=2).reshape(n_pairs).astype(jnp.int32)
    tile_ids = jnp.arange(n_tiles, dtype=jnp.int32)
    tile_expert = jnp.sum((tile_end[None, :] <= tile_ids[:, None]).astype(jnp.int32), axis=1)
    tile_expert = jnp.minimum(tile_expert, N_EXPERTS - 1).astype(jnp.int32)
    in_expert = tile_ids - (tile_end - tiles_per)[tile_expert]
    tile_valid = jnp.clip(counts[tile_expert] - in_expert * tm, 0, tm)
    tile_valid = jnp.where(tile_ids < tile_end[-1], tile_valid, 0).astype(jnp.int32)
    row_token = jnp.zeros((n_tiles * tm,), jnp.int32).at[pos].set(
        jnp.arange(n_pairs, dtype=jnp.int32) // TOP_K)
    return tile_expert, tile_valid, row_token, pos


def kernel(x, l0_mix_norm, l0_w_qkv, l0_q_norm, l0_k_norm, l0_sink, l0_w_o, l0_ffn_norm, l0_w_gate_up, l0_w_down, l1_mix_norm, l1_w_in, l1_v_norm, l1_w_s, l1_b_s, l1_w_out, l1_ffn_norm, l1_w_router, l1_we_gate, l1_we_up, l1_we_down):
    batch, seq, d = x.shape
    n = batch * seq
    x0 = x.reshape(n, d)
    dq = N_HEADS * HEAD_DIM
    dkv = N_KV_HEADS * HEAD_DIM

    head_gain = jnp.concatenate([
        jnp.tile(l0_q_norm * (HEAD_DIM ** -0.5), N_HEADS),
        jnp.tile(l0_k_norm, N_KV_HEADS),
        jnp.ones((dkv,), F32),
    ])[None, :]
    qkv, (w_gate_up, w_o, w_down) = _qkv_proj(
        x0, l0_mix_norm[None, :], l0_w_qkv.astype(BF16), head_gain, dq + dkv,
        casts=[(l0_w_gate_up, DENSE_TF, None, MXU_COLS), (l0_w_o, None, None), (l0_w_down, None, None)])
    attn, (we_gate,) = _attention(qkv, l0_sink, batch, seq, casts=[(l1_we_gate, MOE_TF, None)])
    x1, _ = _proj_residual(attn, w_o, x0, casts=[], name="attn_out_proj")

    x2, (we_down, w_in, w_out) = _dense_ffn(
        x1, l0_ffn_norm[None, :], w_gate_up, w_down,
        casts=[(l1_we_down, None, 256), (l1_w_in, None, 16), (l1_w_out, None, 16)])

    u, v, (we_up,) = _gmlp_in(x2, l1_mix_norm[None, :], w_in, l1_v_norm[None, :],
                              casts=[(l1_we_up, MOE_TF, None)])
    b_full = jnp.repeat(l1_b_s.T, LANES, axis=1)
    w_router = jnp.pad(l1_w_router, ((0, 0), (0, LANES - N_EXPERTS)))
    w_router_hi = w_router.astype(BF16)
    w_router_lo = (w_router - w_router_hi.astype(F32)).astype(BF16)
    w_router_split = jnp.concatenate([w_router_hi, w_router_lo], axis=1)
    x3, h, ridx, rgate = _gmlp_out_router(u, v, l1_w_s.astype(BF16), b_full, w_out, x2,
                                          l1_ffn_norm[None, :], w_router_split)

    tm = MOE_TM
    tile_expert, tile_valid, row_token, pos = _route_plan(ridx[:, :TOP_K], tm)
    y_sorted = _moe_ffn(h, tile_expert, tile_valid, row_token, we_gate, we_up, we_down, tm)
    out = _moe_combine(x3, rgate, y_sorted, pos)
    return out.reshape(batch, seq, d)
```

```python
import functools

import jax
import jax.numpy as jnp
import numpy as np
from jax import lax
from jax.experimental import pallas as pl
from jax.experimental.pallas import tpu as pltpu

F32 = jnp.float32
BF16 = jnp.bfloat16

EPS = 1e-6
NEG_INF = -1e30
LANES = 128
SUBLANES = 8
HEAD_DIM = 128
N_HEADS = 16
N_KV_HEADS = 4
WINDOW = 128
BLOCK = 128
CHUNK = 128
N_EXPERTS = 8
TOP_K = 2
MOE_TF = 1024
MOE_TM = 512
MXU_COLS = 256
DENSE_TF = 512

MIB = 1024 * 1024
VMEM_LIMIT = 56 * MIB


def _params(*sem):
    return pltpu.CompilerParams(dimension_semantics=sem, vmem_limit_bytes=VMEM_LIMIT)


def _resident(shape):
    zeros = (0,) * len(shape)
    return pl.BlockSpec(shape, lambda *_: zeros, pipeline_mode=pl.Buffered(1))


class _CastJob:
    def __init__(self, w, grid, col_tile=None, block_rows=None, pair_width=None):
        self.pair_width = pair_width
        self.stacked = w.ndim == 3
        self.e, self.k, self.f = w.shape if self.stacked else (1,) + w.shape
        self.src = w.reshape(self.e * self.k, self.f)
        steps = 1
        for g in grid:
            steps *= g
        if block_rows is None:
            assert (self.e * self.k) % steps == 0
            block_rows = self.e * self.k // steps
        assert (self.e * self.k) % block_rows == 0 and self.k % block_rows == 0
        self.block_rows = block_rows
        self.n_blocks = self.e * self.k // block_rows
        assert self.n_blocks <= steps, (self.n_blocks, steps)
        strides = [1] * len(grid)
        for a in range(len(grid) - 2, -1, -1):
            strides[a] = strides[a + 1] * grid[a + 1]
        self.strides = tuple(strides)
        self.col_tile = col_tile
        assert col_tile is None or self.f % col_tile == 0

    def _step(self, ids):
        return sum(i * s for i, s in zip(ids, self.strides))

    def _block(self, ids):
        return jnp.minimum(self._step(ids[:len(self.strides)]), self.n_blocks - 1)

    def src_spec(self):
        return pl.BlockSpec((self.block_rows, self.f), lambda *ids: (self._block(ids), 0))

    def _tiles(self):
        if self.pair_width is not None:
            return self.f // (2 * self.col_tile), 2 * self.col_tile
        return self.f // self.col_tile, self.col_tile

    def dst_spec(self):
        if self.col_tile is None:
            return self.src_spec()
        per_expert = self.k // self.block_rows
        n_tiles, width = self._tiles()
        return pl.BlockSpec(
            (None, n_tiles, self.block_rows, width),
            lambda *ids: (self._block(ids) // per_expert, 0, self._block(ids) % per_expert, 0))

    def out_shape(self):
        if self.col_tile is None:
            return jax.ShapeDtypeStruct(self.src.shape, BF16)
        n_tiles, width = self._tiles()
        return jax.ShapeDtypeStruct((self.e, n_tiles, self.k, width), BF16)

    def _cast_block(self, src_ref, dst_ref):
        ct, pw = self.col_tile, self.pair_width
        if ct is None:
            dst_ref[...] = src_ref[...].astype(BF16)
        elif pw is not None:
            half = self.f // 2
            for c in range(half // ct):
                for s in range(ct // pw):
                    a0 = c * ct + s * pw
                    dst_ref[c, :, 2 * s * pw:(2 * s + 1) * pw] = src_ref[:, a0:a0 + pw].astype(BF16)
                    dst_ref[c, :, (2 * s + 1) * pw:(2 * s + 2) * pw] = (
                        src_ref[:, half + a0:half + a0 + pw].astype(BF16))
        else:
            for c in range(self.f // ct):
                dst_ref[c] = src_ref[:, c * ct:(c + 1) * ct].astype(BF16)

    def run(self, src_ref, dst_ref):
        self._cast_block(src_ref, dst_ref)

    def result(self, dst):
        out = dst if self.col_tile is not None else dst.reshape(self.e, self.k, self.f)
        return out if self.stacked else out[0]


def _pallas_with_casts(kernel_fn, *, grid, in_specs, out_specs, out_shape, operands, casts=(),
                       scratch_shapes=(), name):
    jobs = [_CastJob(w, grid, *options) for (w, *options) in casts]
    n_in, n_out, n_jobs = len(in_specs), len(out_specs), len(jobs)

    def body(*refs):
        ins, srcs = refs[:n_in], refs[n_in:n_in + n_jobs]
        outs = refs[n_in + n_jobs:n_in + n_jobs + n_out]
        dsts = refs[n_in + n_jobs + n_out:n_in + 2 * n_jobs + n_out]
        for job, src, dst in zip(jobs, srcs, dsts):
            job.run(src, dst)
        kernel_fn(*ins, *outs, *refs[n_in + 2 * n_jobs + n_out:])

    results = pl.pallas_call(
        body,
        grid=grid,
        in_specs=list(in_specs) + [job.src_spec() for job in jobs],
        out_specs=list(out_specs) + [job.dst_spec() for job in jobs],
        out_shape=list(out_shape) + [job.out_shape() for job in jobs],
        scratch_shapes=list(scratch_shapes),
        compiler_params=_params(*(["arbitrary"] * len(grid))),
        name=name,
    )(*operands, *[job.src for job in jobs])
    return results[:n_out], [job.result(r) for job, r in zip(jobs, results[n_out:])]


def _rms(x, gain):
    ms = jnp.mean(x * x, axis=-1, keepdims=True)
    return x * lax.rsqrt(ms + EPS) * gain


def _qkv_kernel(x_ref, g_ref, w_ref, hg_ref, o_ref, *, n_norm_cols, col_chunk):
    h = _rms(x_ref[...], g_ref[...]).astype(BF16)
    n_out = o_ref.shape[1]
    for c0 in range(0, n_out, col_chunk):
        y = jnp.dot(h, w_ref[:, c0:c0 + col_chunk], preferred_element_type=F32)
        for h0 in range(0, col_chunk, HEAD_DIM):
            col = c0 + h0
            yh = y[:, h0:h0 + HEAD_DIM]
            if col < n_norm_cols:
                yh = _rms(yh, hg_ref[:, col:col + HEAD_DIM])
            o_ref[:, col:col + HEAD_DIM] = yh.astype(BF16)


def _qkv_proj(x, gain, w, head_gain, n_norm_cols, casts, tm=512):
    n, d = x.shape
    nq = w.shape[1]
    kern = functools.partial(_qkv_kernel, n_norm_cols=n_norm_cols, col_chunk=512)
    (qkv,), cast_out = _pallas_with_casts(
        kern,
        grid=(n // tm,),
        in_specs=[
            pl.BlockSpec((tm, d), lambda i: (i, 0)),
            _resident((1, d)),
            _resident((d, nq)),
            _resident((1, nq)),
        ],
        out_specs=[pl.BlockSpec((tm, nq), lambda i: (i, 0))],
        out_shape=[jax.ShapeDtypeStruct((n, nq), BF16)],
        operands=(x, gain, w, head_gain),
        casts=casts,
        name="qkv_proj",
    )
    return qkv, cast_out


def _attn_kernel(sink_ref, q_ref, k0_ref, k1_ref, k2_ref, k3_ref, v0_ref, v1_ref, v2_ref, v3_ref,
                 bias_ref, o_ref):
    j = pl.program_id(1)
    kcat = jnp.concatenate([k0_ref[...], k1_ref[...], k2_ref[...], k3_ref[...]], axis=0)
    vcat = jnp.concatenate([v0_ref[...], v1_ref[...], v2_ref[...], v3_ref[...]], axis=0)
    group = N_HEADS // N_KV_HEADS
    band = 3 * BLOCK
    variants = (jnp.where(j == 0, 0, 1), jnp.where(j == pl.num_programs(1) - 1, 2, 1))
    for a in range(2):
        rows = slice(a * BLOCK, (a + 1) * BLOCK)
        for kh in range(N_KV_HEADS):
            cols = slice(kh * HEAD_DIM, (kh + 1) * HEAD_DIM)
            k_h = kcat[a * BLOCK:a * BLOCK + band, cols]
            v_h = vcat[a * BLOCK:a * BLOCK + band, cols]
            heads = [kh * group + g for g in range(group)]
            q_g = jnp.concatenate([q_ref[rows, hd * HEAD_DIM:(hd + 1) * HEAD_DIM] for hd in heads], axis=0)
            sink = jnp.concatenate([jnp.full((1, BLOCK), sink_ref[hd], F32) for hd in heads], axis=1)
            s = lax.dot_general(k_h, q_g, (((1,), (1,)), ((), ())), preferred_element_type=F32)
            s = s + bias_ref[variants[a], kh * band:(kh + 1) * band, :]
            m = jnp.maximum(jnp.max(s, axis=0, keepdims=True), sink)
            p = jnp.exp(s - m)
            denom = jnp.sum(p, axis=0, keepdims=True) + jnp.exp(sink - m)
            o_t = lax.dot_general(v_h, p.astype(BF16), (((0,), (0,)), ((), ())),
                                  preferred_element_type=F32) * (1.0 / denom)
            for g, hd in enumerate(heads):
                o_ref[rows, hd * HEAD_DIM:(hd + 1) * HEAD_DIM] = (
                    o_t[:, g * BLOCK:(g + 1) * BLOCK].T.astype(BF16))


def _attention_bias():
    group = N_HEADS // N_KV_HEADS
    qi = np.arange(BLOCK)[:, None]
    sj = np.arange(3 * BLOCK)[None, :]
    dist = np.abs(qi - sj + BLOCK)
    slopes = np.exp2(-8.0 * np.arange(1, N_HEADS + 1, dtype=np.float64) / N_HEADS)
    bias = np.where(dist <= WINDOW, -slopes[:, None, None] * dist, NEG_INF)
    first = (sj >= BLOCK)[None]
    last = (sj < 2 * BLOCK)[None]
    edge = np.stack([np.where(first, bias, NEG_INF), bias, np.where(last, bias, NEG_INF)])
    edge = edge.reshape(3, N_KV_HEADS, group, BLOCK, 3 * BLOCK).transpose(0, 1, 4, 2, 3)
    return np.ascontiguousarray(edge.reshape(3, N_KV_HEADS * 3 * BLOCK, group * BLOCK), dtype=np.float32)


def _attention(qkv, sink, batch, seq, casts):
    n = qkv.shape[0]
    nb = seq // BLOCK
    assert nb % 2 == 0
    steps = nb // 2
    dq = N_HEADS * HEAD_DIM
    dkv = N_KV_HEADS * HEAD_DIM
    k_col = dq // dkv
    v_col = k_col + 1
    group = N_HEADS // N_KV_HEADS

    def kv_spec(offset, col):
        return pl.BlockSpec(
            (BLOCK, dkv), lambda b, j: (b * nb + jnp.clip(2 * j + offset, 0, nb - 1), col))

    (attn,), cast_out = _pallas_with_casts(
        _attn_kernel,
        grid=(batch, steps),
        in_specs=[
            pl.BlockSpec(memory_space=pltpu.SMEM),
            pl.BlockSpec((2 * BLOCK, dq), lambda b, j: (b * steps + j, 0)),
            kv_spec(-1, k_col), kv_spec(0, k_col), kv_spec(1, k_col), kv_spec(2, k_col),
            kv_spec(-1, v_col), kv_spec(0, v_col), kv_spec(1, v_col), kv_spec(2, v_col),
            _resident((3, N_KV_HEADS * 3 * BLOCK, group * BLOCK)),
        ],
        out_specs=[pl.BlockSpec((2 * BLOCK, dq), lambda b, j: (b * steps + j, 0))],
        out_shape=[jax.ShapeDtypeStruct((n, dq), BF16)],
        operands=(sink, qkv, qkv, qkv, qkv, qkv, qkv, qkv, qkv, qkv, _attention_bias()),
        casts=casts,
        name="window_attention",
    )
    return attn, cast_out


def _proj_res_kernel(a_ref, w_ref, r_ref, o_ref):
    o_ref[...] = r_ref[...] + jnp.dot(a_ref[...], w_ref[...], preferred_element_type=F32)


def _proj_residual(a, w, res, casts, tm=1024, name="proj_residual"):
    n, k = a.shape
    d = w.shape[1]
    (out,), cast_out = _pallas_with_casts(
        _proj_res_kernel,
        grid=(n // tm,),
        in_specs=[
            pl.BlockSpec((tm, k), lambda i: (i, 0)),
            _resident((k, d)),
            pl.BlockSpec((tm, d), lambda i: (i, 0)),
        ],
        out_specs=[pl.BlockSpec((tm, d), lambda i: (i, 0))],
        out_shape=[jax.ShapeDtypeStruct((n, d), F32)],
        operands=(a, w, res),
        casts=casts,
        name=name,
    )
    return out, cast_out


def _swiglu_paired(h, wgu_ref):
    gu = jnp.dot(h, wgu_ref[...], preferred_element_type=F32)
    acts = []
    for c0 in range(0, gu.shape[1], 2 * MXU_COLS):
        gt = gu[:, c0:c0 + MXU_COLS]
        up = gu[:, c0 + MXU_COLS:c0 + 2 * MXU_COLS]
        acts.append((gt * jax.nn.sigmoid(gt) * up).astype(BF16))
    return jnp.concatenate(acts, axis=1)


def _dense_ffn_kernel(x_ref, g_ref, wgu_ref, wd_ref, o_ref, h_ref):
    @pl.when(pl.program_id(1) == 0)
    def _():
        x = x_ref[...]
        h_ref[...] = _rms(x, g_ref[...]).astype(BF16)
        o_ref[...] = x

    act = _swiglu_paired(h_ref[...], wgu_ref)
    o_ref[...] += jnp.dot(act, wd_ref[...], preferred_element_type=F32)


def _dense_ffn(x, gain, w_gate_up, w_down, casts, tm=512):
    n, d = x.shape
    f = w_down.shape[0]
    nf = w_gate_up.shape[0]
    tf = f // nf
    (out,), cast_out = _pallas_with_casts(
        _dense_ffn_kernel,
        grid=(n // tm, nf),
        in_specs=[
            pl.BlockSpec((tm, d), lambda i, j: (i, 0)),
            _resident((1, d)),
            pl.BlockSpec((None, d, 2 * tf), lambda i, j: (j, 0, 0)),
            pl.BlockSpec((tf, d), lambda i, j: (j, 0)),
        ],
        out_specs=[pl.BlockSpec((tm, d), lambda i, j: (i, 0))],
        out_shape=[jax.ShapeDtypeStruct((n, d), F32)],
        operands=(x, gain, w_gate_up, w_down),
        casts=casts,
        scratch_shapes=[pltpu.VMEM((tm, d), BF16)],
        name="dense_ffn",
    )
    return out, cast_out


def _gmlp_in_kernel(x_ref, g_ref, w_ref, vg_ref, u_ref, v_ref, vf_ref, *, col_chunk):
    h = _rms(x_ref[...], g_ref[...]).astype(BF16)
    width = u_ref.shape[1]
    ssq = jnp.zeros((x_ref.shape[0], 1), F32)
    for c0 in range(0, width, col_chunk):
        z = jnp.dot(h, w_ref[:, width + c0:width + c0 + col_chunk], preferred_element_type=F32)
        v = jax.nn.gelu(z)
        ssq = ssq + jnp.sum(v * v, axis=-1, keepdims=True)
        vf_ref[:, c0:c0 + col_chunk] = v
    inv = lax.rsqrt(ssq / width + EPS)
    v_ref[...] = (vf_ref[...] * inv * vg_ref[...]).astype(BF16)
    for c0 in range(0, width, col_chunk):
        z = jnp.dot(h, w_ref[:, c0:c0 + col_chunk], preferred_element_type=F32)
        u_ref[:, c0:c0 + col_chunk] = jax.nn.gelu(z).astype(BF16)


def _gmlp_in(x, gain, w_in, v_gain, casts, tm=256):
    n, d = x.shape
    width = w_in.shape[1] // 2
    kern = functools.partial(_gmlp_in_kernel, col_chunk=2048)
    (u, v), cast_out = _pallas_with_casts(
        kern,
        grid=(n // tm,),
        in_specs=[
            pl.BlockSpec((tm, d), lambda i: (i, 0)),
            _resident((1, d)),
            _resident((d, 2 * width)),
            _resident((1, width)),
        ],
        out_specs=[
            pl.BlockSpec((tm, width), lambda i: (i, 0)),
            pl.BlockSpec((tm, width), lambda i: (i, 0)),
        ],
        out_shape=[
            jax.ShapeDtypeStruct((n, width), BF16),
            jax.ShapeDtypeStruct((n, width), BF16),
        ],
        operands=(x, gain, w_in, v_gain),
        casts=casts,
        scratch_shapes=[pltpu.VMEM((tm, width), F32)],
        name="gmlp_in",
    )
    return u, v, cast_out


def _top2_route(logits):
    lane = lax.broadcasted_iota(jnp.int32, logits.shape, 1)
    lg = jnp.where(lane < N_EXPERTS, logits, -jnp.inf)
    m1 = jnp.max(lg, axis=-1, keepdims=True)
    i1 = jnp.min(jnp.where(lg == m1, lane, LANES), axis=-1, keepdims=True)
    lg2 = jnp.where(lane == i1, -jnp.inf, lg)
    m2 = jnp.max(lg2, axis=-1, keepdims=True)
    i2 = jnp.min(jnp.where(lg2 == m2, lane, LANES), axis=-1, keepdims=True)
    e2 = jnp.exp(m2 - m1)
    w1 = 1.0 / (1.0 + e2)
    w2 = e2 / (1.0 + e2)
    idx = jnp.where(lane == 0, i1, jnp.where(lane == 1, i2, 0))
    gate = jnp.where(lane == 0, w1, jnp.where(lane == 1, w2, 0.0))
    return idx, gate


def _gmlp_out_router_kernel(u_ref, v_ref, ws_ref, bs_ref, wo_ref, r_ref, g_ref, wr_ref,
                            x_ref, h_ref, idx_ref, gate_ref, y_ref):
    tm, width = u_ref.shape
    for c0 in range(0, tm, CHUNK):
        for g in range(width // LANES):
            cols = slice(g * LANES, (g + 1) * LANES)
            vv = v_ref[c0:c0 + CHUNK, cols]
            mixed = jnp.dot(ws_ref[g], vv, preferred_element_type=F32) + bs_ref[:, cols]
            y_ref[c0:c0 + CHUNK, cols] = (u_ref[c0:c0 + CHUNK, cols].astype(F32) * mixed).astype(BF16)
    x = r_ref[...] + jnp.dot(y_ref[...], wo_ref[...], preferred_element_type=F32)
    x_ref[...] = x
    h = _rms(x, g_ref[...])
    h_ref[...] = h
    h_hi = h.astype(BF16)
    h_lo = (h - h_hi.astype(F32)).astype(BF16)
    a = jnp.dot(h_hi, wr_ref[...], preferred_element_type=F32)
    b = jnp.dot(h_lo, wr_ref[:, 0:LANES], preferred_element_type=F32)
    logits = a[:, 0:LANES] + (a[:, LANES:2 * LANES] + b)
    idx_ref[...], gate_ref[...] = _top2_route(logits)


def _gmlp_out_router(u, v, w_s, b_full, w_out, res, ffn_gain, w_router_split, tm=512):
    n, width = u.shape
    d = w_out.shape[1]
    groups = w_s.shape[0]
    row = lambda i: (i, 0)
    return pl.pallas_call(
        _gmlp_out_router_kernel,
        grid=(n // tm,),
        in_specs=[
            pl.BlockSpec((tm, width), row),
            pl.BlockSpec((tm, width), row),
            _resident((groups, CHUNK, CHUNK)),
            _resident((CHUNK, width)),
            _resident((width, d)),
            pl.BlockSpec((tm, d), row),
            _resident((1, d)),
            _resident((d, 2 * LANES)),
        ],
        out_specs=[
            pl.BlockSpec((tm, d), row),
            pl.BlockSpec((tm, d), row),
            pl.BlockSpec((tm, LANES), row),
            pl.BlockSpec((tm, LANES), row),
        ],
        out_shape=[
            jax.ShapeDtypeStruct((n, d), F32),
            jax.ShapeDtypeStruct((n, d), F32),
            jax.ShapeDtypeStruct((n, LANES), jnp.int32),
            jax.ShapeDtypeStruct((n, LANES), F32),
        ],
        scratch_shapes=[pltpu.VMEM((tm, width), BF16)],
        compiler_params=_params("parallel"),
        name="gmlp_out_router",
    )(u, v, w_s, b_full, w_out, res, ffn_gain, w_router_split)


def _moe_ffn_kernel(te_ref, nv_ref, tok_ref, h_hbm, wg_ref, wu_ref, wd_ref, o_ref,
                    xbuf, hb_ref, sem, *, rows_per_step, small_rows):
    i = pl.program_id(0)
    j = pl.program_id(1)
    n_tiles = pl.num_programs(0)
    nf = pl.num_programs(1)
    tm = hb_ref.shape[0]
    rows_buf = xbuf.shape[0]

    def row_copy(tile, r):
        tok = tok_ref[tile * tm + r]
        return pltpu.make_async_copy(h_hbm.at[pl.ds(tok, 1), :], xbuf.at[pl.ds(r, 1), :], sem)

    def wait_rows():
        pltpu.make_async_copy(h_hbm.at[pl.ds(0, rows_buf), :], xbuf, sem).wait()

    def prefetch_next_tile():
        for k in range(rows_per_step):
            row_copy(i + 1, j * rows_per_step + k).start(priority=1)

    @pl.when((i == 0) & (j == 0))
    def _():
        def issue(r, carry):
            row_copy(0, r).start()
            return carry
        lax.fori_loop(0, rows_buf, issue, 0)

    @pl.when(j == 0)
    def _():
        wait_rows()
        hb_ref[...] = xbuf[0:tm, :].astype(BF16)
        o_ref[...] = jnp.zeros_like(o_ref)

    def swiglu(rows):
        h = hb_ref[0:rows, :]
        gt = jnp.dot(h, wg_ref[...], preferred_element_type=F32)
        up = jnp.dot(h, wu_ref[...], preferred_element_type=F32)
        act = (gt * jax.nn.sigmoid(gt) * up).astype(BF16)
        o_ref[0:rows, :] += jnp.dot(act, wd_ref[...], preferred_element_type=F32)

    n_valid = nv_ref[i]

    @pl.when(n_valid > small_rows)
    def _():
        prefetch_next_tile()
        swiglu(tm)

    @pl.when((n_valid > 0) & (n_valid <= small_rows))
    def _():
        prefetch_next_tile()
        swiglu(small_rows)

    @pl.when(n_valid == 0)
    def _():
        prefetch_next_tile()

    @pl.when((i == n_tiles - 1) & (j == nf - 1))
    def _():
        wait_rows()


def _moe_ffn(h, tile_expert, tile_valid, row_token, we_gate, we_up, we_down, tm, small_rows=128):
    n, d = h.shape
    nf, tf = we_gate.shape[1], we_gate.shape[3]
    n_tiles = tile_expert.shape[0]
    p = n_tiles * tm
    rows_per_step = -(-tm // (nf * SUBLANES)) * SUBLANES
    rows_buf = rows_per_step * nf
    row_token = jnp.pad(row_token, (0, tm + rows_buf))

    def f_tile(i, j, nv):
        return jnp.where(nv[i] > 0, j, nf - 1)

    def w_in_map(i, j, te, nv, tok):
        return (te[i], f_tile(i, j, nv), 0, 0)

    def w_out_map(i, j, te, nv, tok):
        return (te[i], f_tile(i, j, nv), 0)

    grid_spec = pltpu.PrefetchScalarGridSpec(
        num_scalar_prefetch=3,
        grid=(n_tiles, nf),
        in_specs=[
            pl.BlockSpec(memory_space=pl.ANY),
            pl.BlockSpec((None, None, d, tf), w_in_map),
            pl.BlockSpec((None, None, d, tf), w_in_map),
            pl.BlockSpec((None, tf, d), w_out_map),
        ],
        out_specs=pl.BlockSpec((tm, d), lambda i, j, te, nv, tok: (i, 0)),
        scratch_shapes=[
            pltpu.VMEM((rows_buf, d), F32),
            pltpu.VMEM((tm, d), BF16),
            pltpu.SemaphoreType.DMA(()),
        ],
    )
    return pl.pallas_call(
        functools.partial(_moe_ffn_kernel, rows_per_step=rows_per_step, small_rows=small_rows),
        grid_spec=grid_spec,
        out_shape=jax.ShapeDtypeStruct((p, d), F32),
        compiler_params=_params("arbitrary", "arbitrary"),
        name="moe_ffn",
    )(tile_expert, tile_valid, row_token, h, we_gate, we_up, we_down)


def _moe_combine_kernel(pos_ref, x_ref, gate_ref, y_hbm, o_ref, buf, sems):
    i = pl.program_id(0)
    tm = x_ref.shape[0]
    slot = i % 2

    def row_copies(tile, r, dst_slot):
        for k in range(TOP_K):
            p = pos_ref[(tile * tm + r) * TOP_K + k]
            pltpu.make_async_copy(y_hbm.at[pl.ds(p, 1), :], buf.at[dst_slot, k, pl.ds(r, 1), :],
                                  sems.at[dst_slot]).start(priority=k)

    def wait_slot(s):
        for k in range(TOP_K):
            pltpu.make_async_copy(y_hbm.at[pl.ds(0, tm), :], buf.at[s, k], sems.at[s]).wait()

    @pl.when(i == 0)
    def _():
        def issue(r, carry):
            row_copies(0, r, 0)
            return carry
        lax.fori_loop(0, tm, issue, 0, unroll=8)

    wait_slot(slot)
    chunk = 4 * SUBLANES
    for r0 in range(0, tm, chunk):
        for r in range(r0, r0 + chunk):
            row_copies(i + 1, r, 1 - slot)
        rows = slice(r0, r0 + chunk)
        gate = gate_ref[rows, :]
        o_ref[rows, :] = x_ref[rows, :] + (gate[:, 0:1] * buf[slot, 0, rows, :]
                                           + gate[:, 1:2] * buf[slot, 1, rows, :])

    @pl.when(i == pl.num_programs(0) - 1)
    def _():
        wait_slot(1 - slot)


def _moe_combine(x, gate, y_sorted, pos, tm=512):
    n, d = x.shape
    pos = jnp.pad(pos, (0, TOP_K * tm))
    grid_spec = pltpu.PrefetchScalarGridSpec(
        num_scalar_prefetch=1,
        grid=(n // tm,),
        in_specs=[
            pl.BlockSpec((tm, d), lambda i, pos: (i, 0)),
            pl.BlockSpec((tm, LANES), lambda i, pos: (i, 0)),
            pl.BlockSpec(memory_space=pl.ANY),
        ],
        out_specs=pl.BlockSpec((tm, d), lambda i, pos: (i, 0)),
        scratch_shapes=[
            pltpu.VMEM((2, TOP_K, tm, d), F32),
            pltpu.SemaphoreType.DMA((2,)),
        ],
    )
    return pl.pallas_call(
        _moe_combine_kernel,
        grid_spec=grid_spec,
        out_shape=jax.ShapeDtypeStruct((n, d), F32),
        compiler_params=_params("arbitrary"),
        name="moe_combine",
    )(pos, x, gate, y_sorted)


def _route_plan(top_idx, tm):
    n = top_idx.shape[0]
    n_pairs = n * TOP_K
    n_tiles = (n_pairs + N_EXPERTS * (tm - 1)) // tm
    hot = top_idx[:, :, None] == jnp.arange(N_EXPERTS, dtype=jnp.int32)[None, None, :]
    per_token = jnp.sum(hot, axis=1, dtype=jnp.int32)
    csum = jnp.cumsum(per_token, axis=0)
    counts = csum[-1]
    tiles_per = (counts + tm - 1) // tm
    tile_end = jnp.cumsum(tiles_per)
    row_start = (tile_end - tiles_per) * tm
    slot = (csum - per_token) + row_start[None, :]
    pos = jnp.sum(jnp.where(hot, slot[:, None, :], 0), axis=2).reshape(n_pairs).astype(jnp.int32)
    tile_ids = jnp.arange(n_tiles, dtype=jnp.int32)
    tile_expert = jnp.sum((tile_end[None, :] <= tile_ids[:, None]).astype(jnp.int32), axis=1)
    tile_expert = jnp.minimum(tile_expert, N_EXPERTS - 1).astype(jnp.int32)
    in_expert = tile_ids - (tile_end - tiles_per)[tile_expert]
    tile_valid = jnp.clip(counts[tile_expert] - in_expert * tm, 0, tm)
    tile_valid = jnp.where(tile_ids < tile_end[-1], tile_valid, 0).astype(jnp.int32)
    row_token = jnp.zeros((n_tiles * tm,), jnp.int32).at[pos].set(
        jnp.arange(n_pairs, dtype=jnp.int32) // TOP_K)
    return tile_expert, tile_valid, row_token, pos


def kernel(x, l0_mix_norm, l0_w_qkv, l0_q_norm, l0_k_norm, l0_sink, l0_w_o, l0_ffn_norm, l0_w_gate_up, l0_w_down, l1_mix_norm, l1_w_in, l1_v_norm, l1_w_s, l1_b_s, l1_w_out, l1_ffn_norm, l1_w_router, l1_we_gate, l1_we_up, l1_we_down):
    batch, seq, d = x.shape
    n = batch * seq
    x0 = x.reshape(n, d)
    dq = N_HEADS * HEAD_DIM
    dkv = N_KV_HEADS * HEAD_DIM

    head_gain = jnp.concatenate([
        jnp.tile(l0_q_norm * (HEAD_DIM ** -0.5), N_HEADS),
        jnp.tile(l0_k_norm, N_KV_HEADS),
        jnp.ones((dkv,), F32),
    ])[None, :]
    qkv, (w_gate_up, w_o, w_down) = _qkv_proj(
        x0, l0_mix_norm[None, :], l0_w_qkv.astype(BF16), head_gain, dq + dkv,
        casts=[(l0_w_gate_up, DENSE_TF, None, MXU_COLS), (l0_w_o, None, None), (l0_w_down, None, None)])
    attn, (we_gate,) = _attention(qkv, l0_sink, batch, seq, casts=[(l1_we_gate, MOE_TF, None)])
    x1, _ = _proj_residual(attn, w_o, x0, casts=[], name="attn_out_proj")

    x2, (we_down, w_in, w_out) = _dense_ffn(
        x1, l0_ffn_norm[None, :], w_gate_up, w_down,
        casts=[(l1_we_down, None, 256), (l1_w_in, None, 16), (l1_w_out, None, 16)])

    u, v, (we_up,) = _gmlp_in(x2, l1_mix_norm[None, :], w_in, l1_v_norm[None, :],
                              casts=[(l1_we_up, MOE_TF, None)])
    b_full = jnp.repeat(l1_b_s.T, LANES, axis=1)
    w_router = jnp.pad(l1_w_router, ((0, 0), (0, LANES - N_EXPERTS)))
    w_router_hi = w_router.astype(BF16)
    w_router_lo = (w_router - w_router_hi.astype(F32)).astype(BF16)
    w_router_split = jnp.concatenate([w_router_hi, w_router_lo], axis=1)
    x3, h, ridx, rgate = _gmlp_out_router(u, v, l1_w_s.astype(BF16), b_full, w_out, x2,
                                          l1_ffn_norm[None, :], w_router_split)

    tm = MOE_TM
    tile_expert, tile_valid, row_token, pos = _route_plan(ridx[:, :TOP_K], tm)
    y_sorted = _moe_ffn(h, tile_expert, tile_valid, row_token, we_gate, we_up, we_down, tm)
    out = _moe_combine(x3, rgate, y_sorted, pos)
    return out.reshape(batch, seq, d)
```

```python
import functools

import jax
import jax.numpy as jnp
import numpy as np
from jax import lax
from jax.experimental import pallas as pl
from jax.experimental.pallas import tpu as pltpu

F32 = jnp.float32
BF16 = jnp.bfloat16

EPS = 1e-6
NEG_INF = -1e30
LANES = 128
SUBLANES = 8
HEAD_DIM = 128
N_HEADS = 16
N_KV_HEADS = 4
WINDOW = 128
BLOCK = 128
CHUNK = 128
N_EXPERTS = 8
TOP_K = 2
MOE_TF = 1024
MOE_TM = 512
MXU_COLS = 256
DENSE_TF = 512

MIB = 1024 * 1024
VMEM_LIMIT = 56 * MIB


def _params(*sem):
    return pltpu.CompilerParams(dimension_semantics=sem, vmem_limit_bytes=VMEM_LIMIT)


def _resident(shape):
    zeros = (0,) * len(shape)
    return pl.BlockSpec(shape, lambda *_: zeros, pipeline_mode=pl.Buffered(1))


class _CastJob:
    def __init__(self, w, grid, col_tile=None, block_rows=None, pair_width=None):
        self.pair_width = pair_width
        self.stacked = w.ndim == 3
        self.e, self.k, self.f = w.shape if self.stacked else (1,) + w.shape
        self.src = w.reshape(self.e * self.k, self.f)
        steps = 1
        for g in grid:
            steps *= g
        if block_rows is None:
            assert (self.e * self.k) % steps == 0
            block_rows = self.e * self.k // steps
        assert (self.e * self.k) % block_rows == 0 and self.k % block_rows == 0
        self.block_rows = block_rows
        self.n_blocks = self.e * self.k // block_rows
        assert self.n_blocks <= steps, (self.n_blocks, steps)
        strides = [1] * len(grid)
        for a in range(len(grid) - 2, -1, -1):
            strides[a] = strides[a + 1] * grid[a + 1]
        self.strides = tuple(strides)
        self.col_tile = col_tile
        assert col_tile is None or self.f % col_tile == 0

    def _step(self, ids):
        return sum(i * s for i, s in zip(ids, self.strides))

    def _block(self, ids):
        return jnp.minimum(self._step(ids[:len(self.strides)]), self.n_blocks - 1)

    def src_spec(self):
        return pl.BlockSpec((self.block_rows, self.f), lambda *ids: (self._block(ids), 0))

    def _tiles(self):
        if self.pair_width is not None:
            return self.f // (2 * self.col_tile), 2 * self.col_tile
        return self.f // self.col_tile, self.col_tile

    def dst_spec(self):
        if self.col_tile is None:
            return self.src_spec()
        per_expert = self.k // self.block_rows
        n_tiles, width = self._tiles()
        return pl.BlockSpec(
            (None, n_tiles, self.block_rows, width),
            lambda *ids: (self._block(ids) // per_expert, 0, self._block(ids) % per_expert, 0))

    def out_shape(self):
        if self.col_tile is None:
            return jax.ShapeDtypeStruct(self.src.shape, BF16)
        n_tiles, width = self._tiles()
        return jax.ShapeDtypeStruct((self.e, n_tiles, self.k, width), BF16)

    def _cast_block(self, src_ref, dst_ref):
        ct, pw = self.col_tile, self.pair_width
        if ct is None:
            dst_ref[...] = src_ref[...].astype(BF16)
        elif pw is not None:
            half = self.f // 2
            for c in range(half // ct):
                for s in range(ct // pw):
                    a0 = c * ct + s * pw
                    dst_ref[c, :, 2 * s * pw:(2 * s + 1) * pw] = src_ref[:, a0:a0 + pw].astype(BF16)
                    dst_ref[c, :, (2 * s + 1) * pw:(2 * s + 2) * pw] = (
                        src_ref[:, half + a0:half + a0 + pw].astype(BF16))
        else:
            for c in range(self.f // ct):
                dst_ref[c] = src_ref[:, c * ct:(c + 1) * ct].astype(BF16)

    def run(self, src_ref, dst_ref):
        self._cast_block(src_ref, dst_ref)

    def result(self, dst):
        out = dst if self.col_tile is not None else dst.reshape(self.e, self.k, self.f)
        return out if self.stacked else out[0]


def _pallas_with_casts(kernel_fn, *, grid, in_specs, out_specs, out_shape, operands, casts=(),
                       scratch_shapes=(), name):
    jobs = [_CastJob(w, grid, *options) for (w, *options) in casts]
    n_in, n_out, n_jobs = len(in_specs), len(out_specs), len(jobs)

    def body(*refs):
        ins, srcs = refs[:n_in], refs[n_in:n_in + n_jobs]
        outs = refs[n_in + n_jobs:n_in + n_jobs + n_out]
        dsts = refs[n_in + n_jobs + n_out:n_in + 2 * n_jobs + n_out]
        for job, src, dst in zip(jobs, srcs, dsts):
            job.run(src, dst)
        kernel_fn(*ins, *outs, *refs[n_in + 2 * n_jobs + n_out:])

    results = pl.pallas_call(
        body,
        grid=grid,
        in_specs=list(in_specs) + [job.src_spec() for job in jobs],
        out_specs=list(out_specs) + [job.dst_spec() for job in jobs],
        out_shape=list(out_shape) + [job.out_shape() for job in jobs],
        scratch_shapes=list(scratch_shapes),
        compiler_params=_params(*(["arbitrary"] * len(grid))),
        name=name,
    )(*operands, *[job.src for job in jobs])
    return results[:n_out], [job.result(r) for job, r in zip(jobs, results[n_out:])]


def _rms(x, gain):
    ms = jnp.mean(x * x, axis=-1, keepdims=True)
    return x * lax.rsqrt(ms + EPS) * gain


def _qkv_kernel(x_ref, g_ref, w_ref, hg_ref, o_ref, *, n_norm_cols, col_chunk):
    h = _rms(x_ref[...], g_ref[...]).astype(BF16)
    n_out = o_ref.shape[1]
    for c0 in range(0, n_out, col_chunk):
        y = jnp.dot(h, w_ref[:, c0:c0 + col_chunk], preferred_element_type=F32)
        for h0 in range(0, col_chunk, HEAD_DIM):
            col = c0 + h0
            yh = y[:, h0:h0 + HEAD_DIM]
            if col < n_norm_cols:
                yh = _rms(yh, hg_ref[:, col:col + HEAD_DIM])
            o_ref[:, col:col + HEAD_DIM] = yh.astype(BF16)


def _qkv_proj(x, gain, w, head_gain, n_norm_cols, casts, tm=512):
    n, d = x.shape
    nq = w.shape[1]
    kern = functools.partial(_qkv_kernel, n_norm_cols=n_norm_cols, col_chunk=512)
    (qkv,), cast_out = _pallas_with_casts(
        kern,
        grid=(n // tm,),
        in_specs=[
            pl.BlockSpec((tm, d), lambda i: (i, 0)),
            _resident((1, d)),
            _resident((d, nq)),
            _resident((1, nq)),
        ],
        out_specs=[pl.BlockSpec((tm, nq), lambda i: (i, 0))],
        out_shape=[jax.ShapeDtypeStruct((n, nq), BF16)],
        operands=(x, gain, w, head_gain),
        casts=casts,
        name="qkv_proj",
    )
    return qkv, cast_out


def _attn_kernel(sink_ref, q_ref, k0_ref, k1_ref, k2_ref, k3_ref, v0_ref, v1_ref, v2_ref, v3_ref,
                 bias_ref, o_ref):
    j = pl.program_id(1)
    kcat = jnp.concatenate([k0_ref[...], k1_ref[...], k2_ref[...], k3_ref[...]], axis=0)
    vcat = jnp.concatenate([v0_ref[...], v1_ref[...], v2_ref[...], v3_ref[...]], axis=0)
    group = N_HEADS // N_KV_HEADS
    band = 3 * BLOCK
    variants = (jnp.where(j == 0, 0, 1), jnp.where(j == pl.num_programs(1) - 1, 2, 1))
    for a in range(2):
        rows = slice(a * BLOCK, (a + 1) * BLOCK)
        for kh in range(N_KV_HEADS):
            cols = slice(kh * HEAD_DIM, (kh + 1) * HEAD_DIM)
            k_h = kcat[a * BLOCK:a * BLOCK + band, cols]
            v_h = vcat[a * BLOCK:a * BLOCK + band, cols]
            heads = [kh * group + g for g in range(group)]
            q_g = jnp.concatenate([q_ref[rows, hd * HEAD_DIM:(hd + 1) * HEAD_DIM] for hd in heads], axis=0)
            sink = jnp.concatenate([jnp.full((1, BLOCK), sink_ref[hd], F32) for hd in heads], axis=1)
            s = lax.dot_general(k_h, q_g, (((1,), (1,)), ((), ())), preferred_element_type=F32)
            s = s + bias_ref[variants[a], kh * band:(kh + 1) * band, :]
            m = jnp.maximum(jnp.max(s, axis=0, keepdims=True), sink)
            p = jnp.exp(s - m)
            denom = jnp.sum(p, axis=0, keepdims=True) + jnp.exp(sink - m)
            o_t = lax.dot_general(v_h, p.astype(BF16), (((0,), (0,)), ((), ())),
                                  preferred_element_type=F32) * (1.0 / denom)
            for g, hd in enumerate(heads):
                o_ref[rows, hd * HEAD_DIM:(hd + 1) * HEAD_DIM] = (
                    o_t[:, g * BLOCK:(g + 1) * BLOCK].T.astype(BF16))


def _attention_bias():
    group = N_HEADS // N_KV_HEADS
    qi = np.arange(BLOCK)[:, None]
    sj = np.arange(3 * BLOCK)[None, :]
    dist = np.abs(qi - sj + BLOCK)
    slopes = np.exp2(-8.0 * np.arange(1, N_HEADS + 1, dtype=np.float64) / N_HEADS)
    bias = np.where(dist <= WINDOW, -slopes[:, None, None] * dist, NEG_INF)
    first = (sj >= BLOCK)[None]
    last = (sj < 2 * BLOCK)[None]
    edge = np.stack([np.where(first, bias, NEG_INF), bias, np.where(last, bias, NEG_INF)])
    edge = edge.reshape(3, N_KV_HEADS, group, BLOCK, 3 * BLOCK).transpose(0, 1, 4, 2, 3)
    return np.ascontiguousarray(edge.reshape(3, N_KV_HEADS * 3 * BLOCK, group * BLOCK), dtype=np.float32)


def _attention(qkv, sink, batch, seq, casts):
    n = qkv.shape[0]
    nb = seq // BLOCK
    assert nb % 2 == 0
    steps = nb // 2
    dq = N_HEADS * HEAD_DIM
    dkv = N_KV_HEADS * HEAD_DIM
    k_col = dq // dkv
    v_col = k_col + 1
    group = N_HEADS // N_KV_HEADS

    def kv_spec(offset, col):
        return pl.BlockSpec(
            (BLOCK, dkv), lambda b, j: (b * nb + jnp.clip(2 * j + offset, 0, nb - 1), col))

    (attn,), cast_out = _pallas_with_casts(
        _attn_kernel,
        grid=(batch, steps),
        in_specs=[
            pl.BlockSpec(memory_space=pltpu.SMEM),
            pl.BlockSpec((2 * BLOCK, dq), lambda b, j: (b * steps + j, 0)),
            kv_spec(-1, k_col), kv_spec(0, k_col), kv_spec(1, k_col), kv_spec(2, k_col),
            kv_spec(-1, v_col), kv_spec(0, v_col), kv_spec(1, v_col), kv_spec(2, v_col),
            _resident((3, N_KV_HEADS * 3 * BLOCK, group * BLOCK)),
        ],
        out_specs=[pl.BlockSpec((2 * BLOCK, dq), lambda b, j: (b * steps + j, 0))],
        out_shape=[jax.ShapeDtypeStruct((n, dq), BF16)],
        operands=(sink, qkv, qkv, qkv, qkv, qkv, qkv, qkv, qkv, qkv, _attention_bias()),
        casts=casts,
        name="window_attention",
    )
    return attn, cast_out


def _proj_res_kernel(a_ref, w_ref, r_ref, o_ref):
    o_ref[...] = r_ref[...] + jnp.dot(a_ref[...], w_ref[...], preferred_element_type=F32)


def _proj_residual(a, w, res, casts, tm=512, name="proj_residual"):
    n, k = a.shape
    d = w.shape[1]
    (out,), cast_out = _pallas_with_casts(
        _proj_res_kernel,
        grid=(n // tm,),
        in_specs=[
            pl.BlockSpec((tm, k), lambda i: (i, 0)),
            _resident((k, d)),
            pl.BlockSpec((tm, d), lambda i: (i, 0)),
        ],
        out_specs=[pl.BlockSpec((tm, d), lambda i: (i, 0))],
        out_shape=[jax.ShapeDtypeStruct((n, d), F32)],
        operands=(a, w, res),
        casts=casts,
        name=name,
    )
    return out, cast_out


def _swiglu_paired(h, wgu_ref):
    gu = jnp.dot(h, wgu_ref[...], preferred_element_type=F32)
    acts = []
    for c0 in range(0, gu.shape[1], 2 * MXU_COLS):
        gt = gu[:, c0:c0 + MXU_COLS]
        up = gu[:, c0 + MXU_COLS:c0 + 2 * MXU_COLS]
        acts.append((gt * jax.nn.sigmoid(gt) * up).astype(BF16))
    return jnp.concatenate(acts, axis=1)


def _dense_ffn_kernel(x_ref, g_ref, wgu_ref, wd_ref, o_ref, h_ref):
    @pl.when(pl.program_id(1) == 0)
    def _():
        x = x_ref[...]
        h_ref[...] = _rms(x, g_ref[...]).astype(BF16)
        o_ref[...] = x

    act = _swiglu_paired(h_ref[...], wgu_ref)
    o_ref[...] += jnp.dot(act, wd_ref[...], preferred_element_type=F32)


def _dense_ffn(x, gain, w_gate_up, w_down, casts, tm=512):
    n, d = x.shape
    f = w_down.shape[0]
    nf = w_gate_up.shape[0]
    tf = f // nf
    (out,), cast_out = _pallas_with_casts(
        _dense_ffn_kernel,
        grid=(n // tm, nf),
        in_specs=[
            pl.BlockSpec((tm, d), lambda i, j: (i, 0)),
            _resident((1, d)),
            pl.BlockSpec((None, d, 2 * tf), lambda i, j: (j, 0, 0)),
            pl.BlockSpec((tf, d), lambda i, j: (j, 0)),
        ],
        out_specs=[pl.BlockSpec((tm, d), lambda i, j: (i, 0))],
        out_shape=[jax.ShapeDtypeStruct((n, d), F32)],
        operands=(x, gain, w_gate_up, w_down),
        casts=casts,
        scratch_shapes=[pltpu.VMEM((tm, d), BF16)],
        name="dense_ffn",
    )
    return out, cast_out


def _gmlp_in_kernel(x_ref, g_ref, w_ref, vg_ref, u_ref, v_ref, vf_ref, *, col_chunk):
    h = _rms(x_ref[...], g_ref[...]).astype(BF16)
    width = u_ref.shape[1]
    ssq = jnp.zeros((x_ref.shape[0], 1), F32)
    for c0 in range(0, width, col_chunk):
        z = jnp.dot(h, w_ref[:, width + c0:width + c0 + col_chunk], preferred_element_type=F32)
        v = jax.nn.gelu(z)
        ssq = ssq + jnp.sum(v * v, axis=-1, keepdims=True)
        vf_ref[:, c0:c0 + col_chunk] = v
    inv = lax.rsqrt(ssq / width + EPS)
    v_ref[...] = (vf_ref[...] * inv * vg_ref[...]).astype(BF16)
    for c0 in range(0, width, col_chunk):
        z = jnp.dot(h, w_ref[:, c0:c0 + col_chunk], preferred_element_type=F32)
        u_ref[:, c0:c0 + col_chunk] = jax.nn.gelu(z).astype(BF16)


def _gmlp_in(x, gain, w_in, v_gain, casts, tm=256):
    n, d = x.shape
    width = w_in.shape[1] // 2
    kern = functools.partial(_gmlp_in_kernel, col_chunk=2048)
    (u, v), cast_out = _pallas_with_casts(
        kern,
        grid=(n // tm,),
        in_specs=[
            pl.BlockSpec((tm, d), lambda i: (i, 0)),
            _resident((1, d)),
            _resident((d, 2 * width)),
            _resident((1, width)),
        ],
        out_specs=[
            pl.BlockSpec((tm, width), lambda i: (i, 0)),
            pl.BlockSpec((tm, width), lambda i: (i, 0)),
        ],
        out_shape=[
            jax.ShapeDtypeStruct((n, width), BF16),
            jax.ShapeDtypeStruct((n, width), BF16),
        ],
        operands=(x, gain, w_in, v_gain),
        casts=casts,
        scratch_shapes=[pltpu.VMEM((tm, width), F32)],
        name="gmlp_in",
    )
    return u, v, cast_out


def _top2_route(logits):
    lane = lax.broadcasted_iota(jnp.int32, logits.shape, 1)
    lg = jnp.where(lane < N_EXPERTS, logits, -jnp.inf)
    m1 = jnp.max(lg, axis=-1, keepdims=True)
    i1 = jnp.min(jnp.where(lg == m1, lane, LANES), axis=-1, keepdims=True)
    lg2 = jnp.where(lane == i1, -jnp.inf, lg)
    m2 = jnp.max(lg2, axis=-1, keepdims=True)
    i2 = jnp.min(jnp.where(lg2 == m2, lane, LANES), axis=-1, keepdims=True)
    e2 = jnp.exp(m2 - m1)
    w1 = 1.0 / (1.0 + e2)
    w2 = e2 / (1.0 + e2)
    idx = jnp.where(lane == 0, i1, jnp.where(lane == 1, i2, 0))
    gate = jnp.where(lane == 0, w1, jnp.where(lane == 1, w2, 0.0))
    return idx, gate


def _gmlp_out_router_kernel(u_ref, v_ref, ws_ref, bs_ref, wo_ref, r_ref, g_ref, wr_ref,
                            x_ref, h_ref, idx_ref, gate_ref, y_ref):
    tm, width = u_ref.shape
    for c0 in range(0, tm, CHUNK):
        for g in range(width // LANES):
            cols = slice(g * LANES, (g + 1) * LANES)
            vv = v_ref[c0:c0 + CHUNK, cols]
            mixed = jnp.dot(ws_ref[g], vv, preferred_element_type=F32) + bs_ref[:, cols]
            y_ref[c0:c0 + CHUNK, cols] = (u_ref[c0:c0 + CHUNK, cols].astype(F32) * mixed).astype(BF16)
    x = r_ref[...] + jnp.dot(y_ref[...], wo_ref[...], preferred_element_type=F32)
    x_ref[...] = x
    h = _rms(x, g_ref[...])
    h_ref[...] = h
    h_hi = h.astype(BF16)
    h_lo = (h - h_hi.astype(F32)).astype(BF16)
    a = jnp.dot(h_hi, wr_ref[...], preferred_element_type=F32)
    b = jnp.dot(h_lo, wr_ref[:, 0:LANES], preferred_element_type=F32)
    logits = a[:, 0:LANES] + (a[:, LANES:2 * LANES] + b)
    idx_ref[...], gate_ref[...] = _top2_route(logits)


def _gmlp_out_router(u, v, w_s, b_full, w_out, res, ffn_gain, w_router_split, tm=512):
    n, width = u.shape
    d = w_out.shape[1]
    groups = w_s.shape[0]
    row = lambda i: (i, 0)
    return pl.pallas_call(
        _gmlp_out_router_kernel,
        grid=(n // tm,),
        in_specs=[
            pl.BlockSpec((tm, width), row),
            pl.BlockSpec((tm, width), row),
            _resident((groups, CHUNK, CHUNK)),
            _resident((CHUNK, width)),
            _resident((width, d)),
            pl.BlockSpec((tm, d), row),
            _resident((1, d)),
            _resident((d, 2 * LANES)),
        ],
        out_specs=[
            pl.BlockSpec((tm, d), row),
            pl.BlockSpec((tm, d), row),
            pl.BlockSpec((tm, LANES), row),
            pl.BlockSpec((tm, LANES), row),
        ],
        out_shape=[
            jax.ShapeDtypeStruct((n, d), F32),
            jax.ShapeDtypeStruct((n, d), F32),
            jax.ShapeDtypeStruct((n, LANES), jnp.int32),
            jax.ShapeDtypeStruct((n, LANES), F32),
        ],
        scratch_shapes=[pltpu.VMEM((tm, width), BF16)],
        compiler_params=_params("parallel"),
        name="gmlp_out_router",
    )(u, v, w_s, b_full, w_out, res, ffn_gain, w_router_split)


def _moe_ffn_kernel(te_ref, nv_ref, tok_ref, h_hbm, wg_ref, wu_ref, wd_ref, o_ref,
                    xbuf, hb_ref, sem, *, rows_per_step, small_rows):
    i = pl.program_id(0)
    j = pl.program_id(1)
    n_tiles = pl.num_programs(0)
    nf = pl.num_programs(1)
    tm = hb_ref.shape[0]
    rows_buf = xbuf.shape[0]

    def row_copy(tile, r):
        tok = tok_ref[tile * tm + r]
        return pltpu.make_async_copy(h_hbm.at[pl.ds(tok, 1), :], xbuf.at[pl.ds(r, 1), :], sem)

    def wait_rows():
        pltpu.make_async_copy(h_hbm.at[pl.ds(0, rows_buf), :], xbuf, sem).wait()

    def prefetch_next_tile():
        for k in range(rows_per_step):
            row_copy(i + 1, j * rows_per_step + k).start(priority=1)

    @pl.when((i == 0) & (j == 0))
    def _():
        def issue(r, carry):
            row_copy(0, r).start()
            return carry
        lax.fori_loop(0, rows_buf, issue, 0)

    @pl.when(j == 0)
    def _():
        wait_rows()
        hb_ref[...] = xbuf[0:tm, :].astype(BF16)
        o_ref[...] = jnp.zeros_like(o_ref)

    def swiglu(rows):
        h = hb_ref[0:rows, :]
        gt = jnp.dot(h, wg_ref[...], preferred_element_type=F32)
        up = jnp.dot(h, wu_ref[...], preferred_element_type=F32)
        act = (gt * jax.nn.sigmoid(gt) * up).astype(BF16)
        o_ref[0:rows, :] += jnp.dot(act, wd_ref[...], preferred_element_type=F32)

    n_valid = nv_ref[i]

    @pl.when(n_valid > small_rows)
    def _():
        prefetch_next_tile()
        swiglu(tm)

    @pl.when((n_valid > 0) & (n_valid <= small_rows))
    def _():
        prefetch_next_tile()
        swiglu(small_rows)

    @pl.when(n_valid == 0)
    def _():
        prefetch_next_tile()

    @pl.when((i == n_tiles - 1) & (j == nf - 1))
    def _():
        wait_rows()


def _moe_ffn(h, tile_expert, tile_valid, row_token, we_gate, we_up, we_down, tm, small_rows=128):
    n, d = h.shape
    nf, tf = we_gate.shape[1], we_gate.shape[3]
    n_tiles = tile_expert.shape[0]
    p = n_tiles * tm
    rows_per_step = -(-tm // (nf * SUBLANES)) * SUBLANES
    rows_buf = rows_per_step * nf
    row_token = jnp.pad(row_token, (0, tm + rows_buf))

    def f_tile(i, j, nv):
        return jnp.where(nv[i] > 0, j, nf - 1)

    def w_in_map(i, j, te, nv, tok):
        return (te[i], f_tile(i, j, nv), 0, 0)

    def w_out_map(i, j, te, nv, tok):
        return (te[i], f_tile(i, j, nv), 0)

    grid_spec = pltpu.PrefetchScalarGridSpec(
        num_scalar_prefetch=3,
        grid=(n_tiles, nf),
        in_specs=[
            pl.BlockSpec(memory_space=pl.ANY),
            pl.BlockSpec((None, None, d, tf), w_in_map),
            pl.BlockSpec((None, None, d, tf), w_in_map),
            pl.BlockSpec((None, tf, d), w_out_map),
        ],
        out_specs=pl.BlockSpec((tm, d), lambda i, j, te, nv, tok: (i, 0)),
        scratch_shapes=[
            pltpu.VMEM((rows_buf, d), F32),
            pltpu.VMEM((tm, d), BF16),
            pltpu.SemaphoreType.DMA(()),
        ],
    )
    return pl.pallas_call(
        functools.partial(_moe_ffn_kernel, rows_per_step=rows_per_step, small_rows=small_rows),
        grid_spec=grid_spec,
        out_shape=jax.ShapeDtypeStruct((p, d), F32),
        compiler_params=_params("arbitrary", "arbitrary"),
        name="moe_ffn",
    )(tile_expert, tile_valid, row_token, h, we_gate, we_up, we_down)


def _moe_combine_kernel(pos_ref, x_ref, gate_ref, y_hbm, o_ref, buf, sems):
    i = pl.program_id(0)
    tm = x_ref.shape[0]
    slot = i % 2

    def row_copies(tile, r, dst_slot):
        for k in range(TOP_K):
            p = pos_ref[(tile * tm + r) * TOP_K + k]
            pltpu.make_async_copy(y_hbm.at[pl.ds(p, 1), :], buf.at[dst_slot, k, pl.ds(r, 1), :],
                                  sems.at[dst_slot]).start(priority=k)

    def wait_slot(s):
        for k in range(TOP_K):
            pltpu.make_async_copy(y_hbm.at[pl.ds(0, tm), :], buf.at[s, k], sems.at[s]).wait()

    @pl.when(i == 0)
    def _():
        def issue(r, carry):
            row_copies(0, r, 0)
            return carry
        lax.fori_loop(0, tm, issue, 0, unroll=8)

    wait_slot(slot)
    chunk = 4 * SUBLANES
    for r0 in range(0, tm, chunk):
        for r in range(r0, r0 + chunk):
            row_copies(i + 1, r, 1 - slot)
        rows = slice(r0, r0 + chunk)
        gate = gate_ref[rows, :]
        o_ref[rows, :] = x_ref[rows, :] + (gate[:, 0:1] * buf[slot, 0, rows, :]
                                           + gate[:, 1:2] * buf[slot, 1, rows, :])

    @pl.when(i == pl.num_programs(0) - 1)
    def _():
        wait_slot(1 - slot)


def _moe_combine(x, gate, y_sorted, pos, tm=256):
    n, d = x.shape
    pos = jnp.pad(pos, (0, TOP_K * tm))
    grid_spec = pltpu.PrefetchScalarGridSpec(
        num_scalar_prefetch=1,
        grid=(n // tm,),
        in_specs=[
            pl.BlockSpec((tm, d), lambda i, pos: (i, 0)),
            pl.BlockSpec((tm, LANES), lambda i, pos: (i, 0)),
            pl.BlockSpec(memory_space=pl.ANY),
        ],
        out_specs=pl.BlockSpec((tm, d), lambda i, pos: (i, 0)),
        scratch_shapes=[
            pltpu.VMEM((2, TOP_K, tm, d), F32),
            pltpu.SemaphoreType.DMA((2,)),
        ],
    )
    return pl.pallas_call(
        _moe_combine_kernel,
        grid_spec=grid_spec,
        out_shape=jax.ShapeDtypeStruct((n, d), F32),
        compiler_params=_params("arbitrary"),
        name="moe_combine",
    )(pos, x, gate, y_sorted)


def _route_plan(top_idx, tm):
    n = top_idx.shape[0]
    n_pairs = n * TOP_K
    n_tiles = (n_pairs + N_EXPERTS * (tm - 1)) // tm
    hot = top_idx[:, :, None] == jnp.arange(N_EXPERTS, dtype=jnp.int32)[None, None, :]
    per_token = jnp.sum(hot, axis=1, dtype=jnp.int32)
    csum = jnp.cumsum(per_token, axis=0)
    counts = csum[-1]
    tiles_per = (counts + tm - 1) // tm
    tile_end = jnp.cumsum(tiles_per)
    row_start = (tile_end - tiles_per) * tm
    slot = (csum - per_token) + row_start[None, :]
    pos = jnp.sum(jnp.where(hot, slot[:, None, :], 0), axis=2).reshape(n_pairs).astype(jnp.int32)
    tile_ids = jnp.arange(n_tiles, dtype=jnp.int32)
    tile_expert = jnp.sum((tile_end[None, :] <= tile_ids[:, None]).astype(jnp.int32), axis=1)
    tile_expert = jnp.minimum(tile_expert, N_EXPERTS - 1).astype(jnp.int32)
    in_expert = tile_ids - (tile_end - tiles_per)[tile_expert]
    tile_valid = jnp.clip(counts[tile_expert] - in_expert * tm, 0, tm)
    tile_valid = jnp.where(tile_ids < tile_end[-1], tile_valid, 0).astype(jnp.int32)
    row_token = jnp.zeros((n_tiles * tm,), jnp.int32).at[pos].set(
        jnp.arange(n_pairs, dtype=jnp.int32) // TOP_K)
    return tile_expert, tile_valid, row_token, pos


def kernel(x, l0_mix_norm, l0_w_qkv, l0_q_norm, l0_k_norm, l0_sink, l0_w_o, l0_ffn_norm, l0_w_gate_up, l0_w_down, l1_mix_norm, l1_w_in, l1_v_norm, l1_w_s, l1_b_s, l1_w_out, l1_ffn_norm, l1_w_router, l1_we_gate, l1_we_up, l1_we_down):
    batch, seq, d = x.shape
    n = batch * seq
    x0 = x.reshape(n, d)
    dq = N_HEADS * HEAD_DIM
    dkv = N_KV_HEADS * HEAD_DIM

    head_gain = jnp.concatenate([
        jnp.tile(l0_q_norm * (HEAD_DIM ** -0.5), N_HEADS),
        jnp.tile(l0_k_norm, N_KV_HEADS),
        jnp.ones((dkv,), F32),
    ])[None, :]
    qkv, (w_gate_up, w_o, w_down) = _qkv_proj(
        x0, l0_mix_norm[None, :], l0_w_qkv.astype(BF16), head_gain, dq + dkv,
        casts=[(l0_w_gate_up, DENSE_TF, None, MXU_COLS), (l0_w_o, None, None), (l0_w_down, None, None)])
    attn, (we_gate,) = _attention(qkv, l0_sink, batch, seq, casts=[(l1_we_gate, MOE_TF, None)])
    x1, (w_in, w_out) = _proj_residual(attn, w_o, x0, casts=[(l1_w_in, None, None), (l1_w_out, None, None)],
                                       name="attn_out_proj")

    x2, (we_down,) = _dense_ffn(x1, l0_ffn_norm[None, :], w_gate_up, w_down,
                                casts=[(l1_we_down, None, 256)])

    u, v, (we_up,) = _gmlp_in(x2, l1_mix_norm[None, :], w_in, l1_v_norm[None, :],
                              casts=[(l1_we_up, MOE_TF, None)])
    b_full = jnp.repeat(l1_b_s.T, LANES, axis=1)
    w_router = jnp.pad(l1_w_router, ((0, 0), (0, LANES - N_EXPERTS)))
    w_router_hi = w_router.astype(BF16)
    w_router_lo = (w_router - w_router_hi.astype(F32)).astype(BF16)
    w_router_split = jnp.concatenate([w_router_hi, w_router_lo], axis=1)
    x3, h, ridx, rgate = _gmlp_out_router(u, v, l1_w_s.astype(BF16), b_full, w_out, x2,
                                          l1_ffn_norm[None, :], w_router_split)

    tm = MOE_TM
    tile_expert, tile_valid, row_token, pos = _route_plan(ridx[:, :TOP_K], tm)
    y_sorted = _moe_ffn(h, tile_expert, tile_valid, row_token, we_gate, we_up, we_down, tm)
    out = _moe_combine(x3, rgate, y_sorted, pos)
    return out.reshape(batch, seq, d)
```

```python
import functools

import jax
import jax.numpy as jnp
import numpy as np
from jax import lax
from jax.experimental import pallas as pl
from jax.experimental.pallas import tpu as pltpu

F32 = jnp.float32
BF16 = jnp.bfloat16

EPS = 1e-6
NEG_INF = -1e30
LANES = 128
SUBLANES = 8
HEAD_DIM = 128
N_HEADS = 16
N_KV_HEADS = 4
WINDOW = 128
BLOCK = 128
CHUNK = 128
N_EXPERTS = 8
TOP_K = 2
MOE_TF = 1024
MOE_TM = 512
MOE_TM_EXT = 48
MOE_TM_SMALL = 128
MXU_COLS = 256
DENSE_TF = 512

MIB = 1024 * 1024
VMEM_LIMIT = 56 * MIB


def _params(*sem):
    return pltpu.CompilerParams(dimension_semantics=sem, vmem_limit_bytes=VMEM_LIMIT)


def _resident(shape):
    zeros = (0,) * len(shape)
    return pl.BlockSpec(shape, lambda *_: zeros, pipeline_mode=pl.Buffered(1))


class _CastJob:
    def __init__(self, w, grid, col_tile=None, block_rows=None, pair_width=None):
        self.pair_width = pair_width
        self.stacked = w.ndim == 3
        self.e, self.k, self.f = w.shape if self.stacked else (1,) + w.shape
        self.src = w.reshape(self.e * self.k, self.f)
        steps = 1
        for g in grid:
            steps *= g
        if block_rows is None:
            assert (self.e * self.k) % steps == 0
            block_rows = self.e * self.k // steps
        assert (self.e * self.k) % block_rows == 0 and self.k % block_rows == 0
        self.block_rows = block_rows
        self.n_blocks = self.e * self.k // block_rows
        assert self.n_blocks <= steps, (self.n_blocks, steps)
        strides = [1] * len(grid)
        for a in range(len(grid) - 2, -1, -1):
            strides[a] = strides[a + 1] * grid[a + 1]
        self.strides = tuple(strides)
        self.col_tile = col_tile
        assert col_tile is None or self.f % col_tile == 0

    def _step(self, ids):
        return sum(i * s for i, s in zip(ids, self.strides))

    def _block(self, ids):
        return jnp.minimum(self._step(ids[:len(self.strides)]), self.n_blocks - 1)

    def src_spec(self):
        return pl.BlockSpec((self.block_rows, self.f), lambda *ids: (self._block(ids), 0))

    def _tiles(self):
        if self.pair_width is not None:
            return self.f // (2 * self.col_tile), 2 * self.col_tile
        return self.f // self.col_tile, self.col_tile

    def dst_spec(self):
        if self.col_tile is None:
            return self.src_spec()
        per_expert = self.k // self.block_rows
        n_tiles, width = self._tiles()
        return pl.BlockSpec(
            (None, n_tiles, self.block_rows, width),
            lambda *ids: (self._block(ids) // per_expert, 0, self._block(ids) % per_expert, 0))

    def out_shape(self):
        if self.col_tile is None:
            return jax.ShapeDtypeStruct(self.src.shape, BF16)
        n_tiles, width = self._tiles()
        return jax.ShapeDtypeStruct((self.e, n_tiles, self.k, width), BF16)

    def _cast_block(self, src_ref, dst_ref):
        ct, pw = self.col_tile, self.pair_width
        if ct is None:
            dst_ref[...] = src_ref[...].astype(BF16)
        elif pw is not None:
            half = self.f // 2
            for c in range(half // ct):
                for s in range(ct // pw):
                    a0 = c * ct + s * pw
                    dst_ref[c, :, 2 * s * pw:(2 * s + 1) * pw] = src_ref[:, a0:a0 + pw].astype(BF16)
                    dst_ref[c, :, (2 * s + 1) * pw:(2 * s + 2) * pw] = (
                        src_ref[:, half + a0:half + a0 + pw].astype(BF16))
        else:
            for c in range(self.f // ct):
                dst_ref[c] = src_ref[:, c * ct:(c + 1) * ct].astype(BF16)

    def run(self, src_ref, dst_ref):
        self._cast_block(src_ref, dst_ref)

    def result(self, dst):
        out = dst if self.col_tile is not None else dst.reshape(self.e, self.k, self.f)
        return out if self.stacked else out[0]


def _pallas_with_casts(kernel_fn, *, grid, in_specs, out_specs, out_shape, operands, casts=(),
                       scratch_shapes=(), name):
    jobs = [_CastJob(w, grid, *options) for (w, *options) in casts]
    n_in, n_out, n_jobs = len(in_specs), len(out_specs), len(jobs)

    def body(*refs):
        ins, srcs = refs[:n_in], refs[n_in:n_in + n_jobs]
        outs = refs[n_in + n_jobs:n_in + n_jobs + n_out]
        dsts = refs[n_in + n_jobs + n_out:n_in + 2 * n_jobs + n_out]
        for job, src, dst in zip(jobs, srcs, dsts):
            job.run(src, dst)
        kernel_fn(*ins, *outs, *refs[n_in + 2 * n_jobs + n_out:])

    results = pl.pallas_call(
        body,
        grid=grid,
        in_specs=list(in_specs) + [job.src_spec() for job in jobs],
        out_specs=list(out_specs) + [job.dst_spec() for job in jobs],
        out_shape=list(out_shape) + [job.out_shape() for job in jobs],
        scratch_shapes=list(scratch_shapes),
        compiler_params=_params(*(["arbitrary"] * len(grid))),
        name=name,
    )(*operands, *[job.src for job in jobs])
    return results[:n_out], [job.result(r) for job, r in zip(jobs, results[n_out:])]


def _rms(x, gain):
    ms = jnp.mean(x * x, axis=-1, keepdims=True)
    return x * lax.rsqrt(ms + EPS) * gain


def _qkv_kernel(x_ref, g_ref, w_ref, hg_ref, o_ref, *, n_norm_cols, col_chunk):
    h = _rms(x_ref[...], g_ref[...]).astype(BF16)
    n_out = o_ref.shape[1]
    for c0 in range(0, n_out, col_chunk):
        y = jnp.dot(h, w_ref[:, c0:c0 + col_chunk], preferred_element_type=F32)
        for h0 in range(0, col_chunk, HEAD_DIM):
            col = c0 + h0
            yh = y[:, h0:h0 + HEAD_DIM]
            if col < n_norm_cols:
                yh = _rms(yh, hg_ref[:, col:col + HEAD_DIM])
            o_ref[:, col:col + HEAD_DIM] = yh.astype(BF16)


def _qkv_proj(x, gain, w, head_gain, n_norm_cols, casts, tm=512):
    n, d = x.shape
    nq = w.shape[1]
    kern = functools.partial(_qkv_kernel, n_norm_cols=n_norm_cols, col_chunk=512)
    (qkv,), cast_out = _pallas_with_casts(
        kern,
        grid=(n // tm,),
        in_specs=[
            pl.BlockSpec((tm, d), lambda i: (i, 0)),
            _resident((1, d)),
            _resident((d, nq)),
            _resident((1, nq)),
        ],
        out_specs=[pl.BlockSpec((tm, nq), lambda i: (i, 0))],
        out_shape=[jax.ShapeDtypeStruct((n, nq), BF16)],
        operands=(x, gain, w, head_gain),
        casts=casts,
        name="qkv_proj",
    )
    return qkv, cast_out


def _attn_kernel(sink_ref, q_ref, k0_ref, k1_ref, k2_ref, k3_ref, v0_ref, v1_ref, v2_ref, v3_ref,
                 bias_ref, o_ref):
    j = pl.program_id(1)
    kcat = jnp.concatenate([k0_ref[...], k1_ref[...], k2_ref[...], k3_ref[...]], axis=0)
    vcat = jnp.concatenate([v0_ref[...], v1_ref[...], v2_ref[...], v3_ref[...]], axis=0)
    group = N_HEADS // N_KV_HEADS
    band = 3 * BLOCK
    variants = (jnp.where(j == 0, 0, 1), jnp.where(j == pl.num_programs(1) - 1, 2, 1))
    for a in range(2):
        rows = slice(a * BLOCK, (a + 1) * BLOCK)
        for kh in range(N_KV_HEADS):
            cols = slice(kh * HEAD_DIM, (kh + 1) * HEAD_DIM)
            k_h = kcat[a * BLOCK:a * BLOCK + band, cols]
            v_h = vcat[a * BLOCK:a * BLOCK + band, cols]
            heads = [kh * group + g for g in range(group)]
            q_g = jnp.concatenate([q_ref[rows, hd * HEAD_DIM:(hd + 1) * HEAD_DIM] for hd in heads], axis=0)
            sink = jnp.concatenate([jnp.full((1, BLOCK), sink_ref[hd], F32) for hd in heads], axis=1)
            s = lax.dot_general(k_h, q_g, (((1,), (1,)), ((), ())), preferred_element_type=F32)
            s = s + bias_ref[variants[a], kh * band:(kh + 1) * band, :]
            m = jnp.maximum(jnp.max(s, axis=0, keepdims=True), sink)
            p = jnp.exp(s - m)
            denom = jnp.sum(p, axis=0, keepdims=True) + jnp.exp(sink - m)
            o_t = lax.dot_general(v_h, p.astype(BF16), (((0,), (0,)), ((), ())),
                                  preferred_element_type=F32) * (1.0 / denom)
            for g, hd in enumerate(heads):
                o_ref[rows, hd * HEAD_DIM:(hd + 1) * HEAD_DIM] = (
                    o_t[:, g * BLOCK:(g + 1) * BLOCK].T.astype(BF16))


def _attention_bias():
    group = N_HEADS // N_KV_HEADS
    qi = np.arange(BLOCK)[:, None]
    sj = np.arange(3 * BLOCK)[None, :]
    dist = np.abs(qi - sj + BLOCK)
    slopes = np.exp2(-8.0 * np.arange(1, N_HEADS + 1, dtype=np.float64) / N_HEADS)
    bias = np.where(dist <= WINDOW, -slopes[:, None, None] * dist, NEG_INF)
    first = (sj >= BLOCK)[None]
    last = (sj < 2 * BLOCK)[None]
    edge = np.stack([np.where(first, bias, NEG_INF), bias, np.where(last, bias, NEG_INF)])
    edge = edge.reshape(3, N_KV_HEADS, group, BLOCK, 3 * BLOCK).transpose(0, 1, 4, 2, 3)
    return np.ascontiguousarray(edge.reshape(3, N_KV_HEADS * 3 * BLOCK, group * BLOCK), dtype=np.float32)


def _attention(qkv, sink, batch, seq, casts):
    n = qkv.shape[0]
    nb = seq // BLOCK
    assert nb % 2 == 0
    steps = nb // 2
    dq = N_HEADS * HEAD_DIM
    dkv = N_KV_HEADS * HEAD_DIM
    k_col = dq // dkv
    v_col = k_col + 1
    group = N_HEADS // N_KV_HEADS

    def kv_spec(offset, col):
        return pl.BlockSpec(
            (BLOCK, dkv), lambda b, j: (b * nb + jnp.clip(2 * j + offset, 0, nb - 1), col))

    (attn,), cast_out = _pallas_with_casts(
        _attn_kernel,
        grid=(batch, steps),
        in_specs=[
            pl.BlockSpec(memory_space=pltpu.SMEM),
            pl.BlockSpec((2 * BLOCK, dq), lambda b, j: (b * steps + j, 0)),
            kv_spec(-1, k_col), kv_spec(0, k_col), kv_spec(1, k_col), kv_spec(2, k_col),
            kv_spec(-1, v_col), kv_spec(0, v_col), kv_spec(1, v_col), kv_spec(2, v_col),
            _resident((3, N_KV_HEADS * 3 * BLOCK, group * BLOCK)),
        ],
        out_specs=[pl.BlockSpec((2 * BLOCK, dq), lambda b, j: (b * steps + j, 0))],
        out_shape=[jax.ShapeDtypeStruct((n, dq), BF16)],
        operands=(sink, qkv, qkv, qkv, qkv, qkv, qkv, qkv, qkv, qkv, _attention_bias()),
        casts=casts,
        name="window_attention",
    )
    return attn, cast_out


def _proj_res_kernel(a_ref, w_ref, r_ref, o_ref):
    o_ref[...] = r_ref[...] + jnp.dot(a_ref[...], w_ref[...], preferred_element_type=F32)


def _proj_residual(a, w, res, casts, tm=512, name="proj_residual"):
    n, k = a.shape
    d = w.shape[1]
    (out,), cast_out = _pallas_with_casts(
        _proj_res_kernel,
        grid=(n // tm,),
        in_specs=[
            pl.BlockSpec((tm, k), lambda i: (i, 0)),
            _resident((k, d)),
            pl.BlockSpec((tm, d), lambda i: (i, 0)),
        ],
        out_specs=[pl.BlockSpec((tm, d), lambda i: (i, 0))],
        out_shape=[jax.ShapeDtypeStruct((n, d), F32)],
        operands=(a, w, res),
        casts=casts,
        name=name,
    )
    return out, cast_out


def _swiglu_paired(h, wgu_ref):
    gu = jnp.dot(h, wgu_ref[...], preferred_element_type=F32)
    acts = []
    for c0 in range(0, gu.shape[1], 2 * MXU_COLS):
        gt = gu[:, c0:c0 + MXU_COLS]
        up = gu[:, c0 + MXU_COLS:c0 + 2 * MXU_COLS]
        acts.append((gt * jax.nn.sigmoid(gt) * up).astype(BF16))
    return jnp.concatenate(acts, axis=1)


def _dense_ffn_kernel(x_ref, g_ref, wgu_ref, wd_ref, o_ref, h_ref):
    @pl.when(pl.program_id(1) == 0)
    def _():
        x = x_ref[...]
        h_ref[...] = _rms(x, g_ref[...]).astype(BF16)
        o_ref[...] = x

    act = _swiglu_paired(h_ref[...], wgu_ref)
    o_ref[...] += jnp.dot(act, wd_ref[...], preferred_element_type=F32)


def _dense_ffn(x, gain, w_gate_up, w_down, casts, tm=512):
    n, d = x.shape
    f = w_down.shape[0]
    nf = w_gate_up.shape[0]
    tf = f // nf
    (out,), cast_out = _pallas_with_casts(
        _dense_ffn_kernel,
        grid=(n // tm, nf),
        in_specs=[
            pl.BlockSpec((tm, d), lambda i, j: (i, 0)),
            _resident((1, d)),
            pl.BlockSpec((None, d, 2 * tf), lambda i, j: (j, 0, 0)),
            pl.BlockSpec((tf, d), lambda i, j: (j, 0)),
        ],
        out_specs=[pl.BlockSpec((tm, d), lambda i, j: (i, 0))],
        out_shape=[jax.ShapeDtypeStruct((n, d), F32)],
        operands=(x, gain, w_gate_up, w_down),
        casts=casts,
        scratch_shapes=[pltpu.VMEM((tm, d), BF16)],
        name="dense_ffn",
    )
    return out, cast_out


def _gmlp_in_kernel(x_ref, g_ref, w_ref, vg_ref, u_ref, v_ref, vf_ref, *, col_chunk):
    h = _rms(x_ref[...], g_ref[...]).astype(BF16)
    width = u_ref.shape[1]
    ssq = jnp.zeros((x_ref.shape[0], 1), F32)
    for c0 in range(0, width, col_chunk):
        z = jnp.dot(h, w_ref[:, width + c0:width + c0 + col_chunk], preferred_element_type=F32)
        v = jax.nn.gelu(z)
        ssq = ssq + jnp.sum(v * v, axis=-1, keepdims=True)
        vf_ref[:, c0:c0 + col_chunk] = v
    inv = lax.rsqrt(ssq / width + EPS)
    v_ref[...] = (vf_ref[...] * inv * vg_ref[...]).astype(BF16)
    for c0 in range(0, width, col_chunk):
        z = jnp.dot(h, w_ref[:, c0:c0 + col_chunk], preferred_element_type=F32)
        u_ref[:, c0:c0 + col_chunk] = jax.nn.gelu(z).astype(BF16)


def _gmlp_in(x, gain, w_in, v_gain, casts, tm=256):
    n, d = x.shape
    width = w_in.shape[1] // 2
    kern = functools.partial(_gmlp_in_kernel, col_chunk=2048)
    (u, v), cast_out = _pallas_with_casts(
        kern,
        grid=(n // tm,),
        in_specs=[
            pl.BlockSpec((tm, d), lambda i: (i, 0)),
            _resident((1, d)),
            _resident((d, 2 * width)),
            _resident((1, width)),
        ],
        out_specs=[
            pl.BlockSpec((tm, width), lambda i: (i, 0)),
            pl.BlockSpec((tm, width), lambda i: (i, 0)),
        ],
        out_shape=[
            jax.ShapeDtypeStruct((n, width), BF16),
            jax.ShapeDtypeStruct((n, width), BF16),
        ],
        operands=(x, gain, w_in, v_gain),
        casts=casts,
        scratch_shapes=[pltpu.VMEM((tm, width), F32)],
        name="gmlp_in",
    )
    return u, v, cast_out


def _top2_route(logits):
    lane = lax.broadcasted_iota(jnp.int32, logits.shape, 1)
    lg = jnp.where(lane < N_EXPERTS, logits, -jnp.inf)
    m1 = jnp.max(lg, axis=-1, keepdims=True)
    i1 = jnp.min(jnp.where(lg == m1, lane, LANES), axis=-1, keepdims=True)
    lg2 = jnp.where(lane == i1, -jnp.inf, lg)
    m2 = jnp.max(lg2, axis=-1, keepdims=True)
    i2 = jnp.min(jnp.where(lg2 == m2, lane, LANES), axis=-1, keepdims=True)
    e2 = jnp.exp(m2 - m1)
    w1 = 1.0 / (1.0 + e2)
    w2 = e2 / (1.0 + e2)
    idx = jnp.where(lane == 0, i1, jnp.where(lane == 1, i2, 0))
    gate = jnp.where(lane == 0, w1, jnp.where(lane == 1, w2, 0.0))
    return idx, gate


def _gmlp_out_router_kernel(u_ref, v_ref, ws_ref, bs_ref, wo_ref, r_ref, g_ref, wr_ref,
                            x_ref, h_ref, idx_ref, gate_ref, y_ref):
    tm, width = u_ref.shape
    for c0 in range(0, tm, CHUNK):
        for g in range(width // LANES):
            cols = slice(g * LANES, (g + 1) * LANES)
            vv = v_ref[c0:c0 + CHUNK, cols]
            mixed = jnp.dot(ws_ref[g], vv, preferred_element_type=F32) + bs_ref[:, cols]
            y_ref[c0:c0 + CHUNK, cols] = (u_ref[c0:c0 + CHUNK, cols].astype(F32) * mixed).astype(BF16)
    x = r_ref[...] + jnp.dot(y_ref[...], wo_ref[...], preferred_element_type=F32)
    x_ref[...] = x
    h = _rms(x, g_ref[...])
    h_ref[...] = h
    h_hi = h.astype(BF16)
    h_lo = (h - h_hi.astype(F32)).astype(BF16)
    a = jnp.dot(h_hi, wr_ref[...], preferred_element_type=F32)
    b = jnp.dot(h_lo, wr_ref[:, 0:LANES], preferred_element_type=F32)
    logits = a[:, 0:LANES] + (a[:, LANES:2 * LANES] + b)
    idx_ref[...], gate_ref[...] = _top2_route(logits)


def _gmlp_out_router(u, v, w_s, b_full, w_out, res, ffn_gain, w_router_split, tm=512):
    n, width = u.shape
    d = w_out.shape[1]
    groups = w_s.shape[0]
    row = lambda i: (i, 0)
    return pl.pallas_call(
        _gmlp_out_router_kernel,
        grid=(n // tm,),
        in_specs=[
            pl.BlockSpec((tm, width), row),
            pl.BlockSpec((tm, width), row),
            _resident((groups, CHUNK, CHUNK)),
            _resident((CHUNK, width)),
            _resident((width, d)),
            pl.BlockSpec((tm, d), row),
            _resident((1, d)),
            _resident((d, 2 * LANES)),
        ],
        out_specs=[
            pl.BlockSpec((tm, d), row),
            pl.BlockSpec((tm, d), row),
            pl.BlockSpec((tm, LANES), row),
            pl.BlockSpec((tm, LANES), row),
        ],
        out_shape=[
            jax.ShapeDtypeStruct((n, d), F32),
            jax.ShapeDtypeStruct((n, d), F32),
            jax.ShapeDtypeStruct((n, LANES), jnp.int32),
            jax.ShapeDtypeStruct((n, LANES), F32),
        ],
        scratch_shapes=[pltpu.VMEM((tm, width), BF16)],
        compiler_params=_params("parallel"),
        name="gmlp_out_router",
    )(u, v, w_s, b_full, w_out, res, ffn_gain, w_router_split)


def _moe_ffn_kernel(te_ref, nv_ref, tok_ref, h_hbm, wg_ref, wu_ref, wd_ref, o_ref,
                    xbuf, hb_ref, sem, *, rows_per_step, row_paths):
    i = pl.program_id(0)
    j = pl.program_id(1)
    n_tiles = pl.num_programs(0)
    nf = pl.num_programs(1)
    tm = hb_ref.shape[0]
    rows_buf = xbuf.shape[0]

    def row_copy(tile, r):
        tok = tok_ref[tile * tm + r]
        return pltpu.make_async_copy(h_hbm.at[pl.ds(tok, 1), :], xbuf.at[pl.ds(r, 1), :], sem)

    def wait_rows():
        pltpu.make_async_copy(h_hbm.at[pl.ds(0, rows_buf), :], xbuf, sem).wait()

    def prefetch_next_tile():
        for k in range(rows_per_step):
            row_copy(i + 1, j * rows_per_step + k).start(priority=1)

    @pl.when((i == 0) & (j == 0))
    def _():
        def issue(r, carry):
            row_copy(0, r).start()
            return carry
        lax.fori_loop(0, rows_buf, issue, 0)

    @pl.when(j == 0)
    def _():
        wait_rows()
        hb_ref[...] = xbuf[0:tm, :].astype(BF16)
        o_ref[...] = jnp.zeros_like(o_ref)

    def swiglu(rows):
        h = hb_ref[0:rows, :]
        gt = jnp.dot(h, wg_ref[...], preferred_element_type=F32)
        up = jnp.dot(h, wu_ref[...], preferred_element_type=F32)
        act = (gt * jax.nn.sigmoid(gt) * up).astype(BF16)
        o_ref[0:rows, :] += jnp.dot(act, wd_ref[...], preferred_element_type=F32)

    n_valid = nv_ref[i]

    for rows, below in zip(row_paths, row_paths[1:] + (0,)):
        @pl.when((n_valid > below) & (n_valid <= rows))
        def _(rows=rows):
            prefetch_next_tile()
            swiglu(rows)

    @pl.when(n_valid == 0)
    def _():
        prefetch_next_tile()

    @pl.when((i == n_tiles - 1) & (j == nf - 1))
    def _():
        wait_rows()


def _moe_ffn(h, tile_expert, tile_valid, row_token, we_gate, we_up, we_down, row_paths):
    tm = row_paths[0]
    n, d = h.shape
    nf, tf = we_gate.shape[1], we_gate.shape[3]
    n_tiles = tile_expert.shape[0]
    p = n_tiles * tm
    rows_per_step = -(-tm // (nf * SUBLANES)) * SUBLANES
    rows_buf = rows_per_step * nf
    row_token = jnp.pad(row_token, (0, tm + rows_buf))

    def f_tile(i, j, nv):
        return jnp.where(nv[i] > 0, j, nf - 1)

    def w_in_map(i, j, te, nv, tok):
        return (te[i], f_tile(i, j, nv), 0, 0)

    def w_out_map(i, j, te, nv, tok):
        return (te[i], f_tile(i, j, nv), 0)

    grid_spec = pltpu.PrefetchScalarGridSpec(
        num_scalar_prefetch=3,
        grid=(n_tiles, nf),
        in_specs=[
            pl.BlockSpec(memory_space=pl.ANY),
            pl.BlockSpec((None, None, d, tf), w_in_map),
            pl.BlockSpec((None, None, d, tf), w_in_map),
            pl.BlockSpec((None, tf, d), w_out_map),
        ],
        out_specs=pl.BlockSpec((tm, d), lambda i, j, te, nv, tok: (i, 0)),
        scratch_shapes=[
            pltpu.VMEM((rows_buf, d), F32),
            pltpu.VMEM((tm, d), BF16),
            pltpu.SemaphoreType.DMA(()),
        ],
    )
    return pl.pallas_call(
        functools.partial(_moe_ffn_kernel, rows_per_step=rows_per_step, row_paths=tuple(row_paths)),
        grid_spec=grid_spec,
        out_shape=jax.ShapeDtypeStruct((p, d), F32),
        compiler_params=_params("arbitrary", "arbitrary"),
        name="moe_ffn",
    )(tile_expert, tile_valid, row_token, h, we_gate, we_up, we_down)


def _moe_combine_kernel(pos_ref, x_ref, gate_ref, y_hbm, o_ref, buf, sems):
    i = pl.program_id(0)
    tm = x_ref.shape[0]
    slot = i % 2

    def row_copies(tile, r, dst_slot):
        for k in range(TOP_K):
            p = pos_ref[(tile * tm + r) * TOP_K + k]
            pltpu.make_async_copy(y_hbm.at[pl.ds(p, 1), :], buf.at[dst_slot, k, pl.ds(r, 1), :],
                                  sems.at[dst_slot]).start(priority=k)

    def wait_slot(s):
        for k in range(TOP_K):
            pltpu.make_async_copy(y_hbm.at[pl.ds(0, tm), :], buf.at[s, k], sems.at[s]).wait()

    @pl.when(i == 0)
    def _():
        def issue(r, carry):
            row_copies(0, r, 0)
            return carry
        lax.fori_loop(0, tm, issue, 0, unroll=8)

    wait_slot(slot)
    chunk = 4 * SUBLANES
    for r0 in range(0, tm, chunk):
        for r in range(r0, r0 + chunk):
            row_copies(i + 1, r, 1 - slot)
        rows = slice(r0, r0 + chunk)
        gate = gate_ref[rows, :]
        o_ref[rows, :] = x_ref[rows, :] + (gate[:, 0:1] * buf[slot, 0, rows, :]
                                           + gate[:, 1:2] * buf[slot, 1, rows, :])

    @pl.when(i == pl.num_programs(0) - 1)
    def _():
        wait_slot(1 - slot)


def _moe_combine(x, gate, y_sorted, pos, tm=256):
    n, d = x.shape
    pos = jnp.pad(pos, (0, TOP_K * tm))
    grid_spec = pltpu.PrefetchScalarGridSpec(
        num_scalar_prefetch=1,
        grid=(n // tm,),
        in_specs=[
            pl.BlockSpec((tm, d), lambda i, pos: (i, 0)),
            pl.BlockSpec((tm, LANES), lambda i, pos: (i, 0)),
            pl.BlockSpec(memory_space=pl.ANY),
        ],
        out_specs=pl.BlockSpec((tm, d), lambda i, pos: (i, 0)),
        scratch_shapes=[
            pltpu.VMEM((2, TOP_K, tm, d), F32),
            pltpu.SemaphoreType.DMA((2,)),
        ],
    )
    return pl.pallas_call(
        _moe_combine_kernel,
        grid_spec=grid_spec,
        out_shape=jax.ShapeDtypeStruct((n, d), F32),
        compiler_params=_params("arbitrary"),
        name="moe_combine",
    )(pos, x, gate, y_sorted)


def _route_plan(top_idx, tm, ext):
    n = top_idx.shape[0]
    n_pairs = n * TOP_K
    stride = tm + ext
    n_tiles = (n_pairs + N_EXPERTS * (tm - 1)) // tm
    hot = top_idx[:, :, None] == jnp.arange(N_EXPERTS, dtype=jnp.int32)[None, None, :]
    per_token = jnp.sum(hot, axis=1, dtype=jnp.int32)
    csum = jnp.cumsum(per_token, axis=0)
    counts = csum[-1]
    full, rem = counts // tm, counts % tm
    own_tile = (rem > ext) | ((rem > 0) & (full == 0))
    tiles_per = full + own_tile.astype(jnp.int32)
    tile_end = jnp.cumsum(tiles_per)
    tile_first = tile_end - tiles_per
    rank = csum - per_token
    in_tile = jnp.minimum(rank // tm, tiles_per[None, :] - 1)
    slot = (tile_first[None, :] + in_tile) * stride + (rank - in_tile * tm)
    pos = jnp.sum(jnp.where(hot, slot[:, None, :], 0), axis=2).reshape(n_pairs).astype(jnp.int32)
    tile_ids = jnp.arange(n_tiles, dtype=jnp.int32)
    tile_expert = jnp.sum((tile_end[None, :] <= tile_ids[:, None]).astype(jnp.int32), axis=1)
    tile_expert = jnp.minimum(tile_expert, N_EXPERTS - 1).astype(jnp.int32)
    in_expert = tile_ids - tile_first[tile_expert]
    is_last = in_expert == tiles_per[tile_expert] - 1
    tile_valid = jnp.where(is_last, counts[tile_expert] - in_expert * tm, tm)
    tile_valid = jnp.where(tile_ids < tile_end[-1], tile_valid, 0).astype(jnp.int32)
    row_token = jnp.zeros((n_tiles * stride,), jnp.int32).at[pos].set(
        jnp.arange(n_pairs, dtype=jnp.int32) // TOP_K)
    return tile_expert, tile_valid, row_token, pos


def kernel(x, l0_mix_norm, l0_w_qkv, l0_q_norm, l0_k_norm, l0_sink, l0_w_o, l0_ffn_norm, l0_w_gate_up, l0_w_down, l1_mix_norm, l1_w_in, l1_v_norm, l1_w_s, l1_b_s, l1_w_out, l1_ffn_norm, l1_w_router, l1_we_gate, l1_we_up, l1_we_down):
    batch, seq, d = x.shape
    n = batch * seq
    x0 = x.reshape(n, d)
    dq = N_HEADS * HEAD_DIM
    dkv = N_KV_HEADS * HEAD_DIM

    head_gain = jnp.concatenate([
        jnp.tile(l0_q_norm * (HEAD_DIM ** -0.5), N_HEADS),
        jnp.tile(l0_k_norm, N_KV_HEADS),
        jnp.ones((dkv,), F32),
    ])[None, :]
    qkv, (w_gate_up, w_o, w_down) = _qkv_proj(
        x0, l0_mix_norm[None, :], l0_w_qkv.astype(BF16), head_gain, dq + dkv,
        casts=[(l0_w_gate_up, DENSE_TF, None, MXU_COLS), (l0_w_o, None, None), (l0_w_down, None, None)])
    attn, (we_gate,) = _attention(qkv, l0_sink, batch, seq, casts=[(l1_we_gate, MOE_TF, None)])
    x1, _ = _proj_residual(attn, w_o, x0, casts=[], name="attn_out_proj")

    x2, (we_down, w_in, w_out) = _dense_ffn(
        x1, l0_ffn_norm[None, :], w_gate_up, w_down,
        casts=[(l1_we_down, None, 256), (l1_w_in, None, 16), (l1_w_out, None, 16)])

    u, v, (we_up,) = _gmlp_in(x2, l1_mix_norm[None, :], w_in, l1_v_norm[None, :],
                              casts=[(l1_we_up, MOE_TF, None)])
    b_full = jnp.repeat(l1_b_s.T, LANES, axis=1)
    w_router = jnp.pad(l1_w_router, ((0, 0), (0, LANES - N_EXPERTS)))
    w_router_hi = w_router.astype(BF16)
    w_router_lo = (w_router - w_router_hi.astype(F32)).astype(BF16)
    w_router_split = jnp.concatenate([w_router_hi, w_router_lo], axis=1)
    x3, h, ridx, rgate = _gmlp_out_router(u, v, l1_w_s.astype(BF16), b_full, w_out, x2,
                                          l1_ffn_norm[None, :], w_router_split)

    tile_expert, tile_valid, row_token, pos = _route_plan(ridx[:, :TOP_K], MOE_TM, MOE_TM_EXT)
    y_sorted = _moe_ffn(h, tile_expert, tile_valid, row_token, we_gate, we_up, we_down,
                        row_paths=(MOE_TM + MOE_TM_EXT, MOE_TM, MOE_TM_SMALL))
    out = _moe_combine(x3, rgate, y_sorted, pos)
    return out.reshape(batch, seq, d)
```

```python
import functools

import jax
import jax.numpy as jnp
import numpy as np
from jax import lax
from jax.experimental import pallas as pl
from jax.experimental.pallas import tpu as pltpu

F32 = jnp.float32
BF16 = jnp.bfloat16

EPS = 1e-6
NEG_INF = -1e30
LANES = 128
SUBLANES = 8
HEAD_DIM = 128
N_HEADS = 16
N_KV_HEADS = 4
WINDOW = 128
BLOCK = 128
CHUNK = 128
N_EXPERTS = 8
TOP_K = 2
MOE_TF = 1024
MOE_TM = 512
MXU_COLS = 256
DENSE_TF = 512

MIB = 1024 * 1024
VMEM_LIMIT = 56 * MIB


def _params(*sem):
    return pltpu.CompilerParams(dimension_semantics=sem, vmem_limit_bytes=VMEM_LIMIT)


def _resident(shape):
    zeros = (0,) * len(shape)
    return pl.BlockSpec(shape, lambda *_: zeros, pipeline_mode=pl.Buffered(1))


class _CastJob:
    def __init__(self, w, grid, col_tile=None, block_rows=None, pair_width=None):
        self.pair_width = pair_width
        self.stacked = w.ndim == 3
        self.e, self.k, self.f = w.shape if self.stacked else (1,) + w.shape
        self.src = w.reshape(self.e * self.k, self.f)
        steps = 1
        for g in grid:
            steps *= g
        if block_rows is None:
            assert (self.e * self.k) % steps == 0
            block_rows = self.e * self.k // steps
        assert (self.e * self.k) % block_rows == 0 and self.k % block_rows == 0
        self.block_rows = block_rows
        self.n_blocks = self.e * self.k // block_rows
        assert self.n_blocks <= steps, (self.n_blocks, steps)
        strides = [1] * len(grid)
        for a in range(len(grid) - 2, -1, -1):
            strides[a] = strides[a + 1] * grid[a + 1]
        self.strides = tuple(strides)
        self.col_tile = col_tile
        assert col_tile is None or self.f % col_tile == 0

    def _step(self, ids):
        return sum(i * s for i, s in zip(ids, self.strides))

    def _block(self, ids):
        return jnp.minimum(self._step(ids[:len(self.strides)]), self.n_blocks - 1)

    def src_spec(self):
        return pl.BlockSpec((self.block_rows, self.f), lambda *ids: (self._block(ids), 0))

    def _tiles(self):
        if self.pair_width is not None:
            return self.f // (2 * self.col_tile), 2 * self.col_tile
        return self.f // self.col_tile, self.col_tile

    def dst_spec(self):
        if self.col_tile is None:
            return self.src_spec()
        per_expert = self.k // self.block_rows
        n_tiles, width = self._tiles()
        return pl.BlockSpec(
            (None, n_tiles, self.block_rows, width),
            lambda *ids: (self._block(ids) // per_expert, 0, self._block(ids) % per_expert, 0))

    def out_shape(self):
        if self.col_tile is None:
            return jax.ShapeDtypeStruct(self.src.shape, BF16)
        n_tiles, width = self._tiles()
        return jax.ShapeDtypeStruct((self.e, n_tiles, self.k, width), BF16)

    def _cast_block(self, src_ref, dst_ref):
        ct, pw = self.col_tile, self.pair_width
        if ct is None:
            dst_ref[...] = src_ref[...].astype(BF16)
        elif pw is not None:
            half = self.f // 2
            for c in range(half // ct):
                for s in range(ct // pw):
                    a0 = c * ct + s * pw
                    dst_ref[c, :, 2 * s * pw:(2 * s + 1) * pw] = src_ref[:, a0:a0 + pw].astype(BF16)
                    dst_ref[c, :, (2 * s + 1) * pw:(2 * s + 2) * pw] = (
                        src_ref[:, half + a0:half + a0 + pw].astype(BF16))
        else:
            for c in range(self.f // ct):
                dst_ref[c] = src_ref[:, c * ct:(c + 1) * ct].astype(BF16)

    def run(self, src_ref, dst_ref):
        self._cast_block(src_ref, dst_ref)

    def result(self, dst):
        out = dst if self.col_tile is not None else dst.reshape(self.e, self.k, self.f)
        return out if self.stacked else out[0]


def _pallas_with_casts(kernel_fn, *, grid, in_specs, out_specs, out_shape, operands, casts=(),
                       scratch_shapes=(), name):
    jobs = [_CastJob(w, grid, *options) for (w, *options) in casts]
    n_in, n_out, n_jobs = len(in_specs), len(out_specs), len(jobs)

    def body(*refs):
        ins, srcs = refs[:n_in], refs[n_in:n_in + n_jobs]
        outs = refs[n_in + n_jobs:n_in + n_jobs + n_out]
        dsts = refs[n_in + n_jobs + n_out:n_in + 2 * n_jobs + n_out]
        for job, src, dst in zip(jobs, srcs, dsts):
            job.run(src, dst)
        kernel_fn(*ins, *outs, *refs[n_in + 2 * n_jobs + n_out:])

    results = pl.pallas_call(
        body,
        grid=grid,
        in_specs=list(in_specs) + [job.src_spec() for job in jobs],
        out_specs=list(out_specs) + [job.dst_spec() for job in jobs],
        out_shape=list(out_shape) + [job.out_shape() for job in jobs],
        scratch_shapes=list(scratch_shapes),
        compiler_params=_params(*(["arbitrary"] * len(grid))),
        name=name,
    )(*operands, *[job.src for job in jobs])
    return results[:n_out], [job.result(r) for job, r in zip(jobs, results[n_out:])]


def _rms(x, gain):
    ms = jnp.mean(x * x, axis=-1, keepdims=True)
    return x * lax.rsqrt(ms + EPS) * gain


def _qkv_kernel(x_ref, g_ref, w_ref, hg_ref, o_ref, *, n_norm_cols, col_chunk):
    h = _rms(x_ref[...], g_ref[...]).astype(BF16)
    n_out = o_ref.shape[1]
    for c0 in range(0, n_out, col_chunk):
        y = jnp.dot(h, w_ref[:, c0:c0 + col_chunk], preferred_element_type=F32)
        for h0 in range(0, col_chunk, HEAD_DIM):
            col = c0 + h0
            yh = y[:, h0:h0 + HEAD_DIM]
            if col < n_norm_cols:
                yh = _rms(yh, hg_ref[:, col:col + HEAD_DIM])
            o_ref[:, col:col + HEAD_DIM] = yh.astype(BF16)


def _qkv_proj(x, gain, w, head_gain, n_norm_cols, casts, tm=512):
    n, d = x.shape
    nq = w.shape[1]
    kern = functools.partial(_qkv_kernel, n_norm_cols=n_norm_cols, col_chunk=512)
    (qkv,), cast_out = _pallas_with_casts(
        kern,
        grid=(n // tm,),
        in_specs=[
            pl.BlockSpec((tm, d), lambda i: (i, 0)),
            _resident((1, d)),
            _resident((d, nq)),
            _resident((1, nq)),
        ],
        out_specs=[pl.BlockSpec((tm, nq), lambda i: (i, 0))],
        out_shape=[jax.ShapeDtypeStruct((n, nq), BF16)],
        operands=(x, gain, w, head_gain),
        casts=casts,
        name="qkv_proj",
    )
    return qkv, cast_out


def _attn_kernel(sink_ref, q_ref, k0_ref, k1_ref, k2_ref, k3_ref, v0_ref, v1_ref, v2_ref, v3_ref,
                 bias_ref, o_ref):
    j = pl.program_id(1)
    kcat = jnp.concatenate([k0_ref[...], k1_ref[...], k2_ref[...], k3_ref[...]], axis=0)
    vcat = jnp.concatenate([v0_ref[...], v1_ref[...], v2_ref[...], v3_ref[...]], axis=0)
    group = N_HEADS // N_KV_HEADS
    band = 3 * BLOCK
    variants = (jnp.where(j == 0, 0, 1), jnp.where(j == pl.num_programs(1) - 1, 2, 1))
    for a in range(2):
        rows = slice(a * BLOCK, (a + 1) * BLOCK)
        for kh in range(N_KV_HEADS):
            cols = slice(kh * HEAD_DIM, (kh + 1) * HEAD_DIM)
            k_h = kcat[a * BLOCK:a * BLOCK + band, cols]
            v_h = vcat[a * BLOCK:a * BLOCK + band, cols]
            heads = [kh * group + g for g in range(group)]
            q_g = jnp.concatenate([q_ref[rows, hd * HEAD_DIM:(hd + 1) * HEAD_DIM] for hd in heads], axis=0)
            sink = jnp.concatenate([jnp.full((1, BLOCK), sink_ref[hd], F32) for hd in heads], axis=1)
            s = lax.dot_general(k_h, q_g, (((1,), (1,)), ((), ())), preferred_element_type=F32)
            s = s + bias_ref[variants[a], kh * band:(kh + 1) * band, :]
            m = jnp.maximum(jnp.max(s, axis=0, keepdims=True), sink)
            p = jnp.exp(s - m)
            denom = jnp.sum(p, axis=0, keepdims=True) + jnp.exp(sink - m)
            o_t = lax.dot_general(v_h, p.astype(BF16), (((0,), (0,)), ((), ())),
                                  preferred_element_type=F32) * (1.0 / denom)
            for g, hd in enumerate(heads):
                o_ref[rows, hd * HEAD_DIM:(hd + 1) * HEAD_DIM] = (
                    o_t[:, g * BLOCK:(g + 1) * BLOCK].T.astype(BF16))


def _attention_bias():
    group = N_HEADS // N_KV_HEADS
    qi = np.arange(BLOCK)[:, None]
    sj = np.arange(3 * BLOCK)[None, :]
    dist = np.abs(qi - sj + BLOCK)
    slopes = np.exp2(-8.0 * np.arange(1, N_HEADS + 1, dtype=np.float64) / N_HEADS)
    bias = np.where(dist <= WINDOW, -slopes[:, None, None] * dist, NEG_INF)
    first = (sj >= BLOCK)[None]
    last = (sj < 2 * BLOCK)[None]
    edge = np.stack([np.where(first, bias, NEG_INF), bias, np.where(last, bias, NEG_INF)])
    edge = edge.reshape(3, N_KV_HEADS, group, BLOCK, 3 * BLOCK).transpose(0, 1, 4, 2, 3)
    return np.ascontiguousarray(edge.reshape(3, N_KV_HEADS * 3 * BLOCK, group * BLOCK), dtype=np.float32)


def _attention(qkv, sink, batch, seq, casts):
    n = qkv.shape[0]
    nb = seq // BLOCK
    assert nb % 2 == 0
    steps = nb // 2
    dq = N_HEADS * HEAD_DIM
    dkv = N_KV_HEADS * HEAD_DIM
    k_col = dq // dkv
    v_col = k_col + 1
    group = N_HEADS // N_KV_HEADS

    def kv_spec(offset, col):
        return pl.BlockSpec(
            (BLOCK, dkv), lambda b, j: (b * nb + jnp.clip(2 * j + offset, 0, nb - 1), col))

    (attn,), cast_out = _pallas_with_casts(
        _attn_kernel,
        grid=(batch, steps),
        in_specs=[
            pl.BlockSpec(memory_space=pltpu.SMEM),
            pl.BlockSpec((2 * BLOCK, dq), lambda b, j: (b * steps + j, 0)),
            kv_spec(-1, k_col), kv_spec(0, k_col), kv_spec(1, k_col), kv_spec(2, k_col),
            kv_spec(-1, v_col), kv_spec(0, v_col), kv_spec(1, v_col), kv_spec(2, v_col),
            _resident((3, N_KV_HEADS * 3 * BLOCK, group * BLOCK)),
        ],
        out_specs=[pl.BlockSpec((2 * BLOCK, dq), lambda b, j: (b * steps + j, 0))],
        out_shape=[jax.ShapeDtypeStruct((n, dq), BF16)],
        operands=(sink, qkv, qkv, qkv, qkv, qkv, qkv, qkv, qkv, qkv, _attention_bias()),
        casts=casts,
        name="window_attention",
    )
    return attn, cast_out


def _proj_res_kernel(a_ref, w_ref, r_ref, o_ref):
    o_ref[...] = r_ref[...] + jnp.dot(a_ref[...], w_ref[...], preferred_element_type=F32)


def _proj_residual(a, w, res, casts, tm=512, name="proj_residual"):
    n, k = a.shape
    d = w.shape[1]
    (out,), cast_out = _pallas_with_casts(
        _proj_res_kernel,
        grid=(n // tm,),
        in_specs=[
            pl.BlockSpec((tm, k), lambda i: (i, 0)),
            _resident((k, d)),
            pl.BlockSpec((tm, d), lambda i: (i, 0)),
        ],
        out_specs=[pl.BlockSpec((tm, d), lambda i: (i, 0))],
        out_shape=[jax.ShapeDtypeStruct((n, d), F32)],
        operands=(a, w, res),
        casts=casts,
        name=name,
    )
    return out, cast_out


def _swiglu_paired(h, wgu_ref):
    gu = jnp.dot(h, wgu_ref[...], preferred_element_type=F32)
    acts = []
    for c0 in range(0, gu.shape[1], 2 * MXU_COLS):
        gt = gu[:, c0:c0 + MXU_COLS]
        up = gu[:, c0 + MXU_COLS:c0 + 2 * MXU_COLS]
        acts.append((gt * jax.nn.sigmoid(gt) * up).astype(BF16))
    return jnp.concatenate(acts, axis=1)


def _dense_ffn_kernel(x_ref, g_ref, wgu_ref, wd_ref, o_ref, h_ref):
    @pl.when(pl.program_id(1) == 0)
    def _():
        x = x_ref[...]
        h_ref[...] = _rms(x, g_ref[...]).astype(BF16)
        o_ref[...] = x

    act = _swiglu_paired(h_ref[...], wgu_ref)
    o_ref[...] += jnp.dot(act, wd_ref[...], preferred_element_type=F32)


def _dense_ffn(x, gain, w_gate_up, w_down, casts, tm=512):
    n, d = x.shape
    f = w_down.shape[0]
    nf = w_gate_up.shape[0]
    tf = f // nf
    (out,), cast_out = _pallas_with_casts(
        _dense_ffn_kernel,
        grid=(n // tm, nf),
        in_specs=[
            pl.BlockSpec((tm, d), lambda i, j: (i, 0)),
            _resident((1, d)),
            pl.BlockSpec((None, d, 2 * tf), lambda i, j: (j, 0, 0)),
            pl.BlockSpec((tf, d), lambda i, j: (j, 0)),
        ],
        out_specs=[pl.BlockSpec((tm, d), lambda i, j: (i, 0))],
        out_shape=[jax.ShapeDtypeStruct((n, d), F32)],
        operands=(x, gain, w_gate_up, w_down),
        casts=casts,
        scratch_shapes=[pltpu.VMEM((tm, d), BF16)],
        name="dense_ffn",
    )
    return out, cast_out


def _gmlp_in_kernel(x_ref, g_ref, w_ref, vg_ref, u_ref, v_ref):
    h = _rms(x_ref[...], g_ref[...]).astype(BF16)
    v = jax.nn.gelu(jnp.dot(h, w_ref[1], preferred_element_type=F32))
    v_ref[...] = _rms(v, vg_ref[...]).astype(BF16)
    u_ref[...] = jax.nn.gelu(jnp.dot(h, w_ref[0], preferred_element_type=F32)).astype(BF16)


def _gmlp_in(x, gain, w_in, v_gain, casts, tm=256):
    n, d = x.shape
    width = w_in.shape[2]
    (u, v), cast_out = _pallas_with_casts(
        _gmlp_in_kernel,
        grid=(n // tm,),
        in_specs=[
            pl.BlockSpec((tm, d), lambda i: (i, 0)),
            _resident((1, d)),
            _resident((2, d, width)),
            _resident((1, width)),
        ],
        out_specs=[
            pl.BlockSpec((tm, width), lambda i: (i, 0)),
            pl.BlockSpec((tm, width), lambda i: (i, 0)),
        ],
        out_shape=[
            jax.ShapeDtypeStruct((n, width), BF16),
            jax.ShapeDtypeStruct((n, width), BF16),
        ],
        operands=(x, gain, w_in, v_gain),
        casts=casts,
        name="gmlp_in",
    )
    return u, v, cast_out


def _top2_route(logits):
    lane = lax.broadcasted_iota(jnp.int32, logits.shape, 1)
    lg = jnp.where(lane < N_EXPERTS, logits, -jnp.inf)
    m1 = jnp.max(lg, axis=-1, keepdims=True)
    i1 = jnp.min(jnp.where(lg == m1, lane, LANES), axis=-1, keepdims=True)
    lg2 = jnp.where(lane == i1, -jnp.inf, lg)
    m2 = jnp.max(lg2, axis=-1, keepdims=True)
    i2 = jnp.min(jnp.where(lg2 == m2, lane, LANES), axis=-1, keepdims=True)
    e2 = jnp.exp(m2 - m1)
    w1 = 1.0 / (1.0 + e2)
    w2 = e2 / (1.0 + e2)
    idx = jnp.where(lane == 0, i1, jnp.where(lane == 1, i2, 0))
    gate = jnp.where(lane == 0, w1, jnp.where(lane == 1, w2, 0.0))
    return idx, gate


def _gmlp_out_router_kernel(u_ref, v_ref, ws_ref, bs_ref, wo_ref, r_ref, g_ref, wr_ref,
                            x_ref, h_ref, idx_ref, gate_ref, y_ref):
    tm, width = u_ref.shape
    for c0 in range(0, tm, CHUNK):
        for g in range(width // LANES):
            cols = slice(g * LANES, (g + 1) * LANES)
            vv = v_ref[c0:c0 + CHUNK, cols]
            mixed = jnp.dot(ws_ref[g], vv, preferred_element_type=F32) + bs_ref[:, cols]
            y_ref[c0:c0 + CHUNK, cols] = (u_ref[c0:c0 + CHUNK, cols].astype(F32) * mixed).astype(BF16)
    x = r_ref[...] + jnp.dot(y_ref[...], wo_ref[...], preferred_element_type=F32)
    x_ref[...] = x
    h = _rms(x, g_ref[...])
    h_ref[...] = h
    h_hi = h.astype(BF16)
    h_lo = (h - h_hi.astype(F32)).astype(BF16)
    a = jnp.dot(h_hi, wr_ref[...], preferred_element_type=F32)
    b = jnp.dot(h_lo, wr_ref[:, 0:LANES], preferred_element_type=F32)
    logits = a[:, 0:LANES] + (a[:, LANES:2 * LANES] + b)
    idx_ref[...], gate_ref[...] = _top2_route(logits)


def _gmlp_out_router(u, v, w_s, b_full, w_out, res, ffn_gain, w_router_split, tm=512):
    n, width = u.shape
    d = w_out.shape[1]
    groups = w_s.shape[0]
    row = lambda i: (i, 0)
    return pl.pallas_call(
        _gmlp_out_router_kernel,
        grid=(n // tm,),
        in_specs=[
            pl.BlockSpec((tm, width), row),
            pl.BlockSpec((tm, width), row),
            _resident((groups, CHUNK, CHUNK)),
            _resident((CHUNK, width)),
            _resident((width, d)),
            pl.BlockSpec((tm, d), row),
            _resident((1, d)),
            _resident((d, 2 * LANES)),
        ],
        out_specs=[
            pl.BlockSpec((tm, d), row),
            pl.BlockSpec((tm, d), row),
            pl.BlockSpec((tm, LANES), row),
            pl.BlockSpec((tm, LANES), row),
        ],
        out_shape=[
            jax.ShapeDtypeStruct((n, d), F32),
            jax.ShapeDtypeStruct((n, d), F32),
            jax.ShapeDtypeStruct((n, LANES), jnp.int32),
            jax.ShapeDtypeStruct((n, LANES), F32),
        ],
        scratch_shapes=[pltpu.VMEM((tm, width), BF16)],
        compiler_params=_params("parallel"),
        name="gmlp_out_router",
    )(u, v, w_s, b_full, w_out, res, ffn_gain, w_router_split)


def _moe_ffn_kernel(te_ref, nv_ref, tok_ref, h_hbm, wg_ref, wu_ref, wd_ref, o_ref,
                    xbuf, hb_ref, sem, *, rows_per_step, small_rows):
    i = pl.program_id(0)
    j = pl.program_id(1)
    n_tiles = pl.num_programs(0)
    nf = pl.num_programs(1)
    tm = hb_ref.shape[0]
    rows_buf = xbuf.shape[0]

    def row_copy(tile, r):
        tok = tok_ref[tile * tm + r]
        return pltpu.make_async_copy(h_hbm.at[pl.ds(tok, 1), :], xbuf.at[pl.ds(r, 1), :], sem)

    def wait_rows():
        pltpu.make_async_copy(h_hbm.at[pl.ds(0, rows_buf), :], xbuf, sem).wait()

    def prefetch_next_tile():
        for k in range(rows_per_step):
            row_copy(i + 1, j * rows_per_step + k).start(priority=1)

    @pl.when((i == 0) & (j == 0))
    def _():
        def issue(r, carry):
            row_copy(0, r).start()
            return carry
        lax.fori_loop(0, rows_buf, issue, 0)

    @pl.when(j == 0)
    def _():
        wait_rows()
        hb_ref[...] = xbuf[0:tm, :].astype(BF16)
        o_ref[...] = jnp.zeros_like(o_ref)

    def swiglu(rows):
        h = hb_ref[0:rows, :]
        gt = jnp.dot(h, wg_ref[...], preferred_element_type=F32)
        up = jnp.dot(h, wu_ref[...], preferred_element_type=F32)
        act = (gt * jax.nn.sigmoid(gt) * up).astype(BF16)
        o_ref[0:rows, :] += jnp.dot(act, wd_ref[...], preferred_element_type=F32)

    n_valid = nv_ref[i]

    @pl.when(n_valid > small_rows)
    def _():
        prefetch_next_tile()
        swiglu(tm)

    @pl.when((n_valid > 0) & (n_valid <= small_rows))
    def _():
        prefetch_next_tile()
        swiglu(small_rows)

    @pl.when(n_valid == 0)
    def _():
        prefetch_next_tile()

    @pl.when((i == n_tiles - 1) & (j == nf - 1))
    def _():
        wait_rows()


def _moe_ffn(h, tile_expert, tile_valid, row_token, we_gate, we_up, we_down, tm, small_rows=128):
    n, d = h.shape
    nf, tf = we_gate.shape[1], we_gate.shape[3]
    n_tiles = tile_expert.shape[0]
    p = n_tiles * tm
    rows_per_step = -(-tm // (nf * SUBLANES)) * SUBLANES
    rows_buf = rows_per_step * nf
    row_token = jnp.pad(row_token, (0, tm + rows_buf))

    def f_tile(i, j, nv):
        return jnp.where(nv[i] > 0, j, nf - 1)

    def w_in_map(i, j, te, nv, tok):
        return (te[i], f_tile(i, j, nv), 0, 0)

    def w_out_map(i, j, te, nv, tok):
        return (te[i], f_tile(i, j, nv), 0)

    grid_spec = pltpu.PrefetchScalarGridSpec(
        num_scalar_prefetch=3,
        grid=(n_tiles, nf),
        in_specs=[
            pl.BlockSpec(memory_space=pl.ANY),
            pl.BlockSpec((None, None, d, tf), w_in_map),
            pl.BlockSpec((None, None, d, tf), w_in_map),
            pl.BlockSpec((None, tf, d), w_out_map),
        ],
        out_specs=pl.BlockSpec((tm, d), lambda i, j, te, nv, tok: (i, 0)),
        scratch_shapes=[
            pltpu.VMEM((rows_buf, d), F32),
            pltpu.VMEM((tm, d), BF16),
            pltpu.SemaphoreType.DMA(()),
        ],
    )
    return pl.pallas_call(
        functools.partial(_moe_ffn_kernel, rows_per_step=rows_per_step, small_rows=small_rows),
        grid_spec=grid_spec,
        out_shape=jax.ShapeDtypeStruct((p, d), F32),
        compiler_params=_params("arbitrary", "arbitrary"),
        name="moe_ffn",
    )(tile_expert, tile_valid, row_token, h, we_gate, we_up, we_down)


def _moe_combine_kernel(pos_ref, x_ref, gate_ref, y_hbm, o_ref, buf, sems):
    i = pl.program_id(0)
    tm = x_ref.shape[0]
    slot = i % 2

    def row_copies(tile, r, dst_slot):
        for k in range(TOP_K):
            p = pos_ref[(tile * tm + r) * TOP_K + k]
            pltpu.make_async_copy(y_hbm.at[pl.ds(p, 1), :], buf.at[dst_slot, k, pl.ds(r, 1), :],
                                  sems.at[dst_slot]).start(priority=k)

    def wait_slot(s):
        for k in range(TOP_K):
            pltpu.make_async_copy(y_hbm.at[pl.ds(0, tm), :], buf.at[s, k], sems.at[s]).wait()

    @pl.when(i == 0)
    def _():
        def issue(r, carry):
            row_copies(0, r, 0)
            return carry
        lax.fori_loop(0, tm, issue, 0, unroll=8)

    wait_slot(slot)
    chunk = 4 * SUBLANES
    for r0 in range(0, tm, chunk):
        for r in range(r0, r0 + chunk):
            row_copies(i + 1, r, 1 - slot)
        rows = slice(r0, r0 + chunk)
        gate = gate_ref[rows, :]
        o_ref[rows, :] = x_ref[rows, :] + (gate[:, 0:1] * buf[slot, 0, rows, :]
                                           + gate[:, 1:2] * buf[slot, 1, rows, :])

    @pl.when(i == pl.num_programs(0) - 1)
    def _():
        wait_slot(1 - slot)


def _moe_combine(x, gate, y_sorted, pos, tm=256):
    n, d = x.shape
    pos = jnp.pad(pos, (0, TOP_K * tm))
    grid_spec = pltpu.PrefetchScalarGridSpec(
        num_scalar_prefetch=1,
        grid=(n // tm,),
        in_specs=[
            pl.BlockSpec((tm, d), lambda i, pos: (i, 0)),
            pl.BlockSpec((tm, LANES), lambda i, pos: (i, 0)),
            pl.BlockSpec(memory_space=pl.ANY),
        ],
        out_specs=pl.BlockSpec((tm, d), lambda i, pos: (i, 0)),
        scratch_shapes=[
            pltpu.VMEM((2, TOP_K, tm, d), F32),
            pltpu.SemaphoreType.DMA((2,)),
        ],
    )
    return pl.pallas_call(
        _moe_combine_kernel,
        grid_spec=grid_spec,
        out_shape=jax.ShapeDtypeStruct((n, d), F32),
        compiler_params=_params("arbitrary"),
        name="moe_combine",
    )(pos, x, gate, y_sorted)


def _route_plan(top_idx, tm):
    n = top_idx.shape[0]
    n_pairs = n * TOP_K
    n_tiles = (n_pairs + N_EXPERTS * (tm - 1)) // tm
    hot = top_idx[:, :, None] == jnp.arange(N_EXPERTS, dtype=jnp.int32)[None, None, :]
    per_token = jnp.sum(hot, axis=1, dtype=jnp.int32)
    csum = jnp.cumsum(per_token, axis=0)
    counts = csum[-1]
    tiles_per = (counts + tm - 1) // tm
    tile_end = jnp.cumsum(tiles_per)
    row_start = (tile_end - tiles_per) * tm
    slot = (csum - per_token) + row_start[None, :]
    pos = jnp.sum(jnp.where(hot, slot[:, None, :], 0), axis=2).reshape(n_pairs).astype(jnp.int32)
    tile_ids = jnp.arange(n_tiles, dtype=jnp.int32)
    tile_expert = jnp.sum((tile_end[None, :] <= tile_ids[:, None]).astype(jnp.int32), axis=1)
    tile_expert = jnp.minimum(tile_expert, N_EXPERTS - 1).astype(jnp.int32)
    in_expert = tile_ids - (tile_end - tiles_per)[tile_expert]
    tile_valid = jnp.clip(counts[tile_expert] - in_expert * tm, 0, tm)
    tile_valid = jnp.where(tile_ids < tile_end[-1], tile_valid, 0).astype(jnp.int32)
    row_token = jnp.zeros((n_tiles * tm,), jnp.int32).at[pos].set(
        jnp.arange(n_pairs, dtype=jnp.int32) // TOP_K)
    return tile_expert, tile_valid, row_token, pos


def kernel(x, l0_mix_norm, l0_w_qkv, l0_q_norm, l0_k_norm, l0_sink, l0_w_o, l0_ffn_norm, l0_w_gate_up, l0_w_down, l1_mix_norm, l1_w_in, l1_v_norm, l1_w_s, l1_b_s, l1_w_out, l1_ffn_norm, l1_w_router, l1_we_gate, l1_we_up, l1_we_down):
    batch, seq, d = x.shape
    n = batch * seq
    x0 = x.reshape(n, d)
    dq = N_HEADS * HEAD_DIM
    dkv = N_KV_HEADS * HEAD_DIM

    head_gain = jnp.concatenate([
        jnp.tile(l0_q_norm * (HEAD_DIM ** -0.5), N_HEADS),
        jnp.tile(l0_k_norm, N_KV_HEADS),
        jnp.ones((dkv,), F32),
    ])[None, :]
    qkv, (w_gate_up, w_o, w_down) = _qkv_proj(
        x0, l0_mix_norm[None, :], l0_w_qkv.astype(BF16), head_gain, dq + dkv,
        casts=[(l0_w_gate_up, DENSE_TF, None, MXU_COLS), (l0_w_o, None, None), (l0_w_down, None, None)])
    attn, (we_gate,) = _attention(qkv, l0_sink, batch, seq, casts=[(l1_we_gate, MOE_TF, None)])
    x1, _ = _proj_residual(attn, w_o, x0, casts=[], name="attn_out_proj")

    x2, (we_down, w_in, w_out) = _dense_ffn(
        x1, l0_ffn_norm[None, :], w_gate_up, w_down,
        casts=[(l1_we_down, None, 256), (l1_w_in, l1_w_in.shape[1] // 2, 16), (l1_w_out, None, 16)])

    u, v, (we_up,) = _gmlp_in(x2, l1_mix_norm[None, :], w_in, l1_v_norm[None, :],
                              casts=[(l1_we_up, MOE_TF, None)])
    b_full = jnp.repeat(l1_b_s.T, LANES, axis=1)
    w_router = jnp.pad(l1_w_router, ((0, 0), (0, LANES - N_EXPERTS)))
    w_router_hi = w_router.astype(BF16)
    w_router_lo = (w_router - w_router_hi.astype(F32)).astype(BF16)
    w_router_split = jnp.concatenate([w_router_hi, w_router_lo], axis=1)
    x3, h, ridx, rgate = _gmlp_out_router(u, v, l1_w_s.astype(BF16), b_full, w_out, x2,
                                          l1_ffn_norm[None, :], w_router_split)

    tm = MOE_TM
    tile_expert, tile_valid, row_token, pos = _route_plan(ridx[:, :TOP_K], tm)
    y_sorted = _moe_ffn(h, tile_expert, tile_valid, row_token, we_gate, we_up, we_down, tm)
    out = _moe_combine(x3, rgate, y_sorted, pos)
    return out.reshape(batch, seq, d)
```
